```python
import math
import jax, jax.numpy as jnp
from jax import lax
import numpy as np

D_MODEL = 1024
BATCH = 8
SEQ = 4096
DEPTH = 2

GRID_W = 64
CTX_LEN = 256
NORM_EPS = 1e-6
N_MOD = 9

D_FF = 2816
MACARON_W = 0.5

D_LRU = 256
LRU_BLOCKS = 4
LRU_BS = D_LRU // LRU_BLOCKS
LRU_CONV = 4
LRU_PAD = (2, 1)
LRU_C = 8.0

D_HY = 256
HY_CONV = 3
HY_PAD = (1, 1)
HY_EMB = 33
HY_BANDS = (HY_EMB - 1) // 2
HY_HID = 64
HY_INNER = 2
HY_FAST = 0.3
HY_SLOW = 1.5
HY_TARGET = 1e-2

N_QH = 8
N_KVH = 2
HEAD_DIM = 64
Q_PER_KV = N_QH // N_KVH
D_ATTN = N_QH * HEAD_DIM
WINDOW = 128
BLOCK = 128
ROPE_THETA = 10000.0
ROPE_PAIRS_AXIS = HEAD_DIM // 4
NEG_INF = -1e30

D_IN = 2 * D_LRU + 3 * D_HY + (N_QH + 2 * N_KVH) * HEAD_DIM
D_CAT = D_LRU + D_HY + D_ATTN
SPLITS = np.cumsum([D_LRU, D_LRU, 3 * D_HY, N_QH * HEAD_DIM, N_KVH * HEAD_DIM]).tolist()

kernel_name = "hybrid_lru_hyena_swa_dit_block"

F32 = jnp.float32


def rmsnorm(x, g):
    xf = x.astype(F32)
    y = xf * lax.rsqrt(jnp.mean(xf * xf, axis=-1, keepdims=True) + NORM_EPS)
    return (y * g.astype(F32)).astype(x.dtype)


def ada_norm(x, g, shift, scale):
    return rmsnorm(x, g) * (1 + scale) + shift


def swiglu(h, w1, w2):
    a, b = jnp.split(h @ w1, 2, axis=-1)
    return (jax.nn.silu(a) * b) @ w2


def dwconv(x, w, b, pad):
    L = x.shape[1]
    xp = jnp.pad(x, ((0, 0), pad, (0, 0)))
    out = b
    for k in range(w.shape[0]):
        out = out + xp[:, k:k + L] * w[k]
    return out


def rglru_coeffs(u, wa, ba, wx, bx, lam):
    ub = u.reshape(u.shape[:-1] + (LRU_BLOCKS, LRU_BS))
    r = jax.nn.sigmoid(jnp.einsum('blnd,nde->blne', ub, wa.astype(F32)).reshape(u.shape) + ba.astype(F32))
    i = jax.nn.sigmoid(jnp.einsum('blnd,nde->blne', ub, wx.astype(F32)).reshape(u.shape) + bx.astype(F32))
    log_a = -LRU_C * r * jax.nn.softplus(-lam.astype(F32))
    a = jnp.exp(log_a)
    b = jnp.sqrt(-jnp.expm1(2.0 * log_a)) * (i * u)
    return a, b


def _combine(e1, e2):
    a1, b1 = e1
    a2, b2 = e2
    return a1 * a2, a2 * b1 + b2


def linear_scan(a, b, h0):
    a_cum, b_cum = lax.associative_scan(_combine, (a, b), axis=1)
    return a_cum * h0[:, None] + b_cum


def rglru_mixer(xl, gl, xlc, glc, conv_w, conv_b, wa, ba, wx, bx, lam, need_ctx):
    u = dwconv(xl, conv_w, conv_b, LRU_PAD).astype(F32)
    uc = dwconv(xlc, conv_w, conv_b, LRU_PAD).astype(F32)
    hs, hcs = [], []
    for d in range(2):
        ac, bc = rglru_coeffs(uc, wa[d], ba[d], wx[d], bx[d], lam[d])
        ax, bxl = rglru_coeffs(u, wa[d], ba[d], wx[d], bx[d], lam[d])
        if d == 1:
            ac, bc, ax, bxl = (jnp.flip(ac, 1), jnp.flip(bc, 1), jnp.flip(ax, 1), jnp.flip(bxl, 1))
        hc = linear_scan(ac, bc, jnp.zeros_like(ac[:, 0]))
        hx = linear_scan(ax, bxl, hc[:, -1])
        if d == 1:
            hc, hx = jnp.flip(hc, 1), jnp.flip(hx, 1)
        hs.append(hx)
        hcs.append(hc)
    y = ((hs[0] + hs[1]) * jax.nn.gelu(gl.astype(F32))).astype(xl.dtype)
    if not need_ctx:
        return y, None
    yc = ((hcs[0] + hcs[1]) * jax.nn.gelu(glc.astype(F32))).astype(xlc.dtype)
    return y, yc


def hyena_filter(L, fw0, fb0, fw_in, fb_in, freq, fw_last):
    t = jnp.linspace(0.0, 1.0, L, dtype=F32)[:, None]
    w = 2.0 * math.pi * jnp.arange(L, dtype=F32)[:, None] / L
    f = jnp.linspace(1e-4, HY_BANDS - 1, HY_BANDS, dtype=F32)[None, :]
    z = jnp.concatenate([t, jnp.cos(f * w), -jnp.sin(f * w)], axis=-1)
    fr = freq.astype(F32)
    hdn = jnp.sin(fr * (z @ fw0.astype(F32) + fb0.astype(F32)))
    for j in range(HY_INNER):
        hdn = jnp.sin(fr * (hdn @ fw_in[j].astype(F32) + fb_in[j].astype(F32)))
    k = hdn @ fw_last.astype(F32)
    max_decay = math.log(HY_TARGET) / HY_FAST
    min_decay = math.log(HY_TARGET) / HY_SLOW
    deltas = jnp.abs(jnp.linspace(min_decay, max_decay, D_HY, dtype=F32))
    decay = jnp.exp(-t * deltas)
    k_fwd = k[:, :D_HY] * decay
    k_bwd = k[:, D_HY:] * decay
    return jnp.concatenate([k_fwd, jnp.zeros((1, D_HY), F32), jnp.flip(k_bwd[1:], axis=0)], axis=0)


def hyena_op(z, conv_w, conv_b, fw0, fb0, fw_in, fb_in, freq, fw_last, skip):
    L = z.shape[1]
    zc = dwconv(z, conv_w, conv_b, HY_PAD).astype(F32)
    x0, x1, v = jnp.split(zc, 3, axis=-1)
    k = hyena_filter(L, fw0, fb0, fw_in, fb_in, freq, fw_last)
    u = x1 * v
    n = 2 * L
    y = jnp.fft.irfft(jnp.fft.rfft(u, n=n, axis=1) * jnp.fft.rfft(k, n=n, axis=0)[None], n=n, axis=1)[:, :L]
    y = y + u * skip.astype(F32)
    return (x0 * y).astype(z.dtype)


def rope_tables(rows):
    r = jnp.repeat(jnp.arange(rows, dtype=F32), GRID_W)
    col = jnp.tile(jnp.arange(GRID_W, dtype=F32), rows)
    inv = ROPE_THETA ** (-jnp.arange(ROPE_PAIRS_AXIS, dtype=F32) / ROPE_PAIRS_AXIS)
    ang = jnp.concatenate([r[:, None] * inv, col[:, None] * inv], axis=-1)
    return jnp.cos(ang), jnp.sin(ang)


def apply_rope(x, cos, sin):
    xf = x.astype(F32)
    x1, x2 = jnp.split(xf, 2, axis=-1)
    cc = cos[None, :, None]
    ss = sin[None, :, None]
    return jnp.concatenate([x1 * cc - x2 * ss, x1 * ss + x2 * cc], axis=-1).astype(x.dtype)


def sink_attend(s, vals, sink):
    sk = sink.astype(F32)[None, :, :, None, None]
    m = jnp.maximum(jnp.max(s, axis=-1, keepdims=True), sk)
    p = jnp.exp(s - m)
    denom = jnp.sum(p, axis=-1, keepdims=True) + jnp.exp(sk - m)
    return jnp.einsum('bhgqk,bkhd->bqhgd', p / denom, vals.astype(F32))


def window_attention(q, k, v, kc, vc, sink):
    B, L = q.shape[0], q.shape[1]
    nb = L // BLOCK
    span = BLOCK + 2 * WINDOW
    scale = HEAD_DIM ** -0.5
    qb = q.reshape(B, nb, BLOCK, N_KVH, Q_PER_KV, HEAD_DIM).swapaxes(0, 1)
    kp = jnp.pad(k, ((0, 0), (WINDOW, WINDOW), (0, 0), (0, 0)))
    vp = jnp.pad(v, ((0, 0), (WINDOW, WINDOW), (0, 0), (0, 0)))
    sink_g = sink.reshape(N_KVH, Q_PER_KV)
    n_ctx = kc.shape[1]

    def block_fn(args):
        qi, bi = args
        start = bi * BLOCK
        kw = lax.dynamic_slice_in_dim(kp, start, span, axis=1)
        vw = lax.dynamic_slice_in_dim(vp, start, span, axis=1)
        keys = jnp.concatenate([kw, kc], axis=1)
        vals = jnp.concatenate([vw, vc], axis=1)
        s = jnp.einsum('bqhgd,bkhd->bhgqk', qi, keys).astype(F32) * scale
        qpos = start + jnp.arange(BLOCK)
        kpos = start - WINDOW + jnp.arange(span)
        valid = (jnp.abs(qpos[:, None] - kpos[None, :]) <= WINDOW) & (kpos >= 0)[None] & (kpos < L)[None]
        valid = jnp.concatenate([valid, jnp.ones((BLOCK, n_ctx), dtype=bool)], axis=1)
        s = jnp.where(valid, s, NEG_INF)
        return sink_attend(s, vals, sink_g)

    o = lax.map(block_fn, (qb, jnp.arange(nb)))
    return o.swapaxes(0, 1).reshape(B, L, D_ATTN)


def context_attention(qc, kc, vc, sink):
    B, C = qc.shape[0], qc.shape[1]
    qg = qc.reshape(B, C, N_KVH, Q_PER_KV, HEAD_DIM)
    s = jnp.einsum('bqhgd,bkhd->bhgqk', qg, kc).astype(F32) * (HEAD_DIM ** -0.5)
    return sink_attend(s, vc, sink.reshape(N_KVH, Q_PER_KV)).reshape(B, C, D_ATTN)


def token_mixer(h, hc, cos, sin, w_in, w_out, lru_conv_w, lru_conv_b, lru_wa, lru_ba, lru_wx, lru_bx,
                lru_lam, hy_conv_w, hy_conv_b, hy_fw0, hy_fb0, hy_fw_in, hy_fb_in, hy_freq, hy_fw_last,
                hy_skip, attn_sink, need_ctx):
    B, L = h.shape[0], h.shape[1]
    C = hc.shape[1]
    xl, gl, zh, q, k, v = jnp.split(h @ w_in, SPLITS, axis=-1)
    xlc, glc, zhc, qc, kc, vc = jnp.split(hc @ w_in, SPLITS, axis=-1)

    y_lru, yc_lru = rglru_mixer(xl, gl, xlc, glc, lru_conv_w, lru_conv_b, lru_wa, lru_ba,
                                lru_wx, lru_bx, lru_lam, need_ctx)
    y_hy = hyena_op(zh, hy_conv_w, hy_conv_b, hy_fw0, hy_fb0, hy_fw_in, hy_fb_in, hy_freq, hy_fw_last, hy_skip)

    q = apply_rope(q.reshape(B, L, N_QH, HEAD_DIM), cos, sin)
    k = apply_rope(k.reshape(B, L, N_KVH, HEAD_DIM), cos, sin)
    v = v.reshape(B, L, N_KVH, HEAD_DIM)
    kc = kc.reshape(B, C, N_KVH, HEAD_DIM)
    vc = vc.reshape(B, C, N_KVH, HEAD_DIM)
    y_att = window_attention(q, k, v, kc, vc, attn_sink).astype(h.dtype)

    y = jnp.concatenate([y_lru, y_hy, y_att], axis=-1) @ w_out
    if not need_ctx:
        return y, None
    yc_hy = hyena_op(zhc, hy_conv_w, hy_conv_b, hy_fw0, hy_fb0, hy_fw_in, hy_fb_in, hy_freq, hy_fw_last, hy_skip)
    yc_att = context_attention(qc.reshape(B, C, N_QH, HEAD_DIM), kc, vc, attn_sink).astype(hc.dtype)
    yc = jnp.concatenate([yc_lru, yc_hy, yc_att], axis=-1) @ w_out
    return y, yc


def setup_inputs(seed: int = 0) -> dict:
    key = jax.random.key(seed)
    ks = jax.random.split(key, 32)
    D = D_MODEL

    def nrm(k, shape, s):
        return jax.random.normal(k, shape, F32) * s

    u = jax.random.uniform(ks[17], (DEPTH, 2, D_LRU), F32, 0.9, 0.999)
    a = u ** (1.0 / LRU_C)
    lam = jnp.log(a) - jnp.log1p(-a)
    return {
        "x": nrm(ks[0], (BATCH, SEQ, D), 1.0),
        "c": nrm(ks[1], (BATCH, D), 1.0),
        "ctx": nrm(ks[2], (BATCH, CTX_LEN, D), 1.0),
        "c_ctx": nrm(ks[3], (D,), 1.0),
        "w_mod": nrm(ks[4], (DEPTH, D, N_MOD * D), 0.5 * D ** -0.5),
        "b_mod": nrm(ks[5], (DEPTH, N_MOD * D), 0.02),
        "norm_g": 1.0 + nrm(ks[6], (DEPTH, 3, D), 0.05),
        "ffn_w1": nrm(ks[7], (DEPTH, 2, D, 2 * D_FF), D ** -0.5),
        "ffn_w2": nrm(ks[8], (DEPTH, 2, D_FF, D), D_FF ** -0.5),
        "w_in": nrm(ks[9], (DEPTH, D, D_IN), D ** -0.5),
        "w_out": nrm(ks[10], (DEPTH, D_CAT, D), D_CAT ** -0.5),
        "lru_conv_w": nrm(ks[11], (DEPTH, LRU_CONV, D_LRU), LRU_CONV ** -0.5),
        "lru_conv_b": nrm(ks[12], (DEPTH, D_LRU), 0.02),
        "lru_wa": nrm(ks[13], (DEPTH, 2, LRU_BLOCKS, LRU_BS, LRU_BS), LRU_BS ** -0.5),
        "lru_ba": nrm(ks[14], (DEPTH, 2, D_LRU), 0.02),
        "lru_wx": nrm(ks[15], (DEPTH, 2, LRU_BLOCKS, LRU_BS, LRU_BS), LRU_BS ** -0.5),
        "lru_bx": nrm(ks[16], (DEPTH, 2, D_LRU), 0.02),
        "lru_lam": lam,
        "hy_conv_w": nrm(ks[18], (DEPTH, HY_CONV, 3 * D_HY), HY_CONV ** -0.5),
        "hy_conv_b": nrm(ks[19], (DEPTH, 3 * D_HY), 0.02),
        "hy_fw0": nrm(ks[20], (DEPTH, HY_EMB, HY_HID), HY_EMB ** -0.5),
        "hy_fb0": nrm(ks[21], (DEPTH, HY_HID), 0.1),
        "hy_fw_in": nrm(ks[22], (DEPTH, HY_INNER, HY_HID, HY_HID), HY_HID ** -0.5),
        "hy_fb_in": nrm(ks[23], (DEPTH, HY_INNER, HY_HID), 0.1),
        "hy_freq": 1.0 + nrm(ks[24], (DEPTH, HY_HID), 0.05),
        "hy_fw_last": nrm(ks[25], (DEPTH, HY_HID, 2 * D_HY), 0.1 * HY_HID ** -0.5),
        "hy_skip": nrm(ks[26], (DEPTH, D_HY), 0.5),
        "attn_sink": nrm(ks[27], (DEPTH, N_QH), 0.5),
        "final_g": 1.0 + nrm(ks[28], (D,), 0.05),
    }


def reference(x, c, ctx, c_ctx, w_mod, b_mod, norm_g, ffn_w1, ffn_w2, w_in, w_out, lru_conv_w, lru_conv_b,
              lru_wa, lru_ba, lru_wx, lru_bx, lru_lam, hy_conv_w, hy_conv_b, hy_fw0, hy_fb0, hy_fw_in,
              hy_fb_in, hy_freq, hy_fw_last, hy_skip, attn_sink, final_g):
    B = x.shape[0]
    ROWS = x.shape[1] // GRID_W
    cos, sin = rope_tables(ROWS)
    s_lat = jax.nn.silu(c)
    s_ctx = jax.nn.silu(c_ctx)
    xc = ctx
    for l in range(DEPTH):
        need_ctx = l < DEPTH - 1
        mod = (s_lat @ w_mod[l] + b_mod[l]).reshape(B, N_MOD, 1, D_MODEL)
        modc = (s_ctx @ w_mod[l] + b_mod[l]).reshape(N_MOD, D_MODEL)
        m = [mod[:, i] for i in range(N_MOD)]
        mc = [modc[i] for i in range(N_MOD)]

        x = x + MACARON_W * m[2] * swiglu(ada_norm(x, norm_g[l, 0], m[0], m[1]), ffn_w1[l, 0], ffn_w2[l, 0])
        xc = xc + MACARON_W * mc[2] * swiglu(ada_norm(xc, norm_g[l, 0], mc[0], mc[1]), ffn_w1[l, 0], ffn_w2[l, 0])

        h = ada_norm(x, norm_g[l, 1], m[3], m[4])
        hc = ada_norm(xc, norm_g[l, 1], mc[3], mc[4])
        y, yc = token_mixer(h, hc, cos, sin, w_in[l], w_out[l], lru_conv_w[l], lru_conv_b[l], lru_wa[l],
                            lru_ba[l], lru_wx[l], lru_bx[l], lru_lam[l], hy_conv_w[l], hy_conv_b[l], hy_fw0[l],
                            hy_fb0[l], hy_fw_in[l], hy_fb_in[l], hy_freq[l], hy_fw_last[l], hy_skip[l],
                            attn_sink[l], need_ctx)
        x = x + m[5] * y

        x = x + MACARON_W * m[8] * swiglu(ada_norm(x, norm_g[l, 2], m[6], m[7]), ffn_w1[l, 1], ffn_w2[l, 1])
        if need_ctx:
            xc = xc + mc[5] * yc
            xc = xc + MACARON_W * mc[8] * swiglu(ada_norm(xc, norm_g[l, 2], mc[6], mc[7]), ffn_w1[l, 1], ffn_w2[l, 1])
    return rmsnorm(x, final_g)
```

```python
import functools
import math

import jax
import jax.numpy as jnp
from jax import lax
from jax.experimental import pallas as pl
from jax.experimental.pallas import tpu as pltpu

F32 = jnp.float32
BF16 = jnp.bfloat16

NORM_EPS = 1e-6
N_MOD = 9
MACARON_W = 0.5
D_LRU = 256
LRU_BLOCKS = 4
LRU_C = 8.0
LRU_LEFT = 2
D_HY = 256
HY_LEFT = 1
HY_EMB = 33
HY_BANDS = (HY_EMB - 1) // 2
HY_FAST = 0.3
HY_SLOW = 1.5
HY_TARGET = 1e-2
N_QH = 8
N_KVH = 2
HEAD_DIM = 64
D_ATTN = N_QH * HEAD_DIM
D_KV = N_KVH * HEAD_DIM
WINDOW = 128
ATT_BLOCK = 128
GRID_W = 64
ROPE_THETA = 10000.0
ROPE_PAIRS_AXIS = HEAD_DIM // 4
NEG_INF = -1e30

LANES = 128
SUBLANES = 8
VMEM_LIMIT_BYTES = 56 * 1024 * 1024
ROW_TILE = 512
DFT_TILE = 512


def _cparams(*sem):
    return pltpu.CompilerParams(dimension_semantics=sem, vmem_limit_bytes=VMEM_LIMIT_BYTES)


def _row_tile(rows):
    return min(ROW_TILE, rows)


def _resident(shape):
    return pl.BlockSpec(shape, lambda *_: (0,) * len(shape), pipeline_mode=pl.Buffered(1))


def _ada_norm(x, g, shift, scale):
    y = x * lax.rsqrt(jnp.mean(x * x, axis=-1, keepdims=True) + NORM_EPS)
    return (y * g) * (1.0 + scale) + shift


def _sigmoid(x):
    return 1.0 / (1.0 + jnp.exp(-x))


def _gelu_tanh(x):
    return 0.5 * x * (1.0 + jnp.tanh(math.sqrt(2.0 / math.pi) * (x + 0.044715 * (x * x * x))))


def _softplus(x):
    return jnp.maximum(x, 0.0) + jnp.log1p(jnp.exp(-jnp.abs(x)))


def _dwconv_rows(x, w, bias, left):
    rows = x.shape[0]
    row = lax.broadcasted_iota(jnp.int32, x.shape, 0)
    out = jnp.broadcast_to(bias, x.shape)
    for k in range(w.shape[0]):
        off = k - left
        if off == 0:
            term = x
        else:
            shifted = pltpu.roll(x, (-off) % rows, axis=0)
            ok = (row + off >= 0) & (row + off < rows)
            term = jnp.where(ok, shifted, 0.0)
        out = out + term * w[k:k + 1, :]
    return out


def _mod_kernel(c_ref, w_ref, b_ref, o_ref):
    cv = c_ref[...]
    s = cv * _sigmoid(cv)
    o_ref[0] = jnp.dot(s, w_ref[0], preferred_element_type=F32,
                       precision=lax.Precision.HIGHEST) + b_ref[0]


def _modulation(c_rows, w_mod, b_mod):
    depth, d, nd = w_mod.shape
    rows = c_rows.shape[0]
    tn = nd // 8
    return pl.pallas_call(
        _mod_kernel,
        grid=(depth, nd // tn),
        in_specs=[
            pl.BlockSpec((rows, d), lambda l, j: (0, 0)),
            pl.BlockSpec((1, d, tn), lambda l, j: (l, 0, j)),
            pl.BlockSpec((1, 1, tn), lambda l, j: (l, 0, j)),
        ],
        out_specs=pl.BlockSpec((1, rows, tn), lambda l, j: (l, 0, j)),
        out_shape=jax.ShapeDtypeStruct((depth, rows, nd), F32),
        compiler_params=_cparams("parallel", "parallel"),
    )(c_rows, w_mod, b_mod.reshape(depth, 1, nd))


def _ffn_kernel(x_ref, mod_ref, g_ref, w1_ref, w2_ref, o_ref, *, i_mod, d_ff):
    x = x_ref[0]
    m = mod_ref[...]
    h = _ada_norm(x, g_ref[...], m[i_mod:i_mod + 1], m[i_mod + 1:i_mod + 2])
    ab = jnp.dot(h.astype(BF16), w1_ref[...], preferred_element_type=F32)
    a = ab[:, :d_ff]
    b = ab[:, d_ff:]
    gated = (a * _sigmoid(a)) * b
    y = jnp.dot(gated.astype(BF16), w2_ref[...], preferred_element_type=F32)
    o_ref[0] = x + (MACARON_W * m[i_mod + 2:i_mod + 3]) * y


def _ffn(x, mod, mod_row, g, w1, w2, i_mod):
    bsz, rows, d = x.shape
    d_ff = w2.shape[0]
    tm = _row_tile(rows)
    return pl.pallas_call(
        functools.partial(_ffn_kernel, i_mod=i_mod, d_ff=d_ff),
        grid=(bsz, rows // tm),
        in_specs=[
            pl.BlockSpec((1, tm, d), lambda b, t: (b, t, 0)),
            pl.BlockSpec((None, N_MOD, d), lambda b, t: (mod_row(b), 0, 0)),
            pl.BlockSpec((1, d), lambda b, t: (0, 0)),
            _resident((d, 2 * d_ff)),
            _resident((d_ff, d)),
        ],
        out_specs=pl.BlockSpec((1, tm, d), lambda b, t: (b, t, 0)),
        out_shape=jax.ShapeDtypeStruct(x.shape, F32),
        compiler_params=_cparams("parallel", "parallel"),
    )(x, mod, g.reshape(1, d), w1, w2)


def _rope(x, cos_t, sin_a, sin_b):
    width = x.shape[-1]
    half = HEAD_DIM // 2
    up = pltpu.roll(x, width - half, axis=1)
    dn = pltpu.roll(x, half, axis=1)
    return x * cos_t + up * sin_a + dn * sin_b


def _proj_kernel(*refs, rope):
    if rope:
        (x_ref, mod_ref, g_ref, w_ref, cos_ref, sa_ref, sb_ref,
         xl_ref, gl_ref, zh_ref, q_ref, k_ref, v_ref) = refs
    else:
        (x_ref, mod_ref, g_ref, w_ref, xl_ref, gl_ref, zh_ref, q_ref, k_ref, v_ref) = refs
    x = x_ref[0]
    m = mod_ref[...]
    h = _ada_norm(x, g_ref[...], m[3:4], m[4:5])
    z = jnp.dot(h.astype(BF16), w_ref[...], preferred_element_type=F32)
    o = 0
    xl_ref[...] = z[:, o:o + D_LRU]; o += D_LRU
    gl_ref[...] = z[:, o:o + D_LRU]; o += D_LRU
    zh_ref[...] = z[:, o:o + 3 * D_HY]; o += 3 * D_HY
    q = z[:, o:o + D_ATTN]; o += D_ATTN
    k = z[:, o:o + D_KV]; o += D_KV
    v = z[:, o:o + D_KV]
    if rope:
        cos_t, sin_a, sin_b = cos_ref[...], sa_ref[...], sb_ref[...]
        q = _rope(q, cos_t, sin_a, sin_b)
        k = _rope(k, cos_t[:, :D_KV], sin_a[:, :D_KV], sin_b[:, :D_KV])
    q_ref[0] = q * (HEAD_DIM ** -0.5)
    k_ref[0] = k
    v_ref[0] = v


def _input_proj(x, mod, mod_row, g, w_in, rope_tabs):
    bsz, rows, d = x.shape
    tm = _row_tile(rows)
    rope = rope_tabs is not None
    in_specs = [
        pl.BlockSpec((1, tm, d), lambda b, t: (b, t, 0)),
        pl.BlockSpec((None, N_MOD, d), lambda b, t: (mod_row(b), 0, 0)),
        pl.BlockSpec((1, d), lambda b, t: (0, 0)),
        _resident(w_in.shape),
    ]
    args = [x, mod, g.reshape(1, d), w_in]
    if rope:
        in_specs += [pl.BlockSpec((tm, D_ATTN), lambda b, t: (t, 0))] * 3
        args += list(rope_tabs)
    out_shape = [
        jax.ShapeDtypeStruct((rows, bsz * D_LRU), F32),
        jax.ShapeDtypeStruct((rows, bsz * D_LRU), F32),
        jax.ShapeDtypeStruct((rows, bsz * 3 * D_HY), F32),
        jax.ShapeDtypeStruct((bsz, rows, D_ATTN), F32),
        jax.ShapeDtypeStruct((bsz, rows, D_KV), F32),
        jax.ShapeDtypeStruct((bsz, rows, D_KV), F32),
    ]
    out_specs = [
        pl.BlockSpec((tm, D_LRU), lambda b, t: (t, b)),
        pl.BlockSpec((tm, D_LRU), lambda b, t: (t, b)),
        pl.BlockSpec((tm, 3 * D_HY), lambda b, t: (t, b)),
        pl.BlockSpec((1, tm, D_ATTN), lambda b, t: (b, t, 0)),
        pl.BlockSpec((1, tm, D_KV), lambda b, t: (b, t, 0)),
        pl.BlockSpec((1, tm, D_KV), lambda b, t: (b, t, 0)),
    ]
    return pl.pallas_call(
        functools.partial(_proj_kernel, rope=rope),
        grid=(bsz, rows // tm),
        in_specs=in_specs,
        out_specs=out_specs,
        out_shape=out_shape,
        compiler_params=_cparams("parallel", "parallel"),
    )(*args)


def _scan_chunk(a, b, row, reverse):
    for s in (1, 2, 4):
        if reverse:
            ok = row < SUBLANES - s
            sh = SUBLANES - s
        else:
            ok = row >= s
            sh = s
        a_sh = pltpu.roll(a, sh, axis=0)
        b_sh = pltpu.roll(b, sh, axis=0)
        b = jnp.where(ok, a * b_sh + b, b)
        a = jnp.where(ok, a * a_sh, a)
    return a, b


def _lru_kernel(xl_ref, gl_ref, xc_ref, gc_ref, cw_ref, cb_ref, wa_ref, ba_ref, wx_ref, bx_ref,
                lam_ref, y_ref, yc_ref, a_s, b_s, *, n_lat, n_ctx, tile):
    width = xl_ref.shape[1]
    cw = cw_ref[...]
    cb = cb_ref[...]
    y_ref[...] = _dwconv_rows(xl_ref[...], cw, cb, LRU_LEFT)
    yc_ref[...] = _dwconv_rows(xc_ref[...], cw, cb, LRU_LEFT)

    def coeffs(u_ref, base, n_rows):
        def body(i, carry):
            r0 = pl.multiple_of(i * tile, tile)
            u = u_ref[pl.ds(r0, tile), :]
            ub = u.astype(BF16)
            for d in range(2):
                r = _sigmoid(jnp.dot(ub, wa_ref[d], preferred_element_type=F32) + ba_ref[d])
                ig = _sigmoid(jnp.dot(ub, wx_ref[d], preferred_element_type=F32) + bx_ref[d])
                log_a = (-LRU_C * r) * _softplus(-lam_ref[d])
                dst = pl.ds(pl.multiple_of(base + r0, SUBLANES), tile)
                a_s[d, dst, :] = jnp.exp(log_a)
                th = jnp.tanh(log_a)
                b_s[d, dst, :] = jnp.sqrt((-2.0 * th) / (1.0 - th)) * (ig * u)
            return carry
        lax.fori_loop(0, n_rows // tile, body, 0)

    coeffs(yc_ref, 0, n_ctx)
    coeffs(y_ref, n_ctx, n_lat)

    row = lax.broadcasted_iota(jnp.int32, (SUBLANES, width), 0)

    def step(d, chunk, h, reverse):
        r0 = pl.multiple_of(chunk * SUBLANES, SUBLANES)
        a, b = _scan_chunk(a_s[d, pl.ds(r0, SUBLANES), :], b_s[d, pl.ds(r0, SUBLANES), :], row, reverse)
        hh = a * h + b
        b_s[d, pl.ds(r0, SUBLANES), :] = hh
        return hh[0:1, :] if reverse else hh[SUBLANES - 1:SUBLANES, :]

    nc_ctx = n_ctx // SUBLANES
    nc_all = (n_ctx + n_lat) // SUBLANES
    h0 = jnp.zeros((1, width), F32)
    lax.fori_loop(0, nc_all, lambda j, h: step(0, j, h, False), h0)
    hb = lax.fori_loop(0, nc_ctx, lambda j, h: step(1, nc_ctx - 1 - j, h, True), h0)
    lax.fori_loop(0, nc_all - nc_ctx, lambda j, h: step(1, nc_all - 1 - j, h, True), hb)

    def finish(o_ref, g_ref, base, n_rows):
        def body(i, carry):
            r0 = pl.multiple_of(i * tile, tile)
            src = pl.ds(pl.multiple_of(base + r0, SUBLANES), tile)
            hsum = b_s[0, src, :] + b_s[1, src, :]
            o_ref[pl.ds(r0, tile), :] = hsum * _gelu_tanh(g_ref[pl.ds(r0, tile), :])
            return carry
        lax.fori_loop(0, n_rows // tile, body, 0)

    finish(yc_ref, gc_ref, 0, n_ctx)
    finish(y_ref, gl_ref, n_ctx, n_lat)


def _block_diag(w):
    two, nb, bs, _ = w.shape
    eye = jnp.eye(nb, dtype=w.dtype)
    return jnp.einsum('dnij,nm->dnimj', w, eye).reshape(two, nb * bs, nb * bs)


def _rglru(xl, gl, xlc, glc, conv_w, conv_b, wa, ba, wx, bx, lam, bsz):
    n_lat, n_ctx = xl.shape[0], xlc.shape[0]
    width = LANES
    per_b = D_LRU // width
    tile = math.gcd(256, math.gcd(n_lat, n_ctx))
    wa_bd = _block_diag(wa).astype(BF16)
    wx_bd = _block_diag(wx).astype(BF16)
    col = lambda b, j: (0, b * per_b + j)
    par = lambda b, j: (0, j)
    par3 = lambda b, j: (0, 0, j)
    return pl.pallas_call(
        functools.partial(_lru_kernel, n_lat=n_lat, n_ctx=n_ctx, tile=tile),
        grid=(bsz, per_b),
        in_specs=[
            pl.BlockSpec((n_lat, width), col),
            pl.BlockSpec((n_lat, width), col),
            pl.BlockSpec((n_ctx, width), col),
            pl.BlockSpec((n_ctx, width), col),
            pl.BlockSpec((conv_w.shape[0], width), par),
            pl.BlockSpec((1, width), par),
            pl.BlockSpec((2, width, width), lambda b, j: (0, j, j)),
            pl.BlockSpec((2, 1, width), par3),
            pl.BlockSpec((2, width, width), lambda b, j: (0, j, j)),
            pl.BlockSpec((2, 1, width), par3),
            pl.BlockSpec((2, 1, width), par3),
        ],
        out_specs=[pl.BlockSpec((n_lat, width), col), pl.BlockSpec((n_ctx, width), col)],
        out_shape=[jax.ShapeDtypeStruct(xl.shape, F32), jax.ShapeDtypeStruct(xlc.shape, F32)],
        scratch_shapes=[pltpu.VMEM((2, n_ctx + n_lat, width), F32),
                        pltpu.VMEM((2, n_ctx + n_lat, width), F32)],
        compiler_params=_cparams("parallel", "parallel"),
    )(xl, gl, xlc, glc, conv_w, conv_b.reshape(1, -1), wa_bd, ba.reshape(2, 1, -1), wx_bd,
      bx.reshape(2, 1, -1), lam.reshape(2, 1, -1))


def _hy_pre_kernel(z0_ref, z1_ref, z2_ref, w0_ref, w1_ref, w2_ref, b0_ref, b1_ref, b2_ref,
                   ub_ref, u_ref, x0_ref):
    x0 = _dwconv_rows(z0_ref[...], w0_ref[...], b0_ref[...], HY_LEFT)
    x1 = _dwconv_rows(z1_ref[...], w1_ref[...], b1_ref[...], HY_LEFT)
    v = _dwconv_rows(z2_ref[...], w2_ref[...], b2_ref[...], HY_LEFT)
    u = x1 * v
    u_ref[...] = u
    ub_ref[...] = u.astype(BF16)
    x0_ref[...] = x0


def _hyena_pre(zh, conv_w, conv_b, bsz):
    rows = zh.shape[0]
    width = LANES
    per_b = D_HY // width
    zspec = lambda part: pl.BlockSpec((rows, width), lambda b, j: (0, b * 3 * per_b + part * per_b + j))
    wspec = lambda part: pl.BlockSpec((conv_w.shape[0], width), lambda b, j: (0, part * per_b + j))
    bspec = lambda part: pl.BlockSpec((1, width), lambda b, j: (0, part * per_b + j))
    ospec = pl.BlockSpec((rows, width), lambda b, j: (0, b * per_b + j))
    cb = conv_b.reshape(1, -1)
    return pl.pallas_call(
        _hy_pre_kernel,
        grid=(bsz, per_b),
        in_specs=[zspec(0), zspec(1), zspec(2), wspec(0), wspec(1), wspec(2), bspec(0), bspec(1), bspec(2)],
        out_specs=[ospec, ospec, ospec],
        out_shape=[jax.ShapeDtypeStruct((rows, bsz * D_HY), BF16),
                   jax.ShapeDtypeStruct((rows, bsz * D_HY), F32),
                   jax.ShapeDtypeStruct((rows, bsz * D_HY), F32)],
        compiler_params=_cparams("parallel", "parallel"),
    )(zh, zh, zh, conv_w, conv_w, conv_w, cb, cb, cb)


def _hy_filter_kernel(z_ref, fw0_ref, fb0_ref, fwin_ref, fbin_ref, freq_ref, fwl_ref, dl_ref, o_ref, *, tile):
    hp = lax.Precision.HIGHEST
    z = z_ref[...]
    fr = freq_ref[...]
    hdn = jnp.sin(fr * (jnp.dot(z, fw0_ref[...], preferred_element_type=F32, precision=hp) + fb0_ref[...]))
    for j in range(fwin_ref.shape[0]):
        hdn = jnp.sin(fr * (jnp.dot(hdn, fwin_ref[j], preferred_element_type=F32, precision=hp) + fbin_ref[j]))
    k = jnp.dot(hdn, fwl_ref[...], preferred_element_type=F32, precision=hp)
    decay = jnp.exp(-z[:, 0:1] * dl_ref[...])
    k_fwd = k[:, :D_HY] * decay
    k_bwd = k[:, D_HY:] * decay
    row = lax.broadcasted_iota(jnp.int32, k_bwd.shape, 0) + pl.program_id(0) * tile
    k_bwd = jnp.where(row == 0, 0.0, k_bwd)
    o_ref[...] = jnp.concatenate([k_fwd, k_bwd], axis=-1).astype(BF16)


def _hyena_filter_taps(n, fw0, fb0, fw_in, fb_in, freq, fw_last):
    t = jnp.linspace(0.0, 1.0, n, dtype=F32)[:, None]
    w = 2.0 * math.pi * jnp.arange(n, dtype=F32)[:, None] / n
    f = jnp.linspace(1e-4, HY_BANDS - 1, HY_BANDS, dtype=F32)[None, :]
    z = jnp.concatenate([t, jnp.cos(f * w), -jnp.sin(f * w)], axis=-1)
    z = jnp.pad(z, ((0, 0), (0, LANES - HY_EMB)))
    fw0p = jnp.pad(fw0, ((0, LANES - HY_EMB), (0, 0)))
    max_decay = math.log(HY_TARGET) / HY_FAST
    min_decay = math.log(HY_TARGET) / HY_SLOW
    deltas = jnp.abs(jnp.linspace(min_decay, max_decay, D_HY, dtype=F32))[None, :]
    hid = fw0.shape[1]
    tile = min(512, n)
    full = lambda a: pl.BlockSpec(a.shape, lambda i: (0,) * a.ndim)
    args = [fw0p, fb0.reshape(1, hid), fw_in, fb_in.reshape(-1, 1, hid), freq.reshape(1, hid), fw_last, deltas]
    return pl.pallas_call(
        functools.partial(_hy_filter_kernel, tile=tile),
        grid=(n // tile,),
        in_specs=[pl.BlockSpec((tile, LANES), lambda i: (i, 0))] + [full(a) for a in args],
        out_specs=pl.BlockSpec((tile, 2 * D_HY), lambda i: (i, 0)),
        out_shape=jax.ShapeDtypeStruct((n, 2 * D_HY), BF16),
        compiler_params=_cparams("parallel"),
    )(z, *args)


def _dft_matrices(n):
    two_n = 2 * n
    idx = jnp.arange(n, dtype=jnp.int32)
    phase = (idx[:, None] * idx[None, :]) % two_n
    ang = phase.astype(F32) * (2.0 * math.pi / two_n)
    cosm = jnp.cos(ang)
    sinm = jnp.sin(ang)
    alt = jnp.where(idx % 2 == 0, 1.0, -1.0).astype(F32)
    first_row = (idx[:, None] == 0)
    first_col = (idx[None, :] == 0)
    cm = cosm
    sm = jnp.where(first_row, alt[None, :], -sinm)
    ci = jnp.where(first_col, 1.0 / two_n, cosm * (2.0 / two_n))
    si = jnp.where(first_col, alt[:, None] / two_n, -sinm * (2.0 / two_n))
    return cm.astype(BF16), sm.astype(BF16), ci.astype(BF16), si.astype(BF16)


def _dft_filter_kernel(cm_ref, sm_ref, kk_ref, kre_ref, kim_ref, *, tile):
    kk = kk_ref[...]
    xre = jnp.dot(cm_ref[...], kk, preferred_element_type=F32)
    xim = jnp.dot(sm_ref[...], kk, preferred_element_type=F32)
    row = lax.broadcasted_iota(jnp.int32, (tile, D_HY), 0) + pl.program_id(0) * tile
    kre_ref[...] = xre[:, :D_HY] + xre[:, D_HY:]
    kim_ref[...] = jnp.where(row == 0, xim[:, :D_HY] + xim[:, D_HY:], xim[:, :D_HY] - xim[:, D_HY:])


def _dft_filter(cm, sm, kk):
    n = cm.shape[0]
    tile = min(DFT_TILE, n)
    return pl.pallas_call(
        functools.partial(_dft_filter_kernel, tile=tile),
        grid=(n // tile,),
        in_specs=[pl.BlockSpec((tile, n), lambda f: (f, 0)),
                  pl.BlockSpec((tile, n), lambda f: (f, 0)),
                  pl.BlockSpec(kk.shape, lambda f: (0, 0))],
        out_specs=[pl.BlockSpec((tile, D_HY), lambda f: (f, 0))] * 2,
        out_shape=[jax.ShapeDtypeStruct((n, D_HY), F32)] * 2,
        compiler_params=_cparams("parallel"),
    )(cm, sm, kk)


def _dft_fwd_kernel(cm_ref, sm_ref, u_ref, kre_ref, kim_ref, yre_ref, yim_ref, *, tile):
    u = u_ref[...]
    xre = jnp.dot(cm_ref[...], u, preferred_element_type=F32)
    xim = jnp.dot(sm_ref[...], u, preferred_element_type=F32)
    kre = kre_ref[...]
    kim = kim_ref[...]
    row = lax.broadcasted_iota(jnp.int32, xre.shape, 0) + pl.program_id(0) * tile
    first = row == 0
    yre_ref[...] = (xre * kre - jnp.where(first, 0.0, xim * kim)).astype(BF16)
    yim_ref[...] = jnp.where(first, xim * kim, xre * kim + xim * kre).astype(BF16)


def _dft_fwd(cm, sm, ub, kre, kim):
    n = cm.shape[0]
    cols = ub.shape[1]
    tile = min(DFT_TILE, n)
    return pl.pallas_call(
        functools.partial(_dft_fwd_kernel, tile=tile),
        grid=(n // tile, cols // D_HY),
        in_specs=[pl.BlockSpec((tile, n), lambda f, c: (f, 0)),
                  pl.BlockSpec((tile, n), lambda f, c: (f, 0)),
                  pl.BlockSpec((n, D_HY), lambda f, c: (0, c)),
                  pl.BlockSpec((tile, D_HY), lambda f, c: (f, 0)),
                  pl.BlockSpec((tile, D_HY), lambda f, c: (f, 0))],
        out_specs=[pl.BlockSpec((tile, D_HY), lambda f, c: (f, c))] * 2,
        out_shape=[jax.ShapeDtypeStruct((n, cols), BF16)] * 2,
        compiler_params=_cparams("parallel", "parallel"),
    )(cm, sm, ub, kre, kim)


def _dft_inv_kernel(ci_ref, si_ref, yre_ref, yim_ref, u_ref, x0_ref, skip_ref, o_ref):
    y = (jnp.dot(ci_ref[...], yre_ref[...], preferred_element_type=F32)
         + jnp.dot(si_ref[...], yim_ref[...], preferred_element_type=F32))
    o_ref[...] = x0_ref[...] * (y + u_ref[...] * skip_ref[...])


def _dft_inv(ci, si, yre, yim, u, x0, skip):
    n = ci.shape[0]
    cols = yre.shape[1]
    tile = min(DFT_TILE, n)
    return pl.pallas_call(
        _dft_inv_kernel,
        grid=(n // tile, cols // D_HY),
        in_specs=[pl.BlockSpec((tile, n), lambda t, c: (t, 0)),
                  pl.BlockSpec((tile, n), lambda t, c: (t, 0)),
                  pl.BlockSpec((n, D_HY), lambda t, c: (0, c)),
                  pl.BlockSpec((n, D_HY), lambda t, c: (0, c)),
                  pl.BlockSpec((tile, D_HY), lambda t, c: (t, c)),
                  pl.BlockSpec((tile, D_HY), lambda t, c: (t, c)),
                  pl.BlockSpec((1, D_HY), lambda t, c: (0, 0))],
        out_specs=pl.BlockSpec((tile, D_HY), lambda t, c: (t, c)),
        out_shape=jax.ShapeDtypeStruct((n, cols), F32),
        compiler_params=_cparams("parallel", "parallel"),
    )(ci, si, yre, yim, u, x0, skip.reshape(1, D_HY))


def _hyena(zh, bsz, conv_w, conv_b, filt, dft, skip):
    cm, sm, ci, si = dft
    ub, u, x0 = _hyena_pre(zh, conv_w, conv_b, bsz)
    kk = _hyena_filter_taps(zh.shape[0], *filt)
    kre, kim = _dft_filter(cm, sm, kk)
    yre, yim = _dft_fwd(cm, sm, ub, kre, kim)
    return _dft_inv(ci, si, yre, yim, u, x0, skip)


def _attn_kernel(*refs, windowed):
    if windowed:
        (q_ref, kp_ref, kc_ref, kn_ref, vp_ref, vc_ref, vn_ref, kx_ref, vx_ref, sink_ref, o_ref) = refs
        keys = jnp.concatenate([kp_ref[0], kc_ref[0], kn_ref[0], kx_ref[0]], axis=0)
        vals = jnp.concatenate([vp_ref[0], vc_ref[0], vn_ref[0], vx_ref[0]], axis=0)
    else:
        (q_ref, kx_ref, vx_ref, sink_ref, o_ref) = refs
        keys = kx_ref[0]
        vals = vx_ref[0]
    q = q_ref[0]
    nq = q.shape[0]
    nk = keys.shape[0]
    if windowed:
        i = pl.program_id(1)
        last = pl.num_programs(1) - 1
        r = lax.broadcasted_iota(jnp.int32, (nq, nk), 0)
        j = lax.broadcasted_iota(jnp.int32, (nq, nk), 1)
        span = ATT_BLOCK + 2 * WINDOW
        in_band = (j >= r) & (j <= r + 2 * WINDOW)
        lo = jnp.where(i == 0, WINDOW, 0)
        hi = jnp.where(i == last, WINDOW + ATT_BLOCK, span)
        valid = (j >= span) | (in_band & (j >= lo) & (j < hi))
    lane = lax.broadcasted_iota(jnp.int32, keys.shape, 1)
    low = lane < HEAD_DIM
    k_lo = jnp.where(low, keys, 0.0)
    k_hi = jnp.where(low, 0.0, keys)
    v_lo = jnp.where(low, vals, 0.0)
    v_hi = jnp.where(low, 0.0, vals)
    k_var = ((k_lo.astype(BF16), pltpu.roll(k_lo, HEAD_DIM, axis=1).astype(BF16)),
             (pltpu.roll(k_hi, HEAD_DIM, axis=1).astype(BF16), k_hi.astype(BF16)))
    v_var = ((v_lo.astype(BF16), pltpu.roll(v_lo, HEAD_DIM, axis=1).astype(BF16)),
             (pltpu.roll(v_hi, HEAD_DIM, axis=1).astype(BF16), v_hi.astype(BF16)))
    q_per_kv = N_QH // N_KVH
    for c in range(D_ATTN // LANES):
        qc = q[:, c * LANES:(c + 1) * LANES].astype(BF16)
        acc = None
        for p in range(2):
            h = 2 * c + p
            g = h // q_per_kv
            s = lax.dot_general(qc, k_var[g][p], (((1,), (1,)), ((), ())), preferred_element_type=F32)
            if windowed:
                s = jnp.where(valid, s, NEG_INF)
            sk = sink_ref[h:h + 1, 0:1]
            m = jnp.maximum(jnp.max(s, axis=-1, keepdims=True), sk)
            e = jnp.exp(s - m)
            denom = jnp.sum(e, axis=-1, keepdims=True) + jnp.exp(sk - m)
            o = jnp.dot(e.astype(BF16), v_var[g][p], preferred_element_type=F32) / denom
            acc = o if acc is None else acc + o
        o_ref[0, :, c * LANES:(c + 1) * LANES] = acc


def _attention(q, k, v, kx, vx, sink, windowed):
    bsz, lq, _ = q.shape
    n_ctx = kx.shape[1]
    nb = lq // ATT_BLOCK
    sink_t = jnp.broadcast_to(sink.reshape(N_QH, 1), (N_QH, LANES))
    qspec = pl.BlockSpec((1, ATT_BLOCK, D_ATTN), lambda b, i: (b, i, 0))
    xspec = pl.BlockSpec((1, n_ctx, D_KV), lambda b, i: (b, 0, 0))
    sspec = pl.BlockSpec((N_QH, LANES), lambda b, i: (0, 0))
    if windowed:
        prev = pl.BlockSpec((1, ATT_BLOCK, D_KV), lambda b, i: (b, jnp.maximum(i - 1, 0), 0))
        cur = pl.BlockSpec((1, ATT_BLOCK, D_KV), lambda b, i: (b, i, 0))
        nxt = pl.BlockSpec((1, ATT_BLOCK, D_KV), lambda b, i: (b, jnp.minimum(i + 1, nb - 1), 0))
        in_specs = [qspec, prev, cur, nxt, prev, cur, nxt, xspec, xspec, sspec]
        args = (q, k, k, k, v, v, v, kx, vx, sink_t)
    else:
        in_specs = [qspec, xspec, xspec, sspec]
        args = (q, kx, vx, sink_t)
    return pl.pallas_call(
        functools.partial(_attn_kernel, windowed=windowed),
        grid=(bsz, nb),
        in_specs=in_specs,
        out_specs=pl.BlockSpec((1, ATT_BLOCK, D_ATTN), lambda b, i: (b, i, 0)),
        out_shape=jax.ShapeDtypeStruct(q.shape, F32),
        compiler_params=_cparams("parallel", "parallel"),
    )(*args)


def _out_kernel(x_ref, mod_ref, yl_ref, yh_ref, ya_ref, w_ref, o_ref):
    m = mod_ref[...]
    y = (jnp.dot(yl_ref[...].astype(BF16), w_ref[0:D_LRU, :], preferred_element_type=F32)
         + jnp.dot(yh_ref[...].astype(BF16), w_ref[D_LRU:D_LRU + D_HY, :], preferred_element_type=F32)
         + jnp.dot(ya_ref[0].astype(BF16), w_ref[D_LRU + D_HY:, :], preferred_element_type=F32))
    o_ref[0] = x_ref[0] + m[5:6] * y


def _out_proj(x, mod, mod_row, y_lru, y_hy, y_att, w_out):
    bsz, rows, d = x.shape
    tm = _row_tile(rows)
    return pl.pallas_call(
        _out_kernel,
        grid=(bsz, rows // tm),
        in_specs=[
            pl.BlockSpec((1, tm, d), lambda b, t: (b, t, 0)),
            pl.BlockSpec((None, N_MOD, d), lambda b, t: (mod_row(b), 0, 0)),
            pl.BlockSpec((tm, D_LRU), lambda b, t: (t, b)),
            pl.BlockSpec((tm, D_HY), lambda b, t: (t, b)),
            pl.BlockSpec((1, tm, D_ATTN), lambda b, t: (b, t, 0)),
            _resident(w_out.shape),
        ],
        out_specs=pl.BlockSpec((1, tm, d), lambda b, t: (b, t, 0)),
        out_shape=jax.ShapeDtypeStruct(x.shape, F32),
        compiler_params=_cparams("parallel", "parallel"),
    )(x, mod, y_lru, y_hy, y_att, w_out)


def _final_norm_kernel(x_ref, g_ref, o_ref):
    x = x_ref[0]
    o_ref[0] = (x * lax.rsqrt(jnp.mean(x * x, axis=-1, keepdims=True) + NORM_EPS)) * g_ref[...]


def _final_norm(x, g):
    bsz, rows, d = x.shape
    tm = _row_tile(rows)
    return pl.pallas_call(
        _final_norm_kernel,
        grid=(bsz, rows // tm),
        in_specs=[pl.BlockSpec((1, tm, d), lambda b, t: (b, t, 0)),
                  pl.BlockSpec((1, d), lambda b, t: (0, 0))],
        out_specs=pl.BlockSpec((1, tm, d), lambda b, t: (b, t, 0)),
        out_shape=jax.ShapeDtypeStruct(x.shape, F32),
        compiler_params=_cparams("parallel", "parallel"),
    )(x, g.reshape(1, d))


def _rope_tables(n_lat):
    rows = n_lat // GRID_W
    r = jnp.repeat(jnp.arange(rows, dtype=F32), GRID_W)
    col = jnp.tile(jnp.arange(GRID_W, dtype=F32), rows)
    inv = ROPE_THETA ** (-jnp.arange(ROPE_PAIRS_AXIS, dtype=F32) / ROPE_PAIRS_AXIS)
    ang = jnp.concatenate([r[:, None] * inv, col[:, None] * inv], axis=-1)
    cos, sin = jnp.cos(ang), jnp.sin(ang)
    zero = jnp.zeros_like(sin)
    reps = D_ATTN // HEAD_DIM
    cos_t = jnp.tile(jnp.concatenate([cos, cos], axis=-1), (1, reps))
    sin_a = jnp.tile(jnp.concatenate([-sin, zero], axis=-1), (1, reps))
    sin_b = jnp.tile(jnp.concatenate([zero, sin], axis=-1), (1, reps))
    return cos_t, sin_a, sin_b


def kernel(x, c, ctx, c_ctx, w_mod, b_mod, norm_g, ffn_w1, ffn_w2, w_in, w_out, lru_conv_w, lru_conv_b,
           lru_wa, lru_ba, lru_wx, lru_bx, lru_lam, hy_conv_w, hy_conv_b, hy_fw0, hy_fb0, hy_fw_in,
           hy_fb_in, hy_freq, hy_fw_last, hy_skip, attn_sink, final_g):
    bsz, n_lat, d = x.shape
    n_ctx = ctx.shape[1]
    depth = w_mod.shape[0]
    assert n_lat % ATT_BLOCK == 0 and n_ctx % ATT_BLOCK == 0 and n_lat % GRID_W == 0

    mod_rows = -(-(bsz + 1) // SUBLANES) * SUBLANES
    c_rows = jnp.zeros((mod_rows, d), F32).at[:bsz].set(c).at[bsz].set(c_ctx)
    mod_all = _modulation(c_rows, w_mod, b_mod).reshape(depth, mod_rows, N_MOD, d)
    lat_row = lambda b: b
    ctx_row = lambda b: bsz

    rope_tabs = _rope_tables(n_lat)
    dft_lat = _dft_matrices(n_lat)
    w1_b = ffn_w1.astype(BF16)
    w2_b = ffn_w2.astype(BF16)
    w_in_b = w_in.astype(BF16)
    w_out_b = w_out.astype(BF16)

    xc = ctx
    for l in range(depth):
        need_ctx = l < depth - 1
        mod = mod_all[l]
        filt = (hy_fw0[l], hy_fb0[l], hy_fw_in[l], hy_fb_in[l], hy_freq[l], hy_fw_last[l])

        x = _ffn(x, mod, lat_row, norm_g[l, 0], w1_b[l, 0], w2_b[l, 0], 0)
        xc = _ffn(xc, mod, ctx_row, norm_g[l, 0], w1_b[l, 0], w2_b[l, 0], 0)

        xl, gl, zh, q, k, v = _input_proj(x, mod, lat_row, norm_g[l, 1], w_in_b[l], rope_tabs)
        xlc, glc, zhc, qc, kc, vc = _input_proj(xc, mod, ctx_row, norm_g[l, 1], w_in_b[l], None)

        y_lru, yc_lru = _rglru(xl, gl, xlc, glc, lru_conv_w[l], lru_conv_b[l], lru_wa[l], lru_ba[l],
                               lru_wx[l], lru_bx[l], lru_lam[l], bsz)
        y_hy = _hyena(zh, bsz, hy_conv_w[l], hy_conv_b[l], filt, dft_lat, hy_skip[l])
        y_att = _attention(q, k, v, kc, vc, attn_sink[l], True)
        x = _out_proj(x, mod, lat_row, y_lru, y_hy, y_att, w_out_b[l])
        x = _ffn(x, mod, lat_row, norm_g[l, 2], w1_b[l, 1], w2_b[l, 1], 6)

        if need_ctx:
            yc_hy = _hyena(zhc, bsz, hy_conv_w[l], hy_conv_b[l], filt, _dft_matrices(n_ctx), hy_skip[l])
            yc_att = _attention(qc, None, None, kc, vc, attn_sink[l], False)
            xc = _out_proj(xc, mod, ctx_row, yc_lru, yc_hy, yc_att, w_out_b[l])
            xc = _ffn(xc, mod, ctx_row, norm_g[l, 2], w1_b[l, 1], w2_b[l, 1], 6)
    return _final_norm(x, final_g)
```

```python
import functools
import math

import jax
import jax.numpy as jnp
from jax import lax
from jax.experimental import pallas as pl
from jax.experimental.pallas import tpu as pltpu

F32 = jnp.float32
BF16 = jnp.bfloat16

NORM_EPS = 1e-6
N_MOD = 9
MACARON_W = 0.5
D_LRU = 256
LRU_BLOCKS = 4
LRU_C = 8.0
LRU_LEFT = 2
D_HY = 256
HY_LEFT = 1
HY_EMB = 33
HY_BANDS = (HY_EMB - 1) // 2
HY_FAST = 0.3
HY_SLOW = 1.5
HY_TARGET = 1e-2
N_QH = 8
N_KVH = 2
HEAD_DIM = 64
D_ATTN = N_QH * HEAD_DIM
D_KV = N_KVH * HEAD_DIM
WINDOW = 128
ATT_BLOCK = 128
GRID_W = 64
ROPE_THETA = 10000.0
ROPE_PAIRS_AXIS = HEAD_DIM // 4
NEG_INF = -1e30

LANES = 128
SUBLANES = 8
VMEM_LIMIT_BYTES = 56 * 1024 * 1024
ROW_TILE = 512
DFT_TILE = 512
SCAN_GROUP = 64


def _cparams(*sem):
    return pltpu.CompilerParams(dimension_semantics=sem, vmem_limit_bytes=VMEM_LIMIT_BYTES)


def _row_tile(rows):
    return min(ROW_TILE, rows)


def _resident(shape):
    return pl.BlockSpec(shape, lambda *_: (0,) * len(shape), pipeline_mode=pl.Buffered(1))


def _ada_norm(x, g, shift, scale):
    y = x * lax.rsqrt(jnp.mean(x * x, axis=-1, keepdims=True) + NORM_EPS)
    return (y * g) * (1.0 + scale) + shift


def _sigmoid(x):
    return 0.5 * (1.0 + jnp.tanh(0.5 * x))


def _gelu_tanh(x):
    return 0.5 * x * (1.0 + jnp.tanh(math.sqrt(2.0 / math.pi) * (x + 0.044715 * (x * x * x))))


def _softplus(x):
    return jnp.maximum(x, 0.0) + jnp.log1p(jnp.exp(-jnp.abs(x)))


def _dwconv_rows(x, w, bias, left):
    rows = x.shape[0]
    row = lax.broadcasted_iota(jnp.int32, x.shape, 0)
    out = jnp.broadcast_to(bias, x.shape)
    for k in range(w.shape[0]):
        off = k - left
        if off == 0:
            term = x
        else:
            shifted = pltpu.roll(x, (-off) % rows, axis=0)
            ok = (row + off >= 0) & (row + off < rows)
            term = jnp.where(ok, shifted, 0.0)
        out = out + term * w[k:k + 1, :]
    return out


def _mod_kernel(c_ref, w_ref, b_ref, o_ref):
    cv = c_ref[...]
    s = cv * _sigmoid(cv)
    o_ref[0] = jnp.dot(s, w_ref[0], preferred_element_type=F32,
                       precision=lax.Precision.HIGHEST) + b_ref[0]


def _modulation(c_rows, w_mod, b_mod):
    depth, d, nd = w_mod.shape
    rows = c_rows.shape[0]
    tn = nd // 8
    return pl.pallas_call(
        _mod_kernel,
        grid=(depth, nd // tn),
        in_specs=[
            pl.BlockSpec((rows, d), lambda l, j: (0, 0)),
            pl.BlockSpec((1, d, tn), lambda l, j: (l, 0, j)),
            pl.BlockSpec((1, 1, tn), lambda l, j: (l, 0, j)),
        ],
        out_specs=pl.BlockSpec((1, rows, tn), lambda l, j: (l, 0, j)),
        out_shape=jax.ShapeDtypeStruct((depth, rows, nd), F32),
        compiler_params=_cparams("parallel", "parallel"),
        name="modulation",
    )(c_rows, w_mod, b_mod.reshape(depth, 1, nd))


def _ffn_kernel(x_ref, mod_ref, g_ref, w1_ref, w2_ref, o_ref, *, i_mod, d_ff):
    x = x_ref[0]
    m = mod_ref[...]
    h = _ada_norm(x, g_ref[...], m[i_mod:i_mod + 1], m[i_mod + 1:i_mod + 2])
    ab = jnp.dot(h.astype(BF16), w1_ref[...], preferred_element_type=F32)
    a = ab[:, :d_ff]
    b = ab[:, d_ff:]
    gated = (a * _sigmoid(a)) * b
    y = jnp.dot(gated.astype(BF16), w2_ref[...], preferred_element_type=F32)
    o_ref[0] = x + (MACARON_W * m[i_mod + 2:i_mod + 3]) * y


def _ffn(x, mod, mod_row, g, w1, w2, i_mod):
    bsz, rows, d = x.shape
    d_ff = w2.shape[0]
    tm = _row_tile(rows)
    return pl.pallas_call(
        functools.partial(_ffn_kernel, i_mod=i_mod, d_ff=d_ff),
        grid=(bsz, rows // tm),
        in_specs=[
            pl.BlockSpec((1, tm, d), lambda b, t: (b, t, 0)),
            pl.BlockSpec((None, N_MOD, d), lambda b, t: (mod_row(b), 0, 0)),
            pl.BlockSpec((1, d), lambda b, t: (0, 0)),
            _resident((d, 2 * d_ff)),
            _resident((d_ff, d)),
        ],
        out_specs=pl.BlockSpec((1, tm, d), lambda b, t: (b, t, 0)),
        out_shape=jax.ShapeDtypeStruct(x.shape, F32),
        compiler_params=_cparams("parallel", "parallel"),
        name="ffn",
    )(x, mod, g.reshape(1, d), w1, w2)


def _rope(x, cos_t, sin_a, sin_b):
    width = x.shape[-1]
    half = HEAD_DIM // 2
    up = pltpu.roll(x, width - half, axis=1)
    dn = pltpu.roll(x, half, axis=1)
    return x * cos_t + up * sin_a + dn * sin_b


def _proj_kernel(*refs, rope):
    if rope:
        (x_ref, mod_ref, g_ref, w_ref, cos_ref, sa_ref, sb_ref,
         xl_ref, gl_ref, zh_ref, q_ref, k_ref, v_ref) = refs
    else:
        (x_ref, mod_ref, g_ref, w_ref, xl_ref, gl_ref, zh_ref, q_ref, k_ref, v_ref) = refs
    x = x_ref[0]
    m = mod_ref[...]
    h = _ada_norm(x, g_ref[...], m[3:4], m[4:5])
    z = jnp.dot(h.astype(BF16), w_ref[...], preferred_element_type=F32)
    o = 0
    xl_ref[...] = z[:, o:o + D_LRU]; o += D_LRU
    gl_ref[...] = z[:, o:o + D_LRU]; o += D_LRU
    zh_ref[...] = z[:, o:o + 3 * D_HY]; o += 3 * D_HY
    q = z[:, o:o + D_ATTN]; o += D_ATTN
    k = z[:, o:o + D_KV]; o += D_KV
    v = z[:, o:o + D_KV]
    if rope:
        cos_t, sin_a, sin_b = cos_ref[...], sa_ref[...], sb_ref[...]
        q = _rope(q, cos_t, sin_a, sin_b)
        k = _rope(k, cos_t[:, :D_KV], sin_a[:, :D_KV], sin_b[:, :D_KV])
    q_ref[0] = q * (HEAD_DIM ** -0.5)
    k_ref[0] = k
    v_ref[0] = v


def _input_proj(x, mod, mod_row, g, w_in, rope_tabs):
    bsz, rows, d = x.shape
    tm = _row_tile(rows)
    rope = rope_tabs is not None
    in_specs = [
        pl.BlockSpec((1, tm, d), lambda b, t: (b, t, 0)),
        pl.BlockSpec((None, N_MOD, d), lambda b, t: (mod_row(b), 0, 0)),
        pl.BlockSpec((1, d), lambda b, t: (0, 0)),
        _resident(w_in.shape),
    ]
    args = [x, mod, g.reshape(1, d), w_in]
    if rope:
        in_specs += [pl.BlockSpec((tm, D_ATTN), lambda b, t: (t, 0))] * 3
        args += list(rope_tabs)
    out_shape = [
        jax.ShapeDtypeStruct((rows, bsz * D_LRU), F32),
        jax.ShapeDtypeStruct((rows, bsz * D_LRU), F32),
        jax.ShapeDtypeStruct((rows, bsz * 3 * D_HY), F32),
        jax.ShapeDtypeStruct((bsz, rows, D_ATTN), F32),
        jax.ShapeDtypeStruct((bsz, rows, D_KV), F32),
        jax.ShapeDtypeStruct((bsz, rows, D_KV), F32),
    ]
    out_specs = [
        pl.BlockSpec((tm, D_LRU), lambda b, t: (t, b)),
        pl.BlockSpec((tm, D_LRU), lambda b, t: (t, b)),
        pl.BlockSpec((tm, 3 * D_HY), lambda b, t: (t, b)),
        pl.BlockSpec((1, tm, D_ATTN), lambda b, t: (b, t, 0)),
        pl.BlockSpec((1, tm, D_KV), lambda b, t: (b, t, 0)),
        pl.BlockSpec((1, tm, D_KV), lambda b, t: (b, t, 0)),
    ]
    return pl.pallas_call(
        functools.partial(_proj_kernel, rope=rope),
        grid=(bsz, rows // tm),
        in_specs=in_specs,
        out_specs=out_specs,
        out_shape=out_shape,
        compiler_params=_cparams("parallel", "parallel"),
        name="input_proj",
    )(*args)


def _scan_chunk(a, b, row, reverse):
    for s in (1, 2, 4):
        if reverse:
            ok = row < SUBLANES - s
            sh = SUBLANES - s
        else:
            ok = row >= s
            sh = s
        a_sh = pltpu.roll(a, sh, axis=0)
        b_sh = pltpu.roll(b, sh, axis=0)
        b = jnp.where(ok, a * b_sh + b, b)
        a = jnp.where(ok, a * a_sh, a)
    return a, b


def _lru_kernel(xl_ref, gl_ref, xc_ref, gc_ref, cw_ref, cb_ref, wa_ref, ba_ref, wx_ref, bx_ref,
                lam_ref, y_ref, yc_ref, a_s, b_s, *, n_lat, n_ctx, tile):
    width = xl_ref.shape[1]
    cw = cw_ref[...]
    cb = cb_ref[...]
    y_ref[...] = _dwconv_rows(xl_ref[...], cw, cb, LRU_LEFT)
    yc_ref[...] = _dwconv_rows(xc_ref[...], cw, cb, LRU_LEFT)

    def coeffs(u_ref, base, n_rows):
        def body(i, carry):
            r0 = pl.multiple_of(i * tile, tile)
            u = u_ref[pl.ds(r0, tile), :]
            ub = u.astype(BF16)
            for d in range(2):
                r = _sigmoid(jnp.dot(ub, wa_ref[d], preferred_element_type=F32) + ba_ref[d])
                ig = _sigmoid(jnp.dot(ub, wx_ref[d], preferred_element_type=F32) + bx_ref[d])
                log_a = (-LRU_C * r) * _softplus(-lam_ref[d])
                dst = pl.ds(pl.multiple_of(base + r0, SUBLANES), tile)
                a = jnp.exp(log_a)
                a_s[d, dst, :] = a
                b_s[d, dst, :] = jnp.sqrt(-jnp.tanh(log_a) * (1.0 + a * a)) * (ig * u)
            return carry
        lax.fori_loop(0, n_rows // tile, body, 0)

    coeffs(yc_ref, 0, n_ctx)
    coeffs(y_ref, n_ctx, n_lat)

    row = lax.broadcasted_iota(jnp.int32, (SUBLANES, width), 0)

    def scan_group(d, group, h, reverse):
        r0 = pl.multiple_of(group * SCAN_GROUP, SCAN_GROUP)
        a = a_s[d, pl.ds(r0, SCAN_GROUP), :]
        b = b_s[d, pl.ds(r0, SCAN_GROUP), :]
        order = range(SCAN_GROUP // SUBLANES)
        parts = [_scan_chunk(a[c * SUBLANES:(c + 1) * SUBLANES], b[c * SUBLANES:(c + 1) * SUBLANES], row, reverse)
                 for c in order]
        for c in (reversed(order) if reverse else order):
            hh = parts[c][0] * h + parts[c][1]
            b_s[d, pl.ds(r0 + c * SUBLANES, SUBLANES), :] = hh
            h = hh[0:1, :] if reverse else hh[SUBLANES - 1:SUBLANES, :]
        return h

    ng_ctx = n_ctx // SCAN_GROUP
    ng_all = (n_ctx + n_lat) // SCAN_GROUP
    h0 = jnp.zeros((1, width), F32)

    def ctx_body(j, hs):
        return (scan_group(0, j, hs[0], False), scan_group(1, ng_ctx - 1 - j, hs[1], True))

    def lat_body(j, hs):
        return (scan_group(0, ng_ctx + j, hs[0], False), scan_group(1, ng_all - 1 - j, hs[1], True))

    hs = lax.fori_loop(0, ng_ctx, ctx_body, (h0, h0))
    lax.fori_loop(0, ng_all - ng_ctx, lat_body, hs)

    def finish(o_ref, g_ref, base, n_rows):
        def body(i, carry):
            r0 = pl.multiple_of(i * tile, tile)
            src = pl.ds(pl.multiple_of(base + r0, SUBLANES), tile)
            hsum = b_s[0, src, :] + b_s[1, src, :]
            o_ref[pl.ds(r0, tile), :] = hsum * _gelu_tanh(g_ref[pl.ds(r0, tile), :])
            return carry
        lax.fori_loop(0, n_rows // tile, body, 0)

    finish(yc_ref, gc_ref, 0, n_ctx)
    finish(y_ref, gl_ref, n_ctx, n_lat)


def _block_diag(w):
    two, nb, bs, _ = w.shape
    eye = jnp.eye(nb, dtype=w.dtype)
    return jnp.einsum('dnij,nm->dnimj', w, eye).reshape(two, nb * bs, nb * bs)


def _rglru(xl, gl, xlc, glc, conv_w, conv_b, wa, ba, wx, bx, lam, bsz):
    n_lat, n_ctx = xl.shape[0], xlc.shape[0]
    width = LANES
    per_b = D_LRU // width
    tile = math.gcd(256, math.gcd(n_lat, n_ctx))
    wa_bd = _block_diag(wa).astype(BF16)
    wx_bd = _block_diag(wx).astype(BF16)
    col = lambda b, j: (0, b * per_b + j)
    par = lambda b, j: (0, j)
    par3 = lambda b, j: (0, 0, j)
    return pl.pallas_call(
        functools.partial(_lru_kernel, n_lat=n_lat, n_ctx=n_ctx, tile=tile),
        grid=(bsz, per_b),
        in_specs=[
            pl.BlockSpec((n_lat, width), col),
            pl.BlockSpec((n_lat, width), col),
            pl.BlockSpec((n_ctx, width), col),
            pl.BlockSpec((n_ctx, width), col),
            pl.BlockSpec((conv_w.shape[0], width), par),
            pl.BlockSpec((1, width), par),
            pl.BlockSpec((2, width, width), lambda b, j: (0, j, j)),
            pl.BlockSpec((2, 1, width), par3),
            pl.BlockSpec((2, width, width), lambda b, j: (0, j, j)),
            pl.BlockSpec((2, 1, width), par3),
            pl.BlockSpec((2, 1, width), par3),
        ],
        out_specs=[pl.BlockSpec((n_lat, width), col), pl.BlockSpec((n_ctx, width), col)],
        out_shape=[jax.ShapeDtypeStruct(xl.shape, F32), jax.ShapeDtypeStruct(xlc.shape, F32)],
        scratch_shapes=[pltpu.VMEM((2, n_ctx + n_lat, width), F32),
                        pltpu.VMEM((2, n_ctx + n_lat, width), F32)],
        compiler_params=_cparams("parallel", "parallel"),
        name="rglru",
    )(xl, gl, xlc, glc, conv_w, conv_b.reshape(1, -1), wa_bd, ba.reshape(2, 1, -1), wx_bd,
      bx.reshape(2, 1, -1), lam.reshape(2, 1, -1))


def _hy_pre_kernel(z0_ref, z1_ref, z2_ref, w0_ref, w1_ref, w2_ref, b0_ref, b1_ref, b2_ref,
                   ub_ref, u_ref, x0_ref):
    x0 = _dwconv_rows(z0_ref[...], w0_ref[...], b0_ref[...], HY_LEFT)
    x1 = _dwconv_rows(z1_ref[...], w1_ref[...], b1_ref[...], HY_LEFT)
    v = _dwconv_rows(z2_ref[...], w2_ref[...], b2_ref[...], HY_LEFT)
    u = x1 * v
    u_ref[...] = u
    ub_ref[...] = u.astype(BF16)
    x0_ref[...] = x0


def _hyena_pre(zh, conv_w, conv_b, bsz):
    rows = zh.shape[0]
    width = LANES
    per_b = D_HY // width
    zspec = lambda part: pl.BlockSpec((rows, width), lambda b, j: (0, b * 3 * per_b + part * per_b + j))
    wspec = lambda part: pl.BlockSpec((conv_w.shape[0], width), lambda b, j: (0, part * per_b + j))
    bspec = lambda part: pl.BlockSpec((1, width), lambda b, j: (0, part * per_b + j))
    ospec = pl.BlockSpec((rows, width), lambda b, j: (0, b * per_b + j))
    cb = conv_b.reshape(1, -1)
    return pl.pallas_call(
        _hy_pre_kernel,
        grid=(bsz, per_b),
        in_specs=[zspec(0), zspec(1), zspec(2), wspec(0), wspec(1), wspec(2), bspec(0), bspec(1), bspec(2)],
        out_specs=[ospec, ospec, ospec],
        out_shape=[jax.ShapeDtypeStruct((rows, bsz * D_HY), BF16),
                   jax.ShapeDtypeStruct((rows, bsz * D_HY), F32),
                   jax.ShapeDtypeStruct((rows, bsz * D_HY), F32)],
        compiler_params=_cparams("parallel", "parallel"),
        name="hyena_pre",
    )(zh, zh, zh, conv_w, conv_w, conv_w, cb, cb, cb)


def _hy_filter_kernel(z_ref, fw0_ref, fb0_ref, fwin_ref, fbin_ref, freq_ref, fwl_ref, dl_ref, o_ref, *, tile):
    hp = lax.Precision.HIGHEST
    z = z_ref[...]
    fr = freq_ref[...]
    hdn = jnp.sin(fr * (jnp.dot(z, fw0_ref[...], preferred_element_type=F32, precision=hp) + fb0_ref[...]))
    for j in range(fwin_ref.shape[0]):
        hdn = jnp.sin(fr * (jnp.dot(hdn, fwin_ref[j], preferred_element_type=F32, precision=hp) + fbin_ref[j]))
    k = jnp.dot(hdn, fwl_ref[...], preferred_element_type=F32, precision=hp)
    decay = jnp.exp(-z[:, 0:1] * dl_ref[...])
    k_fwd = k[:, :D_HY] * decay
    k_bwd = k[:, D_HY:] * decay
    row = lax.broadcasted_iota(jnp.int32, k_bwd.shape, 0) + pl.program_id(0) * tile
    k_bwd = jnp.where(row == 0, 0.0, k_bwd)
    o_ref[...] = jnp.concatenate([k_fwd, k_bwd], axis=-1).astype(BF16)


def _hyena_filter_taps(n, fw0, fb0, fw_in, fb_in, freq, fw_last):
    t = jnp.linspace(0.0, 1.0, n, dtype=F32)[:, None]
    w = 2.0 * math.pi * jnp.arange(n, dtype=F32)[:, None] / n
    f = jnp.linspace(1e-4, HY_BANDS - 1, HY_BANDS, dtype=F32)[None, :]
    z = jnp.concatenate([t, jnp.cos(f * w), -jnp.sin(f * w)], axis=-1)
    z = jnp.pad(z, ((0, 0), (0, LANES - HY_EMB)))
    fw0p = jnp.pad(fw0, ((0, LANES - HY_EMB), (0, 0)))
    max_decay = math.log(HY_TARGET) / HY_FAST
    min_decay = math.log(HY_TARGET) / HY_SLOW
    deltas = jnp.abs(jnp.linspace(min_decay, max_decay, D_HY, dtype=F32))[None, :]
    hid = fw0.shape[1]
    tile = min(512, n)
    full = lambda a: pl.BlockSpec(a.shape, lambda i: (0,) * a.ndim)
    args = [fw0p, fb0.reshape(1, hid), fw_in, fb_in.reshape(-1, 1, hid), freq.reshape(1, hid), fw_last, deltas]
    return pl.pallas_call(
        functools.partial(_hy_filter_kernel, tile=tile),
        grid=(n // tile,),
        in_specs=[pl.BlockSpec((tile, LANES), lambda i: (i, 0))] + [full(a) for a in args],
        out_specs=pl.BlockSpec((tile, 2 * D_HY), lambda i: (i, 0)),
        out_shape=jax.ShapeDtypeStruct((n, 2 * D_HY), BF16),
        compiler_params=_cparams("parallel"),
        name="hyena_filter",
    )(z, *args)


def _dft_expand_kernel(ca_ref, sa_ref, cb_ref, sb_ref, cm_ref, sm_ref, *, tile):
    ca, sa, cb, sb = ca_ref[...], sa_ref[...], cb_ref[...], sb_ref[...]
    row = lax.broadcasted_iota(jnp.int32, cb.shape, 0) + pl.program_id(0) * tile
    lane = lax.broadcasted_iota(jnp.int32, cb.shape, 1)
    alt = jnp.where(lane % 2 == 0, 1.0, -1.0)
    for s1 in range(cm_ref.shape[1] // LANES):
        c1 = ca[:, s1:s1 + 1]
        d1 = sa[:, s1:s1 + 1]
        cols = slice(s1 * LANES, (s1 + 1) * LANES)
        cm_ref[:, cols] = (c1 * cb - d1 * sb).astype(BF16)
        sm_ref[:, cols] = jnp.where(row == 0, alt, -(d1 * cb + c1 * sb)).astype(BF16)


def _dft_matrices(n):
    two_n = 2 * n
    f = jnp.arange(n, dtype=jnp.int32)[:, None]
    s_hi = jnp.arange(n // LANES, dtype=jnp.int32)[None, :] * LANES
    s_lo = jnp.arange(LANES, dtype=jnp.int32)[None, :]
    ang_a = ((f * s_hi) % two_n).astype(F32) * (2.0 * math.pi / two_n)
    ang_b = ((f * s_lo) % two_n).astype(F32) * (2.0 * math.pi / two_n)
    tile = min(DFT_TILE, n)
    hi_spec = pl.BlockSpec((tile, n // LANES), lambda i: (i, 0))
    lo_spec = pl.BlockSpec((tile, LANES), lambda i: (i, 0))
    return pl.pallas_call(
        functools.partial(_dft_expand_kernel, tile=tile),
        grid=(n // tile,),
        in_specs=[hi_spec, hi_spec, lo_spec, lo_spec],
        out_specs=[pl.BlockSpec((tile, n), lambda i: (i, 0))] * 2,
        out_shape=[jax.ShapeDtypeStruct((n, n), BF16)] * 2,
        compiler_params=_cparams("parallel"),
        name="dft_matrices",
    )(jnp.cos(ang_a), jnp.sin(ang_a), jnp.cos(ang_b), jnp.sin(ang_b))


def _dft_filter_kernel(cm_ref, sm_ref, kk_ref, kre_ref, kim_ref, *, tile):
    kk = kk_ref[...]
    xre = jnp.dot(cm_ref[...], kk, preferred_element_type=F32)
    xim = jnp.dot(sm_ref[...], kk, preferred_element_type=F32)
    row = lax.broadcasted_iota(jnp.int32, (tile, D_HY), 0) + pl.program_id(0) * tile
    kre_ref[...] = xre[:, :D_HY] + xre[:, D_HY:]
    kim_ref[...] = jnp.where(row == 0, xim[:, :D_HY] + xim[:, D_HY:], xim[:, :D_HY] - xim[:, D_HY:])


def _dft_filter(cm, sm, kk):
    n = cm.shape[0]
    tile = min(DFT_TILE, n)
    return pl.pallas_call(
        functools.partial(_dft_filter_kernel, tile=tile),
        grid=(n // tile,),
        in_specs=[pl.BlockSpec((tile, n), lambda f: (f, 0)),
                  pl.BlockSpec((tile, n), lambda f: (f, 0)),
                  pl.BlockSpec(kk.shape, lambda f: (0, 0))],
        out_specs=[pl.BlockSpec((tile, D_HY), lambda f: (f, 0))] * 2,
        out_shape=[jax.ShapeDtypeStruct((n, D_HY), F32)] * 2,
        compiler_params=_cparams("parallel"),
        name="dft_filter",
    )(cm, sm, kk)


def _dft_fwd_kernel(cm_ref, sm_ref, u_ref, kre_ref, kim_ref, yre_ref, yim_ref, *, tile):
    n_freq = cm_ref.shape[1]
    u = u_ref[...]
    xre = jnp.dot(cm_ref[...], u, preferred_element_type=F32)
    xim = jnp.dot(sm_ref[...], u, preferred_element_type=F32)
    kre = kre_ref[...]
    kim = kim_ref[...]
    row = lax.broadcasted_iota(jnp.int32, xre.shape, 0) + pl.program_id(0) * tile
    first = row == 0
    scale = jnp.where(first, 0.5 / n_freq, 1.0 / n_freq)
    yre_ref[...] = (scale * (xre * kre - jnp.where(first, 0.0, xim * kim))).astype(BF16)
    yim_ref[...] = (scale * jnp.where(first, xim * kim, xre * kim + xim * kre)).astype(BF16)


def _dft_fwd(cm, sm, ub, kre, kim):
    n = cm.shape[0]
    cols = ub.shape[1]
    tile = min(DFT_TILE, n)
    return pl.pallas_call(
        functools.partial(_dft_fwd_kernel, tile=tile),
        grid=(n // tile, cols // D_HY),
        in_specs=[pl.BlockSpec((tile, n), lambda f, c: (f, 0)),
                  pl.BlockSpec((tile, n), lambda f, c: (f, 0)),
                  pl.BlockSpec((n, D_HY), lambda f, c: (0, c)),
                  pl.BlockSpec((tile, D_HY), lambda f, c: (f, 0)),
                  pl.BlockSpec((tile, D_HY), lambda f, c: (f, 0))],
        out_specs=[pl.BlockSpec((tile, D_HY), lambda f, c: (f, c))] * 2,
        out_shape=[jax.ShapeDtypeStruct((n, cols), BF16)] * 2,
        compiler_params=_cparams("parallel", "parallel"),
        name="dft_fwd",
    )(cm, sm, ub, kre, kim)


def _dft_inv_kernel(ci_ref, si_ref, yre_ref, yim_ref, u_ref, x0_ref, skip_ref, o_ref, *, tile):
    y_cos = jnp.dot(ci_ref[...], yre_ref[...], preferred_element_type=F32)
    y_sin = jnp.dot(si_ref[...], yim_ref[...], preferred_element_type=F32)
    row = lax.broadcasted_iota(jnp.int32, y_cos.shape, 0) + pl.program_id(0) * tile
    nyq = jnp.where(row % 2 == 0, 1.0, -1.0) * yim_ref[0:1, :].astype(F32)
    y = y_cos + jnp.where(row == 0, 0.0, y_sin) + nyq
    o_ref[...] = x0_ref[...] * (y + u_ref[...] * skip_ref[...])


def _dft_inv(ci, si, yre, yim, u, x0, skip):
    n = ci.shape[0]
    cols = yre.shape[1]
    tile = min(DFT_TILE, n)
    return pl.pallas_call(
        functools.partial(_dft_inv_kernel, tile=tile),
        grid=(n // tile, cols // D_HY),
        in_specs=[pl.BlockSpec((tile, n), lambda t, c: (t, 0)),
                  pl.BlockSpec((tile, n), lambda t, c: (t, 0)),
                  pl.BlockSpec((n, D_HY), lambda t, c: (0, c)),
                  pl.BlockSpec((n, D_HY), lambda t, c: (0, c)),
                  pl.BlockSpec((tile, D_HY), lambda t, c: (t, c)),
                  pl.BlockSpec((tile, D_HY), lambda t, c: (t, c)),
                  pl.BlockSpec((1, D_HY), lambda t, c: (0, 0))],
        out_specs=pl.BlockSpec((tile, D_HY), lambda t, c: (t, c)),
        out_shape=jax.ShapeDtypeStruct((n, cols), F32),
        compiler_params=_cparams("parallel", "parallel"),
        name="dft_inv",
    )(ci, si, yre, yim, u, x0, skip.reshape(1, D_HY))


def _hyena(zh, bsz, conv_w, conv_b, filt, dft, skip):
    cm, sm = dft
    ub, u, x0 = _hyena_pre(zh, conv_w, conv_b, bsz)
    kk = _hyena_filter_taps(zh.shape[0], *filt)
    kre, kim = _dft_filter(cm, sm, kk)
    yre, yim = _dft_fwd(cm, sm, ub, kre, kim)
    return _dft_inv(cm, sm, yre, yim, u, x0, skip)


def _attn_kernel(*refs, windowed):
    if windowed:
        (q_ref, kp_ref, kc_ref, kn_ref, vp_ref, vc_ref, vn_ref, kx_ref, vx_ref, sink_ref, o_ref) = refs
        keys = jnp.concatenate([kp_ref[0], kc_ref[0], kn_ref[0], kx_ref[0]], axis=0)
        vals = jnp.concatenate([vp_ref[0], vc_ref[0], vn_ref[0], vx_ref[0]], axis=0)
    else:
        (q_ref, kx_ref, vx_ref, sink_ref, o_ref) = refs
        keys = kx_ref[0]
        vals = vx_ref[0]
    q = q_ref[0]
    nq = q.shape[0]
    nk = keys.shape[0]
    if windowed:
        i = pl.program_id(1)
        last = pl.num_programs(1) - 1
        r = lax.broadcasted_iota(jnp.int32, (nq, nk), 0)
        j = lax.broadcasted_iota(jnp.int32, (nq, nk), 1)
        span = ATT_BLOCK + 2 * WINDOW
        in_band = (j >= r) & (j <= r + 2 * WINDOW)
        lo = jnp.where(i == 0, WINDOW, 0)
        hi = jnp.where(i == last, WINDOW + ATT_BLOCK, span)
        valid = (j >= span) | (in_band & (j >= lo) & (j < hi))
    lane = lax.broadcasted_iota(jnp.int32, keys.shape, 1)
    low = lane < HEAD_DIM
    k_low = (jnp.where(low, keys, 0.0).astype(BF16),
             pltpu.roll(jnp.where(low, 0.0, keys), HEAD_DIM, axis=1).astype(BF16))
    v_low = (jnp.where(low, vals, 0.0).astype(BF16),
             pltpu.roll(jnp.where(low, 0.0, vals), HEAD_DIM, axis=1).astype(BF16))
    cols_per_g = D_ATTN // LANES // N_KVH
    for g in range(N_KVH):
        cols = [q[:, c * LANES:(c + 1) * LANES] for c in range(g * cols_per_g, (g + 1) * cols_per_g)]
        qg = jnp.concatenate(cols + [pltpu.roll(cq, HEAD_DIM, axis=1) for cq in cols], axis=0).astype(BF16)
        s_all = lax.dot_general(qg, k_low[g], (((1,), (1,)), ((), ())), preferred_element_type=F32)
        es, denoms = [], []
        for hb in range(2 * cols_per_g):
            h = 2 * (g * cols_per_g + hb % cols_per_g) + hb // cols_per_g
            s = s_all[hb * nq:(hb + 1) * nq]
            if windowed:
                s = jnp.where(valid, s, NEG_INF)
            sk = sink_ref[h:h + 1, 0:1]
            m = jnp.maximum(jnp.max(s, axis=-1, keepdims=True), sk)
            e = jnp.exp(s - m)
            denoms.append(jnp.sum(e, axis=-1, keepdims=True) + jnp.exp(sk - m))
            es.append(e.astype(BF16))
        o_all = jnp.dot(jnp.concatenate(es, axis=0), v_low[g], preferred_element_type=F32)
        outs = [o_all[hb * nq:(hb + 1) * nq] / denoms[hb] for hb in range(2 * cols_per_g)]
        for ci in range(cols_per_g):
            c = g * cols_per_g + ci
            o_ref[0, :, c * LANES:(c + 1) * LANES] = outs[ci] + pltpu.roll(outs[cols_per_g + ci], HEAD_DIM, axis=1)


def _attention(q, k, v, kx, vx, sink, windowed):
    bsz, lq, _ = q.shape
    n_ctx = kx.shape[1]
    nb = lq // ATT_BLOCK
    sink_t = jnp.broadcast_to(sink.reshape(N_QH, 1), (N_QH, LANES))
    qspec = pl.BlockSpec((1, ATT_BLOCK, D_ATTN), lambda b, i: (b, i, 0))
    xspec = pl.BlockSpec((1, n_ctx, D_KV), lambda b, i: (b, 0, 0))
    sspec = pl.BlockSpec((N_QH, LANES), lambda b, i: (0, 0))
    if windowed:
        prev = pl.BlockSpec((1, ATT_BLOCK, D_KV), lambda b, i: (b, jnp.maximum(i - 1, 0), 0))
        cur = pl.BlockSpec((1, ATT_BLOCK, D_KV), lambda b, i: (b, i, 0))
        nxt = pl.BlockSpec((1, ATT_BLOCK, D_KV), lambda b, i: (b, jnp.minimum(i + 1, nb - 1), 0))
        in_specs = [qspec, prev, cur, nxt, prev, cur, nxt, xspec, xspec, sspec]
        args = (q, k, k, k, v, v, v, kx, vx, sink_t)
    else:
        in_specs = [qspec, xspec, xspec, sspec]
        args = (q, kx, vx, sink_t)
    return pl.pallas_call(
        functools.partial(_attn_kernel, windowed=windowed),
        grid=(bsz, nb),
        in_specs=in_specs,
        out_specs=pl.BlockSpec((1, ATT_BLOCK, D_ATTN), lambda b, i: (b, i, 0)),
        out_shape=jax.ShapeDtypeStruct(q.shape, F32),
        compiler_params=_cparams("parallel", "parallel"),
        name="attention",
    )(*args)


def _out_kernel(x_ref, mod_ref, yl_ref, yh_ref, ya_ref, w_ref, o_ref):
    m = mod_ref[...]
    y = (jnp.dot(yl_ref[...].astype(BF16), w_ref[0:D_LRU, :], preferred_element_type=F32)
         + jnp.dot(yh_ref[...].astype(BF16), w_ref[D_LRU:D_LRU + D_HY, :], preferred_element_type=F32)
         + jnp.dot(ya_ref[0].astype(BF16), w_ref[D_LRU + D_HY:, :], preferred_element_type=F32))
    o_ref[0] = x_ref[0] + m[5:6] * y


def _out_proj(x, mod, mod_row, y_lru, y_hy, y_att, w_out):
    bsz, rows, d = x.shape
    tm = _row_tile(rows)
    return pl.pallas_call(
        _out_kernel,
        grid=(bsz, rows // tm),
        in_specs=[
            pl.BlockSpec((1, tm, d), lambda b, t: (b, t, 0)),
            pl.BlockSpec((None, N_MOD, d), lambda b, t: (mod_row(b), 0, 0)),
            pl.BlockSpec((tm, D_LRU), lambda b, t: (t, b)),
            pl.BlockSpec((tm, D_HY), lambda b, t: (t, b)),
            pl.BlockSpec((1, tm, D_ATTN), lambda b, t: (b, t, 0)),
            _resident(w_out.shape),
        ],
        out_specs=pl.BlockSpec((1, tm, d), lambda b, t: (b, t, 0)),
        out_shape=jax.ShapeDtypeStruct(x.shape, F32),
        compiler_params=_cparams("parallel", "parallel"),
        name="out_proj",
    )(x, mod, y_lru, y_hy, y_att, w_out)


def _final_norm_kernel(x_ref, g_ref, o_ref):
    x = x_ref[0]
    o_ref[0] = (x * lax.rsqrt(jnp.mean(x * x, axis=-1, keepdims=True) + NORM_EPS)) * g_ref[...]


def _final_norm(x, g):
    bsz, rows, d = x.shape
    tm = _row_tile(rows)
    return pl.pallas_call(
        _final_norm_kernel,
        grid=(bsz, rows // tm),
        in_specs=[pl.BlockSpec((1, tm, d), lambda b, t: (b, t, 0)),
                  pl.BlockSpec((1, d), lambda b, t: (0, 0))],
        out_specs=pl.BlockSpec((1, tm, d), lambda b, t: (b, t, 0)),
        out_shape=jax.ShapeDtypeStruct(x.shape, F32),
        compiler_params=_cparams("parallel", "parallel"),
        name="final_norm",
    )(x, g.reshape(1, d))


def _rope_tables(n_lat):
    rows = n_lat // GRID_W
    r = jnp.repeat(jnp.arange(rows, dtype=F32), GRID_W)
    col = jnp.tile(jnp.arange(GRID_W, dtype=F32), rows)
    inv = ROPE_THETA ** (-jnp.arange(ROPE_PAIRS_AXIS, dtype=F32) / ROPE_PAIRS_AXIS)
    ang = jnp.concatenate([r[:, None] * inv, col[:, None] * inv], axis=-1)
    cos, sin = jnp.cos(ang), jnp.sin(ang)
    zero = jnp.zeros_like(sin)
    reps = D_ATTN // HEAD_DIM
    cos_t = jnp.tile(jnp.concatenate([cos, cos], axis=-1), (1, reps))
    sin_a = jnp.tile(jnp.concatenate([-sin, zero], axis=-1), (1, reps))
    sin_b = jnp.tile(jnp.concatenate([zero, sin], axis=-1), (1, reps))
    return cos_t, sin_a, sin_b


def kernel(x, c, ctx, c_ctx, w_mod, b_mod, norm_g, ffn_w1, ffn_w2, w_in, w_out, lru_conv_w, lru_conv_b,
           lru_wa, lru_ba, lru_wx, lru_bx, lru_lam, hy_conv_w, hy_conv_b, hy_fw0, hy_fb0, hy_fw_in,
           hy_fb_in, hy_freq, hy_fw_last, hy_skip, attn_sink, final_g):
    bsz, n_lat, d = x.shape
    n_ctx = ctx.shape[1]
    depth = w_mod.shape[0]
    assert n_lat % ATT_BLOCK == 0 and n_ctx % ATT_BLOCK == 0 and n_lat % GRID_W == 0
    assert ATT_BLOCK % SCAN_GROUP == 0 and ATT_BLOCK % LANES == 0

    mod_rows = -(-(bsz + 1) // SUBLANES) * SUBLANES
    c_rows = jnp.zeros((mod_rows, d), F32).at[:bsz].set(c).at[bsz].set(c_ctx)
    mod_all = _modulation(c_rows, w_mod, b_mod).reshape(depth, mod_rows, N_MOD, d)
    lat_row = lambda b: b
    ctx_row = lambda b: bsz

    rope_tabs = _rope_tables(n_lat)
    dft_lat = _dft_matrices(n_lat)
    w1_b = ffn_w1.astype(BF16)
    w2_b = ffn_w2.astype(BF16)
    w_in_b = w_in.astype(BF16)
    w_out_b = w_out.astype(BF16)

    xc = ctx
    for l in range(depth):
        need_ctx = l < depth - 1
        mod = mod_all[l]
        filt = (hy_fw0[l], hy_fb0[l], hy_fw_in[l], hy_fb_in[l], hy_freq[l], hy_fw_last[l])

        x = _ffn(x, mod, lat_row, norm_g[l, 0], w1_b[l, 0], w2_b[l, 0], 0)
        xc = _ffn(xc, mod, ctx_row, norm_g[l, 0], w1_b[l, 0], w2_b[l, 0], 0)

        xl, gl, zh, q, k, v = _input_proj(x, mod, lat_row, norm_g[l, 1], w_in_b[l], rope_tabs)
        xlc, glc, zhc, qc, kc, vc = _input_proj(xc, mod, ctx_row, norm_g[l, 1], w_in_b[l], None)

        y_lru, yc_lru = _rglru(xl, gl, xlc, glc, lru_conv_w[l], lru_conv_b[l], lru_wa[l], lru_ba[l],
                               lru_wx[l], lru_bx[l], lru_lam[l], bsz)
        y_hy = _hyena(zh, bsz, hy_conv_w[l], hy_conv_b[l], filt, dft_lat, hy_skip[l])
        y_att = _attention(q, k, v, kc, vc, attn_sink[l], True)
        x = _out_proj(x, mod, lat_row, y_lru, y_hy, y_att, w_out_b[l])
        x = _ffn(x, mod, lat_row, norm_g[l, 2], w1_b[l, 1], w2_b[l, 1], 6)

        if need_ctx:
            yc_hy = _hyena(zhc, bsz, hy_conv_w[l], hy_conv_b[l], filt, _dft_matrices(n_ctx), hy_skip[l])
            yc_att = _attention(qc, None, None, kc, vc, attn_sink[l], False)
            xc = _out_proj(xc, mod, ctx_row, yc_lru, yc_hy, yc_att, w_out_b[l])
            xc = _ffn(xc, mod, ctx_row, norm_g[l, 2], w1_b[l, 1], w2_b[l, 1], 6)
    return _final_norm(x, final_g)
```

```python
import functools
import math

import jax
import jax.numpy as jnp
from jax import lax
from jax.experimental import pallas as pl
from jax.experimental.pallas import tpu as pltpu

F32 = jnp.float32
BF16 = jnp.bfloat16

NORM_EPS = 1e-6
N_MOD = 9
MACARON_W = 0.5
D_LRU = 256
LRU_BLOCKS = 4
LRU_C = 8.0
LRU_LEFT = 2
D_HY = 256
HY_LEFT = 1
HY_EMB = 33
HY_BANDS = (HY_EMB - 1) // 2
HY_FAST = 0.3
HY_SLOW = 1.5
HY_TARGET = 1e-2
N_QH = 8
N_KVH = 2
HEAD_DIM = 64
D_ATTN = N_QH * HEAD_DIM
D_KV = N_KVH * HEAD_DIM
WINDOW = 128
ATT_BLOCK = 128
GRID_W = 64
ROPE_THETA = 10000.0
ROPE_PAIRS_AXIS = HEAD_DIM // 4
NEG_INF = -1e30
LOG2E = math.log2(math.e)

LANES = 128
SUBLANES = 8
VMEM_LIMIT_BYTES = 56 * 1024 * 1024
ROW_TILE = 512
DFT_TILE = 1024
DFT_FILTER_TILE = 512
SCAN_GROUP = 64


def _cparams(*sem):
    return pltpu.CompilerParams(dimension_semantics=sem, vmem_limit_bytes=VMEM_LIMIT_BYTES)


def _row_tile(rows):
    return min(ROW_TILE, rows)


def _resident(shape, index=None):
    index = (0,) * len(shape) if index is None else index
    return pl.BlockSpec(shape, lambda *_: index, pipeline_mode=pl.Buffered(1))


def _ada_norm(x, g, shift, scale):
    y = x * lax.rsqrt(jnp.mean(x * x, axis=-1, keepdims=True) + NORM_EPS)
    return (y * g) * (1.0 + scale) + shift


def _sigmoid(x):
    return 0.5 * (1.0 + jnp.tanh(0.5 * x))


def _gelu_tanh(x):
    return 0.5 * x * (1.0 + jnp.tanh(math.sqrt(2.0 / math.pi) * (x + 0.044715 * (x * x * x))))


def _softplus(x):
    return jnp.maximum(x, 0.0) + jnp.log1p(jnp.exp(-jnp.abs(x)))


def _dwconv_rows(x, w, bias, left):
    rows = x.shape[0]
    row = lax.broadcasted_iota(jnp.int32, x.shape, 0)
    out = jnp.broadcast_to(bias, x.shape)
    for k in range(w.shape[0]):
        off = k - left
        if off == 0:
            term = x
        else:
            shifted = pltpu.roll(x, (-off) % rows, axis=0)
            ok = (row + off >= 0) & (row + off < rows)
            term = jnp.where(ok, shifted, 0.0)
        out = out + term * w[k:k + 1, :]
    return out


def _mod_kernel(c_ref, w_ref, b_ref, o_ref):
    cv = c_ref[...]
    s = cv * _sigmoid(cv)
    o_ref[0] = jnp.dot(s, w_ref[0], preferred_element_type=F32,
                       precision=lax.Precision.HIGHEST) + b_ref[0]


def _modulation(c_rows, w_mod, b_mod):
    depth, d, nd = w_mod.shape
    rows = c_rows.shape[0]
    tn = nd // 8
    return pl.pallas_call(
        _mod_kernel,
        grid=(depth, nd // tn),
        in_specs=[
            pl.BlockSpec((rows, d), lambda l, j: (0, 0)),
            pl.BlockSpec((1, d, tn), lambda l, j: (l, 0, j)),
            pl.BlockSpec((1, 1, tn), lambda l, j: (l, 0, j)),
        ],
        out_specs=pl.BlockSpec((1, rows, tn), lambda l, j: (l, 0, j)),
        out_shape=jax.ShapeDtypeStruct((depth, rows, nd), F32),
        compiler_params=_cparams("parallel", "parallel"),
        name="modulation",
    )(c_rows, w_mod, b_mod.reshape(depth, 1, nd))


def _ffn_kernel(*refs, i_mod, d_ff, mixer, final):
    refs = list(refs)
    o_ref = refs.pop()
    x_ref, mod_ref, g_ref, w1_ref, w2_ref = refs[:5]
    rest = refs[5:]
    x = x_ref[0]
    m = mod_ref[...]
    if mixer:
        yl_ref, yh_ref, ya_ref, wo_ref = rest[:4]
        rest = rest[4:]
        y = (jnp.dot(yl_ref[...].astype(BF16), wo_ref[0:D_LRU, :], preferred_element_type=F32)
             + jnp.dot(yh_ref[...].astype(BF16), wo_ref[D_LRU:D_LRU + D_HY, :], preferred_element_type=F32)
             + jnp.dot(ya_ref[0].astype(BF16), wo_ref[D_LRU + D_HY:, :], preferred_element_type=F32))
        x = x + m[5:6] * y
    h = _ada_norm(x, g_ref[...], m[i_mod:i_mod + 1], m[i_mod + 1:i_mod + 2])
    ab = jnp.dot(h.astype(BF16), w1_ref[...], preferred_element_type=F32)
    a = ab[:, :d_ff]
    b = ab[:, d_ff:]
    gated = (a * _sigmoid(a)) * b
    y = jnp.dot(gated.astype(BF16), w2_ref[...], preferred_element_type=F32)
    x = x + (MACARON_W * m[i_mod + 2:i_mod + 3]) * y
    if final:
        x = (x * lax.rsqrt(jnp.mean(x * x, axis=-1, keepdims=True) + NORM_EPS)) * rest[0][...]
    o_ref[0] = x


def _ffn(x, mod, mod_row, g, w1, w2, layer, which, i_mod, mixer=None, final_g=None):
    bsz, rows, d = x.shape
    d_ff = w2.shape[2]
    tm = _row_tile(rows)
    in_specs = [
        pl.BlockSpec((1, tm, d), lambda b, t: (b, t, 0)),
        pl.BlockSpec((None, N_MOD, d), lambda b, t: (mod_row(b), 0, 0)),
        pl.BlockSpec((1, d), lambda b, t: (0, 0)),
        _resident((None, None, d, 2 * d_ff), (layer, which, 0, 0)),
        _resident((None, None, d_ff, d), (layer, which, 0, 0)),
    ]
    args = [x, mod, g.reshape(1, d), w1, w2]
    if mixer is not None:
        y_lru, y_hy, y_att, w_out = mixer
        in_specs += [
            pl.BlockSpec((tm, D_LRU), lambda b, t: (t, b)),
            pl.BlockSpec((tm, D_HY), lambda b, t: (t, b)),
            pl.BlockSpec((1, tm, D_ATTN), lambda b, t: (b, t, 0)),
            _resident((None,) + w_out.shape[1:], (layer, 0, 0)),
        ]
        args += [y_lru, y_hy, y_att, w_out]
    if final_g is not None:
        in_specs.append(pl.BlockSpec((1, d), lambda b, t: (0, 0)))
        args.append(final_g.reshape(1, d))
    return pl.pallas_call(
        functools.partial(_ffn_kernel, i_mod=i_mod, d_ff=d_ff, mixer=mixer is not None,
                          final=final_g is not None),
        grid=(bsz, rows // tm),
        in_specs=in_specs,
        out_specs=pl.BlockSpec((1, tm, d), lambda b, t: (b, t, 0)),
        out_shape=jax.ShapeDtypeStruct(x.shape, F32),
        compiler_params=_cparams("parallel", "parallel"),
        name="ffn",
    )(*args)


def _rope(x, cos_t, sin_a, sin_b):
    width = x.shape[-1]
    half = HEAD_DIM // 2
    up = pltpu.roll(x, width - half, axis=1)
    dn = pltpu.roll(x, half, axis=1)
    return x * cos_t + up * sin_a + dn * sin_b


def _proj_kernel(*refs, rope):
    if rope:
        (x_ref, mod_ref, g_ref, w_ref, cos_ref, sa_ref, sb_ref,
         xl_ref, gl_ref, zh_ref, q_ref, kv_ref) = refs
    else:
        (x_ref, mod_ref, g_ref, w_ref, xl_ref, gl_ref, zh_ref, q_ref, kv_ref) = refs
    x = x_ref[0]
    m = mod_ref[...]
    h = _ada_norm(x, g_ref[...], m[3:4], m[4:5])
    z = jnp.dot(h.astype(BF16), w_ref[...], preferred_element_type=F32)
    o = 0
    xl_ref[...] = z[:, o:o + D_LRU]; o += D_LRU
    gl_ref[...] = z[:, o:o + D_LRU]; o += D_LRU
    zh_ref[...] = z[:, o:o + 3 * D_HY]; o += 3 * D_HY
    q = z[:, o:o + D_ATTN]; o += D_ATTN
    k = z[:, o:o + D_KV]; o += D_KV
    v = z[:, o:o + D_KV]
    if rope:
        cos_t, sin_a, sin_b = cos_ref[...], sa_ref[...], sb_ref[...]
        q = _rope(q, cos_t, sin_a, sin_b)
        k = _rope(k, cos_t[:, :D_KV], sin_a[:, :D_KV], sin_b[:, :D_KV])
    q_ref[0] = q * (HEAD_DIM ** -0.5 * LOG2E)
    low = lax.broadcasted_iota(jnp.int32, k.shape, 1) < HEAD_DIM
    parts = [jnp.where(low, k, 0.0), jnp.where(low, pltpu.roll(k, HEAD_DIM, axis=1), 0.0),
             jnp.where(low, v, 0.0), jnp.where(low, pltpu.roll(v, HEAD_DIM, axis=1), 0.0)]
    kv_ref[0] = jnp.concatenate(parts, axis=1).astype(BF16)


def _input_proj(x, mod, mod_row, g, w_in, layer, rope_tabs):
    bsz, rows, d = x.shape
    tm = _row_tile(rows)
    rope = rope_tabs is not None
    in_specs = [
        pl.BlockSpec((1, tm, d), lambda b, t: (b, t, 0)),
        pl.BlockSpec((None, N_MOD, d), lambda b, t: (mod_row(b), 0, 0)),
        pl.BlockSpec((1, d), lambda b, t: (0, 0)),
        _resident((None,) + w_in.shape[1:], (layer, 0, 0)),
    ]
    args = [x, mod, g.reshape(1, d), w_in]
    if rope:
        in_specs += [pl.BlockSpec((tm, D_ATTN), lambda b, t: (t, 0))] * 3
        args += list(rope_tabs)
    out_shape = [
        jax.ShapeDtypeStruct((rows, bsz * D_LRU), F32),
        jax.ShapeDtypeStruct((rows, bsz * D_LRU), F32),
        jax.ShapeDtypeStruct((rows, bsz * 3 * D_HY), F32),
        jax.ShapeDtypeStruct((bsz, rows, D_ATTN), F32),
        jax.ShapeDtypeStruct((bsz, rows, 2 * N_KVH * LANES), BF16),
    ]
    out_specs = [
        pl.BlockSpec((tm, D_LRU), lambda b, t: (t, b)),
        pl.BlockSpec((tm, D_LRU), lambda b, t: (t, b)),
        pl.BlockSpec((tm, 3 * D_HY), lambda b, t: (t, b)),
        pl.BlockSpec((1, tm, D_ATTN), lambda b, t: (b, t, 0)),
        pl.BlockSpec((1, tm, 2 * N_KVH * LANES), lambda b, t: (b, t, 0)),
    ]
    return pl.pallas_call(
        functools.partial(_proj_kernel, rope=rope),
        grid=(bsz, rows // tm),
        in_specs=in_specs,
        out_specs=out_specs,
        out_shape=out_shape,
        compiler_params=_cparams("parallel", "parallel"),
        name="input_proj",
    )(*args)


def _scan_chunk(a, b, row, reverse):
    for s in (1, 2, 4):
        if reverse:
            ok = row < SUBLANES - s
            sh = SUBLANES - s
        else:
            ok = row >= s
            sh = s
        a_sh = pltpu.roll(a, sh, axis=0)
        b_sh = pltpu.roll(b, sh, axis=0)
        b = jnp.where(ok, a * b_sh + b, b)
        a = jnp.where(ok, a * a_sh, a)
    return a, b


def _lru_kernel(xl_ref, gl_ref, xc_ref, gc_ref, cw_ref, cb_ref, wa_ref, ba_ref, wx_ref, bx_ref,
                lam_ref, y_ref, yc_ref, a_s, b_s, *, n_lat, n_ctx, tile):
    width = xl_ref.shape[1]
    cw = cw_ref[...]
    cb = cb_ref[...]
    y_ref[...] = _dwconv_rows(xl_ref[...], cw, cb, LRU_LEFT)
    yc_ref[...] = _dwconv_rows(xc_ref[...], cw, cb, LRU_LEFT)

    def coeffs(u_ref, base, n_rows):
        def body(i, carry):
            r0 = pl.multiple_of(i * tile, tile)
            u = u_ref[pl.ds(r0, tile), :]
            ub = u.astype(BF16)
            for d in range(2):
                r = _sigmoid(jnp.dot(ub, wa_ref[d], preferred_element_type=F32) + ba_ref[d])
                ig = _sigmoid(jnp.dot(ub, wx_ref[d], preferred_element_type=F32) + bx_ref[d])
                log_a = (-LRU_C * r) * _softplus(-lam_ref[d])
                dst = pl.ds(pl.multiple_of(base + r0, SUBLANES), tile)
                a = jnp.exp(log_a)
                a_s[d, dst, :] = a
                b_s[d, dst, :] = jnp.sqrt(-jnp.tanh(log_a) * (1.0 + a * a)) * (ig * u)
            return carry
        lax.fori_loop(0, n_rows // tile, body, 0)

    coeffs(yc_ref, 0, n_ctx)
    coeffs(y_ref, n_ctx, n_lat)

    row = lax.broadcasted_iota(jnp.int32, (SUBLANES, width), 0)

    def scan_group(d, group, h, reverse):
        r0 = pl.multiple_of(group * SCAN_GROUP, SCAN_GROUP)
        a = a_s[d, pl.ds(r0, SCAN_GROUP), :]
        b = b_s[d, pl.ds(r0, SCAN_GROUP), :]
        order = range(SCAN_GROUP // SUBLANES)
        parts = [_scan_chunk(a[c * SUBLANES:(c + 1) * SUBLANES], b[c * SUBLANES:(c + 1) * SUBLANES], row, reverse)
                 for c in order]
        for c in (reversed(order) if reverse else order):
            hh = parts[c][0] * h + parts[c][1]
            b_s[d, pl.ds(r0 + c * SUBLANES, SUBLANES), :] = hh
            h = hh[0:1, :] if reverse else hh[SUBLANES - 1:SUBLANES, :]
        return h

    ng_ctx = n_ctx // SCAN_GROUP
    ng_all = (n_ctx + n_lat) // SCAN_GROUP
    h0 = jnp.zeros((1, width), F32)

    def ctx_body(j, hs):
        return (scan_group(0, j, hs[0], False), scan_group(1, ng_ctx - 1 - j, hs[1], True))

    def lat_body(j, hs):
        return (scan_group(0, ng_ctx + j, hs[0], False), scan_group(1, ng_all - 1 - j, hs[1], True))

    hs = lax.fori_loop(0, ng_ctx, ctx_body, (h0, h0))
    lax.fori_loop(0, ng_all - ng_ctx, lat_body, hs)

    def finish(o_ref, g_ref, base, n_rows):
        def body(i, carry):
            r0 = pl.multiple_of(i * tile, tile)
            src = pl.ds(pl.multiple_of(base + r0, SUBLANES), tile)
            hsum = b_s[0, src, :] + b_s[1, src, :]
            o_ref[pl.ds(r0, tile), :] = hsum * _gelu_tanh(g_ref[pl.ds(r0, tile), :])
            return carry
        lax.fori_loop(0, n_rows // tile, body, 0)

    finish(yc_ref, gc_ref, 0, n_ctx)
    finish(y_ref, gl_ref, n_ctx, n_lat)


def _block_diag(w):
    two, nb, bs, _ = w.shape
    eye = jnp.eye(nb, dtype=w.dtype)
    return jnp.einsum('dnij,nm->dnimj', w, eye).reshape(two, nb * bs, nb * bs)


def _rglru(xl, gl, xlc, glc, conv_w, conv_b, wa, ba, wx, bx, lam, bsz):
    n_lat, n_ctx = xl.shape[0], xlc.shape[0]
    width = LANES
    per_b = D_LRU // width
    tile = math.gcd(256, math.gcd(n_lat, n_ctx))
    wa_bd = _block_diag(wa).astype(BF16)
    wx_bd = _block_diag(wx).astype(BF16)
    col = lambda b, j: (0, b * per_b + j)
    par = lambda b, j: (0, j)
    par3 = lambda b, j: (0, 0, j)
    return pl.pallas_call(
        functools.partial(_lru_kernel, n_lat=n_lat, n_ctx=n_ctx, tile=tile),
        grid=(bsz, per_b),
        in_specs=[
            pl.BlockSpec((n_lat, width), col),
            pl.BlockSpec((n_lat, width), col),
            pl.BlockSpec((n_ctx, width), col),
            pl.BlockSpec((n_ctx, width), col),
            pl.BlockSpec((conv_w.shape[0], width), par),
            pl.BlockSpec((1, width), par),
            pl.BlockSpec((2, width, width), lambda b, j: (0, j, j)),
            pl.BlockSpec((2, 1, width), par3),
            pl.BlockSpec((2, width, width), lambda b, j: (0, j, j)),
            pl.BlockSpec((2, 1, width), par3),
            pl.BlockSpec((2, 1, width), par3),
        ],
        out_specs=[pl.BlockSpec((n_lat, width), col), pl.BlockSpec((n_ctx, width), col)],
        out_shape=[jax.ShapeDtypeStruct(xl.shape, F32), jax.ShapeDtypeStruct(xlc.shape, F32)],
        scratch_shapes=[pltpu.VMEM((2, n_ctx + n_lat, width), F32),
                        pltpu.VMEM((2, n_ctx + n_lat, width), F32)],
        compiler_params=_cparams("parallel", "parallel"),
        name="rglru",
    )(xl, gl, xlc, glc, conv_w, conv_b.reshape(1, -1), wa_bd, ba.reshape(2, 1, -1), wx_bd,
      bx.reshape(2, 1, -1), lam.reshape(2, 1, -1))


def _hy_pre_kernel(z0_ref, z1_ref, z2_ref, w0_ref, w1_ref, w2_ref, b0_ref, b1_ref, b2_ref,
                   ub_ref, u_ref, x0_ref):
    x0 = _dwconv_rows(z0_ref[...], w0_ref[...], b0_ref[...], HY_LEFT)
    x1 = _dwconv_rows(z1_ref[...], w1_ref[...], b1_ref[...], HY_LEFT)
    v = _dwconv_rows(z2_ref[...], w2_ref[...], b2_ref[...], HY_LEFT)
    u = x1 * v
    u_ref[...] = u
    ub_ref[...] = u.astype(BF16)
    x0_ref[...] = x0


def _hyena_pre(zh, conv_w, conv_b, bsz):
    rows = zh.shape[0]
    width = LANES
    per_b = D_HY // width
    zspec = lambda part: pl.BlockSpec((rows, width), lambda b, j: (0, b * 3 * per_b + part * per_b + j))
    wspec = lambda part: pl.BlockSpec((conv_w.shape[0], width), lambda b, j: (0, part * per_b + j))
    bspec = lambda part: pl.BlockSpec((1, width), lambda b, j: (0, part * per_b + j))
    ospec = pl.BlockSpec((rows, width), lambda b, j: (0, b * per_b + j))
    cb = conv_b.reshape(1, -1)
    return pl.pallas_call(
        _hy_pre_kernel,
        grid=(bsz, per_b),
        in_specs=[zspec(0), zspec(1), zspec(2), wspec(0), wspec(1), wspec(2), bspec(0), bspec(1), bspec(2)],
        out_specs=[ospec, ospec, ospec],
        out_shape=[jax.ShapeDtypeStruct((rows, bsz * D_HY), BF16),
                   jax.ShapeDtypeStruct((rows, bsz * D_HY), F32),
                   jax.ShapeDtypeStruct((rows, bsz * D_HY), F32)],
        compiler_params=_cparams("parallel", "parallel"),
        name="hyena_pre",
    )(zh, zh, zh, conv_w, conv_w, conv_w, cb, cb, cb)


def _hy_filter_kernel(z_ref, fw0_ref, fb0_ref, fwin_ref, fbin_ref, freq_ref, fwl_ref, dl_ref, o_ref, *, tile):
    hp = lax.Precision.HIGHEST
    z = z_ref[...]
    fr = freq_ref[...]
    hdn = jnp.sin(fr * (jnp.dot(z, fw0_ref[...], preferred_element_type=F32, precision=hp) + fb0_ref[...]))
    for j in range(fwin_ref.shape[0]):
        hdn = jnp.sin(fr * (jnp.dot(hdn, fwin_ref[j], preferred_element_type=F32, precision=hp) + fbin_ref[j]))
    k = jnp.dot(hdn, fwl_ref[...], preferred_element_type=F32, precision=hp)
    decay = jnp.exp(-z[:, 0:1] * dl_ref[...])
    k_fwd = k[:, :D_HY] * decay
    k_bwd = k[:, D_HY:] * decay
    row = lax.broadcasted_iota(jnp.int32, k_bwd.shape, 0) + pl.program_id(0) * tile
    k_bwd = jnp.where(row == 0, 0.0, k_bwd)
    o_ref[...] = jnp.concatenate([k_fwd, k_bwd], axis=-1).astype(BF16)


def _hyena_filter_taps(n, fw0, fb0, fw_in, fb_in, freq, fw_last):
    t = jnp.linspace(0.0, 1.0, n, dtype=F32)[:, None]
    w = 2.0 * math.pi * jnp.arange(n, dtype=F32)[:, None] / n
    f = jnp.linspace(1e-4, HY_BANDS - 1, HY_BANDS, dtype=F32)[None, :]
    z = jnp.concatenate([t, jnp.cos(f * w), -jnp.sin(f * w)], axis=-1)
    z = jnp.pad(z, ((0, 0), (0, LANES - HY_EMB)))
    fw0p = jnp.pad(fw0, ((0, LANES - HY_EMB), (0, 0)))
    max_decay = math.log(HY_TARGET) / HY_FAST
    min_decay = math.log(HY_TARGET) / HY_SLOW
    deltas = jnp.abs(jnp.linspace(min_decay, max_decay, D_HY, dtype=F32))[None, :]
    hid = fw0.shape[1]
    tile = min(512, n)
    full = lambda a: pl.BlockSpec(a.shape, lambda i: (0,) * a.ndim)
    args = [fw0p, fb0.reshape(1, hid), fw_in, fb_in.reshape(-1, 1, hid), freq.reshape(1, hid), fw_last, deltas]
    return pl.pallas_call(
        functools.partial(_hy_filter_kernel, tile=tile),
        grid=(n // tile,),
        in_specs=[pl.BlockSpec((tile, LANES), lambda i: (i, 0))] + [full(a) for a in args],
        out_specs=pl.BlockSpec((tile, 2 * D_HY), lambda i: (i, 0)),
        out_shape=jax.ShapeDtypeStruct((n, 2 * D_HY), BF16),
        compiler_params=_cparams("parallel"),
        name="hyena_filter",
    )(z, *args)


def _dft_expand_kernel(ca_ref, sa_ref, cb_ref, sb_ref, cm_ref, sm_ref, *, tile):
    ca, sa, cb, sb = ca_ref[...], sa_ref[...], cb_ref[...], sb_ref[...]
    row = lax.broadcasted_iota(jnp.int32, cb.shape, 0) + pl.program_id(0) * tile
    lane = lax.broadcasted_iota(jnp.int32, cb.shape, 1)
    alt = jnp.where(lane % 2 == 0, 1.0, -1.0)
    for s1 in range(cm_ref.shape[1] // LANES):
        c1 = ca[:, s1:s1 + 1]
        d1 = sa[:, s1:s1 + 1]
        cols = slice(s1 * LANES, (s1 + 1) * LANES)
        cm_ref[:, cols] = (c1 * cb - d1 * sb).astype(BF16)
        sm_ref[:, cols] = jnp.where(row == 0, alt, -(d1 * cb + c1 * sb)).astype(BF16)


def _dft_matrices(n):
    two_n = 2 * n
    f = jnp.arange(n, dtype=jnp.int32)[:, None]
    s_hi = jnp.arange(n // LANES, dtype=jnp.int32)[None, :] * LANES
    s_lo = jnp.arange(LANES, dtype=jnp.int32)[None, :]
    ang_a = ((f * s_hi) % two_n).astype(F32) * (2.0 * math.pi / two_n)
    ang_b = ((f * s_lo) % two_n).astype(F32) * (2.0 * math.pi / two_n)
    tile = min(DFT_TILE, n)
    hi_spec = pl.BlockSpec((tile, n // LANES), lambda i: (i, 0))
    lo_spec = pl.BlockSpec((tile, LANES), lambda i: (i, 0))
    return pl.pallas_call(
        functools.partial(_dft_expand_kernel, tile=tile),
        grid=(n // tile,),
        in_specs=[hi_spec, hi_spec, lo_spec, lo_spec],
        out_specs=[pl.BlockSpec((tile, n), lambda i: (i, 0))] * 2,
        out_shape=[jax.ShapeDtypeStruct((n, n), BF16)] * 2,
        compiler_params=_cparams("parallel"),
        name="dft_matrices",
    )(jnp.cos(ang_a), jnp.sin(ang_a), jnp.cos(ang_b), jnp.sin(ang_b))


def _dft_filter_kernel(cm_ref, sm_ref, kk_ref, kre_ref, kim_ref, *, tile):
    kk = kk_ref[...]
    xre = jnp.dot(cm_ref[...], kk, preferred_element_type=F32)
    xim = jnp.dot(sm_ref[...], kk, preferred_element_type=F32)
    row = lax.broadcasted_iota(jnp.int32, (tile, D_HY), 0) + pl.program_id(0) * tile
    kre_ref[...] = xre[:, :D_HY] + xre[:, D_HY:]
    kim_ref[...] = jnp.where(row == 0, xim[:, :D_HY] + xim[:, D_HY:], xim[:, :D_HY] - xim[:, D_HY:])


def _dft_filter(cm, sm, kk):
    n = cm.shape[0]
    tile = min(DFT_FILTER_TILE, n)
    return pl.pallas_call(
        functools.partial(_dft_filter_kernel, tile=tile),
        grid=(n // tile,),
        in_specs=[pl.BlockSpec((tile, n), lambda f: (f, 0)),
                  pl.BlockSpec((tile, n), lambda f: (f, 0)),
                  _resident(kk.shape)],
        out_specs=[pl.BlockSpec((tile, D_HY), lambda f: (f, 0))] * 2,
        out_shape=[jax.ShapeDtypeStruct((n, D_HY), F32)] * 2,
        compiler_params=_cparams("parallel"),
        name="dft_filter",
    )(cm, sm, kk)


def _dft_fwd_kernel(cm_ref, sm_ref, u_ref, kre_ref, kim_ref, yre_ref, yim_ref, *, tile):
    n_freq = cm_ref.shape[1]
    u = u_ref[...]
    xre = jnp.dot(cm_ref[...], u, preferred_element_type=F32)
    xim = jnp.dot(sm_ref[...], u, preferred_element_type=F32)
    kre = kre_ref[...]
    kim = kim_ref[...]
    row = lax.broadcasted_iota(jnp.int32, xre.shape, 0) + pl.program_id(0) * tile
    first = row == 0
    scale = jnp.where(first, 0.5 / n_freq, 1.0 / n_freq)
    yre_ref[...] = (scale * (xre * kre - jnp.where(first, 0.0, xim * kim))).astype(BF16)
    yim_ref[...] = (scale * jnp.where(first, xim * kim, xre * kim + xim * kre)).astype(BF16)


def _dft_fwd(cm, sm, ub, kre, kim):
    n = cm.shape[0]
    cols = ub.shape[1]
    tile = min(DFT_TILE, n)
    return pl.pallas_call(
        functools.partial(_dft_fwd_kernel, tile=tile),
        grid=(n // tile, cols // D_HY),
        in_specs=[pl.BlockSpec((tile, n), lambda f, c: (f, 0)),
                  pl.BlockSpec((tile, n), lambda f, c: (f, 0)),
                  pl.BlockSpec((n, D_HY), lambda f, c: (0, c)),
                  pl.BlockSpec((tile, D_HY), lambda f, c: (f, 0)),
                  pl.BlockSpec((tile, D_HY), lambda f, c: (f, 0))],
        out_specs=[pl.BlockSpec((tile, D_HY), lambda f, c: (f, c))] * 2,
        out_shape=[jax.ShapeDtypeStruct((n, cols), BF16)] * 2,
        compiler_params=_cparams("parallel", "parallel"),
        name="dft_fwd",
    )(cm, sm, ub, kre, kim)


def _dft_inv_kernel(ci_ref, si_ref, yre_ref, yim_ref, u_ref, x0_ref, skip_ref, o_ref, *, tile):
    y_cos = jnp.dot(ci_ref[...], yre_ref[...], preferred_element_type=F32)
    y_sin = jnp.dot(si_ref[...], yim_ref[...], preferred_element_type=F32)
    row = lax.broadcasted_iota(jnp.int32, y_cos.shape, 0) + pl.program_id(0) * tile
    nyq = jnp.where(row % 2 == 0, 1.0, -1.0) * yim_ref[0:1, :].astype(F32)
    y = y_cos + jnp.where(row == 0, 0.0, y_sin) + nyq
    o_ref[...] = x0_ref[...] * (y + u_ref[...] * skip_ref[...])


def _dft_inv(ci, si, yre, yim, u, x0, skip):
    n = ci.shape[0]
    cols = yre.shape[1]
    tile = min(DFT_TILE, n)
    return pl.pallas_call(
        functools.partial(_dft_inv_kernel, tile=tile),
        grid=(n // tile, cols // D_HY),
        in_specs=[pl.BlockSpec((tile, n), lambda t, c: (t, 0)),
                  pl.BlockSpec((tile, n), lambda t, c: (t, 0)),
                  pl.BlockSpec((n, D_HY), lambda t, c: (0, c)),
                  pl.BlockSpec((n, D_HY), lambda t, c: (0, c)),
                  pl.BlockSpec((tile, D_HY), lambda t, c: (t, c)),
                  pl.BlockSpec((tile, D_HY), lambda t, c: (t, c)),
                  pl.BlockSpec((1, D_HY), lambda t, c: (0, 0))],
        out_specs=pl.BlockSpec((tile, D_HY), lambda t, c: (t, c)),
        out_shape=jax.ShapeDtypeStruct((n, cols), F32),
        compiler_params=_cparams("parallel", "parallel"),
        name="dft_inv",
    )(ci, si, yre, yim, u, x0, skip.reshape(1, D_HY))


def _hyena(zh, bsz, conv_w, conv_b, filt, dft, skip):
    cm, sm = dft
    ub, u, x0 = _hyena_pre(zh, conv_w, conv_b, bsz)
    kk = _hyena_filter_taps(zh.shape[0], *filt)
    kre, kim = _dft_filter(cm, sm, kk)
    yre, yim = _dft_fwd(cm, sm, ub, kre, kim)
    return _dft_inv(cm, sm, yre, yim, u, x0, skip)


def _attn_kernel(*refs, windowed):
    if windowed:
        (q_ref, kvp_ref, kvc_ref, kvn_ref, kvx_ref, sink_ref, o_ref) = refs
        kv = jnp.concatenate([kvp_ref[0], kvc_ref[0], kvn_ref[0], kvx_ref[0]], axis=0)
    else:
        (q_ref, kvx_ref, sink_ref, o_ref) = refs
        kv = kvx_ref[0]
    k_low = [kv[:, g * LANES:(g + 1) * LANES] for g in range(N_KVH)]
    ones = jnp.ones((kv.shape[0], LANES), BF16)
    v_aug = [jnp.concatenate([kv[:, (N_KVH + g) * LANES:(N_KVH + g + 1) * LANES], ones], axis=1)
             for g in range(N_KVH)]
    q = q_ref[0]
    nq = q.shape[0]
    if windowed:
        i = pl.program_id(1)
        last = pl.num_programs(1) - 1
        r = lax.broadcasted_iota(jnp.int32, (nq, ATT_BLOCK), 0)
        j = lax.broadcasted_iota(jnp.int32, (nq, ATT_BLOCK), 1)
        valid_prev = j >= r + jnp.where(i == 0, ATT_BLOCK, 0)
        valid_next = j <= r - jnp.where(i == last, ATT_BLOCK, 0)
    cols_per_g = D_ATTN // LANES // N_KVH
    for g in range(N_KVH):
        cols = [q[:, c * LANES:(c + 1) * LANES] for c in range(g * cols_per_g, (g + 1) * cols_per_g)]
        qg = jnp.concatenate(cols + [pltpu.roll(cq, HEAD_DIM, axis=1) for cq in cols], axis=0).astype(BF16)
        s_all = lax.dot_general(qg, k_low[g], (((1,), (1,)), ((), ())), preferred_element_type=F32)
        es, sinks = [], []
        for hb in range(2 * cols_per_g):
            h = 2 * (g * cols_per_g + hb % cols_per_g) + hb // cols_per_g
            s = s_all[hb * nq:(hb + 1) * nq]
            if windowed:
                s = jnp.concatenate([
                    jnp.where(valid_prev, s[:, :ATT_BLOCK], NEG_INF),
                    s[:, ATT_BLOCK:2 * ATT_BLOCK],
                    jnp.where(valid_next, s[:, 2 * ATT_BLOCK:3 * ATT_BLOCK], NEG_INF),
                    s[:, 3 * ATT_BLOCK:]], axis=1)
            sk = sink_ref[h:h + 1, 0:1] * LOG2E
            m = jnp.maximum(jnp.max(s, axis=-1, keepdims=True), sk)
            es.append(jnp.exp2(s - m).astype(BF16))
            sinks.append(jnp.exp2(sk - m))
        o_all = jnp.dot(jnp.concatenate(es, axis=0), v_aug[g], preferred_element_type=F32)
        outs = []
        for hb in range(2 * cols_per_g):
            o = o_all[hb * nq:(hb + 1) * nq]
            outs.append(o[:, :LANES] / (o[:, LANES:] + sinks[hb]))
        for ci in range(cols_per_g):
            c = g * cols_per_g + ci
            o_ref[0, :, c * LANES:(c + 1) * LANES] = outs[ci] + pltpu.roll(outs[cols_per_g + ci], HEAD_DIM, axis=1)


def _attention(q, kv, kvx, sink, windowed):
    bsz, lq, _ = q.shape
    n_ctx, kv_w = kvx.shape[1:]
    nb = lq // ATT_BLOCK
    sink_t = jnp.broadcast_to(sink.reshape(N_QH, 1), (N_QH, LANES))
    qspec = pl.BlockSpec((1, ATT_BLOCK, D_ATTN), lambda b, i: (b, i, 0))
    xspec = pl.BlockSpec((1, n_ctx, kv_w), lambda b, i: (b, 0, 0))
    sspec = pl.BlockSpec((N_QH, LANES), lambda b, i: (0, 0))
    if windowed:
        prev = pl.BlockSpec((1, ATT_BLOCK, kv_w), lambda b, i: (b, jnp.maximum(i - 1, 0), 0))
        cur = pl.BlockSpec((1, ATT_BLOCK, kv_w), lambda b, i: (b, i, 0))
        nxt = pl.BlockSpec((1, ATT_BLOCK, kv_w), lambda b, i: (b, jnp.minimum(i + 1, nb - 1), 0))
        in_specs = [qspec, prev, cur, nxt, xspec, sspec]
        args = (q, kv, kv, kv, kvx, sink_t)
    else:
        in_specs = [qspec, xspec, sspec]
        args = (q, kvx, sink_t)
    return pl.pallas_call(
        functools.partial(_attn_kernel, windowed=windowed),
        grid=(bsz, nb),
        in_specs=in_specs,
        out_specs=pl.BlockSpec((1, ATT_BLOCK, D_ATTN), lambda b, i: (b, i, 0)),
        out_shape=jax.ShapeDtypeStruct(q.shape, F32),
        compiler_params=_cparams("parallel", "parallel"),
        name="attention",
    )(*args)


def _rope_tables(n_lat):
    rows = n_lat // GRID_W
    r = jnp.repeat(jnp.arange(rows, dtype=F32), GRID_W)
    col = jnp.tile(jnp.arange(GRID_W, dtype=F32), rows)
    inv = ROPE_THETA ** (-jnp.arange(ROPE_PAIRS_AXIS, dtype=F32) / ROPE_PAIRS_AXIS)
    ang = jnp.concatenate([r[:, None] * inv, col[:, None] * inv], axis=-1)
    cos, sin = jnp.cos(ang), jnp.sin(ang)
    zero = jnp.zeros_like(sin)
    reps = D_ATTN // HEAD_DIM
    cos_t = jnp.tile(jnp.concatenate([cos, cos], axis=-1), (1, reps))
    sin_a = jnp.tile(jnp.concatenate([-sin, zero], axis=-1), (1, reps))
    sin_b = jnp.tile(jnp.concatenate([zero, sin], axis=-1), (1, reps))
    return cos_t, sin_a, sin_b


def kernel(x, c, ctx, c_ctx, w_mod, b_mod, norm_g, ffn_w1, ffn_w2, w_in, w_out, lru_conv_w, lru_conv_b,
           lru_wa, lru_ba, lru_wx, lru_bx, lru_lam, hy_conv_w, hy_conv_b, hy_fw0, hy_fb0, hy_fw_in,
           hy_fb_in, hy_freq, hy_fw_last, hy_skip, attn_sink, final_g):
    bsz, n_lat, d = x.shape
    n_ctx = ctx.shape[1]
    depth = w_mod.shape[0]
    assert n_lat % ATT_BLOCK == 0 and n_ctx % ATT_BLOCK == 0 and n_lat % GRID_W == 0
    assert ATT_BLOCK % SCAN_GROUP == 0 and ATT_BLOCK % LANES == 0

    mod_rows = -(-(bsz + 1) // SUBLANES) * SUBLANES
    c_rows = jnp.zeros((mod_rows, d), F32).at[:bsz].set(c).at[bsz].set(c_ctx)
    mod_all = _modulation(c_rows, w_mod, b_mod).reshape(depth, mod_rows, N_MOD, d)
    lat_row = lambda b: b
    ctx_row = lambda b: bsz

    rope_tabs = _rope_tables(n_lat)
    dft_lat = _dft_matrices(n_lat)
    w1_b = ffn_w1.astype(BF16)
    w2_b = ffn_w2.astype(BF16)
    w_in_b = w_in.astype(BF16)
    w_out_b = w_out.astype(BF16)

    xc = ctx
    for l in range(depth):
        need_ctx = l < depth - 1
        mod = mod_all[l]
        filt = (hy_fw0[l], hy_fb0[l], hy_fw_in[l], hy_fb_in[l], hy_freq[l], hy_fw_last[l])

        x = _ffn(x, mod, lat_row, norm_g[l, 0], w1_b, w2_b, l, 0, 0)
        xc = _ffn(xc, mod, ctx_row, norm_g[l, 0], w1_b, w2_b, l, 0, 0)

        xl, gl, zh, q, kv = _input_proj(x, mod, lat_row, norm_g[l, 1], w_in_b, l, rope_tabs)
        xlc, glc, zhc, qc, kvc = _input_proj(xc, mod, ctx_row, norm_g[l, 1], w_in_b, l, None)

        y_lru, yc_lru = _rglru(xl, gl, xlc, glc, lru_conv_w[l], lru_conv_b[l], lru_wa[l], lru_ba[l],
                               lru_wx[l], lru_bx[l], lru_lam[l], bsz)
        y_hy = _hyena(zh, bsz, hy_conv_w[l], hy_conv_b[l], filt, dft_lat, hy_skip[l])
        y_att = _attention(q, kv, kvc, attn_sink[l], True)
        x = _ffn(x, mod, lat_row, norm_g[l, 2], w1_b, w2_b, l, 1, 6, mixer=(y_lru, y_hy, y_att, w_out_b),
                 final_g=None if need_ctx else final_g)

        if need_ctx:
            yc_hy = _hyena(zhc, bsz, hy_conv_w[l], hy_conv_b[l], filt, _dft_matrices(n_ctx), hy_skip[l])
            yc_att = _attention(qc, None, kvc, attn_sink[l], False)
            xc = _ffn(xc, mod, ctx_row, norm_g[l, 2], w1_b, w2_b, l, 1, 6,
                      mixer=(yc_lru, yc_hy, yc_att, w_out_b))
    return x
```

```python
import functools
import math

import jax
import jax.numpy as jnp
from jax import lax
from jax.experimental import pallas as pl
from jax.experimental.pallas import tpu as pltpu

F32 = jnp.float32
BF16 = jnp.bfloat16

NORM_EPS = 1e-6
N_MOD = 9
MACARON_W = 0.5
D_LRU = 256
LRU_BLOCKS = 4
LRU_C = 8.0
LRU_LEFT = 2
D_HY = 256
HY_LEFT = 1
HY_EMB = 33
HY_BANDS = (HY_EMB - 1) // 2
HY_FAST = 0.3
HY_SLOW = 1.5
HY_TARGET = 1e-2
N_QH = 8
N_KVH = 2
HEAD_DIM = 64
D_ATTN = N_QH * HEAD_DIM
D_KV = N_KVH * HEAD_DIM
WINDOW = 128
ATT_BLOCK = 128
ATT_STEP = 2
GRID_W = 64
ROPE_THETA = 10000.0
ROPE_PAIRS_AXIS = HEAD_DIM // 4
NEG_INF = -1e30
LOG2E = math.log2(math.e)

LANES = 128
SUBLANES = 8
VMEM_LIMIT_BYTES = 56 * 1024 * 1024
ROW_TILE = 512
DFT_TILE = 1024
DFT_FILTER_TILE = 512
SCAN_GROUP = 64


def _cparams(*sem):
    return pltpu.CompilerParams(dimension_semantics=sem, vmem_limit_bytes=VMEM_LIMIT_BYTES)


def _row_tile(rows):
    return min(ROW_TILE, rows)


def _resident(shape, index=None):
    index = (0,) * len(shape) if index is None else index
    return pl.BlockSpec(shape, lambda *_: index, pipeline_mode=pl.Buffered(1))


def _ada_norm(x, g, shift, scale):
    y = x * lax.rsqrt(jnp.mean(x * x, axis=-1, keepdims=True) + NORM_EPS)
    return (y * g) * (1.0 + scale) + shift


def _sigmoid(x):
    return 0.5 * (1.0 + jnp.tanh(0.5 * x))


def _gelu_tanh(x):
    return 0.5 * x * (1.0 + jnp.tanh(math.sqrt(2.0 / math.pi) * (x + 0.044715 * (x * x * x))))


def _softplus(x):
    return jnp.maximum(x, 0.0) + jnp.log1p(jnp.exp(-jnp.abs(x)))


def _dwconv_rows(x, w, bias, left):
    rows = x.shape[0]
    row = lax.broadcasted_iota(jnp.int32, x.shape, 0)
    out = jnp.broadcast_to(bias, x.shape)
    for k in range(w.shape[0]):
        off = k - left
        if off == 0:
            term = x
        else:
            shifted = pltpu.roll(x, (-off) % rows, axis=0)
            ok = (row + off >= 0) & (row + off < rows)
            term = jnp.where(ok, shifted, 0.0)
        out = out + term * w[k:k + 1, :]
    return out


def _mod_kernel(c_ref, w_ref, b_ref, o_ref):
    cv = c_ref[...]
    s = cv * _sigmoid(cv)
    o_ref[0] = jnp.dot(s, w_ref[0], preferred_element_type=F32,
                       precision=lax.Precision.HIGHEST) + b_ref[0]


def _modulation(c_rows, w_mod, b_mod):
    depth, d, nd = w_mod.shape
    rows = c_rows.shape[0]
    tn = nd // 8
    return pl.pallas_call(
        _mod_kernel,
        grid=(depth, nd // tn),
        in_specs=[
            pl.BlockSpec((rows, d), lambda l, j: (0, 0)),
            pl.BlockSpec((1, d, tn), lambda l, j: (l, 0, j)),
            pl.BlockSpec((1, 1, tn), lambda l, j: (l, 0, j)),
        ],
        out_specs=pl.BlockSpec((1, rows, tn), lambda l, j: (l, 0, j)),
        out_shape=jax.ShapeDtypeStruct((depth, rows, nd), F32),
        compiler_params=_cparams("parallel", "parallel"),
        name="modulation",
    )(c_rows, w_mod, b_mod.reshape(depth, 1, nd))


def _ffn_kernel(*refs, i_mod, d_ff, mixer, final):
    refs = list(refs)
    o_ref = refs.pop()
    x_ref, mod_ref, g_ref, w1_ref, w2_ref = refs[:5]
    rest = refs[5:]
    x = x_ref[0]
    m = mod_ref[...]
    if mixer:
        yl_ref, yh_ref, ya_ref, wo_ref = rest[:4]
        rest = rest[4:]
        y = (jnp.dot(yl_ref[...].astype(BF16), wo_ref[0:D_LRU, :], preferred_element_type=F32)
             + jnp.dot(yh_ref[...].astype(BF16), wo_ref[D_LRU:D_LRU + D_HY, :], preferred_element_type=F32)
             + jnp.dot(ya_ref[0].astype(BF16), wo_ref[D_LRU + D_HY:, :], preferred_element_type=F32))
        x = x + m[5:6] * y
    h = _ada_norm(x, g_ref[...], m[i_mod:i_mod + 1], m[i_mod + 1:i_mod + 2])
    ab = jnp.dot(h.astype(BF16), w1_ref[...], preferred_element_type=F32)
    a = ab[:, :d_ff]
    b = ab[:, d_ff:]
    gated = (a * _sigmoid(a)) * b
    y = jnp.dot(gated.astype(BF16), w2_ref[...], preferred_element_type=F32)
    x = x + (MACARON_W * m[i_mod + 2:i_mod + 3]) * y
    if final:
        x = (x * lax.rsqrt(jnp.mean(x * x, axis=-1, keepdims=True) + NORM_EPS)) * rest[0][...]
    o_ref[0] = x


def _ffn(x, mod, mod_row, g, w1, w2, layer, which, i_mod, mixer=None, final_g=None):
    bsz, rows, d = x.shape
    d_ff = w2.shape[2]
    tm = _row_tile(rows)
    in_specs = [
        pl.BlockSpec((1, tm, d), lambda b, t: (b, t, 0)),
        pl.BlockSpec((None, N_MOD, d), lambda b, t: (mod_row(b), 0, 0)),
        pl.BlockSpec((1, d), lambda b, t: (0, 0)),
        _resident((None, None, d, 2 * d_ff), (layer, which, 0, 0)),
        _resident((None, None, d_ff, d), (layer, which, 0, 0)),
    ]
    args = [x, mod, g.reshape(1, d), w1, w2]
    if mixer is not None:
        y_lru, y_hy, y_att, w_out = mixer
        in_specs += [
            pl.BlockSpec((tm, D_LRU), lambda b, t: (t, b)),
            pl.BlockSpec((tm, D_HY), lambda b, t: (t, b)),
            pl.BlockSpec((1, tm, D_ATTN), lambda b, t: (b, t, 0)),
            _resident((None,) + w_out.shape[1:], (layer, 0, 0)),
        ]
        args += [y_lru, y_hy, y_att, w_out]
    if final_g is not None:
        in_specs.append(pl.BlockSpec((1, d), lambda b, t: (0, 0)))
        args.append(final_g.reshape(1, d))
    return pl.pallas_call(
        functools.partial(_ffn_kernel, i_mod=i_mod, d_ff=d_ff, mixer=mixer is not None,
                          final=final_g is not None),
        grid=(bsz, rows // tm),
        in_specs=in_specs,
        out_specs=pl.BlockSpec((1, tm, d), lambda b, t: (b, t, 0)),
        out_shape=jax.ShapeDtypeStruct(x.shape, F32),
        compiler_params=_cparams("parallel", "parallel"),
        name="ffn",
    )(*args)


def _rope(x, cos_t, sin_a, sin_b):
    width = x.shape[-1]
    half = HEAD_DIM // 2
    up = pltpu.roll(x, width - half, axis=1)
    dn = pltpu.roll(x, half, axis=1)
    return x * cos_t + up * sin_a + dn * sin_b


def _proj_kernel(*refs, rope):
    if rope:
        (x_ref, mod_ref, g_ref, w_ref, cos_ref, sa_ref, sb_ref,
         xl_ref, gl_ref, zh_ref, q_ref, kv_ref) = refs
    else:
        (x_ref, mod_ref, g_ref, w_ref, xl_ref, gl_ref, zh_ref, q_ref, kv_ref) = refs
    x = x_ref[0]
    m = mod_ref[...]
    h = _ada_norm(x, g_ref[...], m[3:4], m[4:5])
    z = jnp.dot(h.astype(BF16), w_ref[...], preferred_element_type=F32)
    o = 0
    xl_ref[...] = z[:, o:o + D_LRU]; o += D_LRU
    gl_ref[...] = z[:, o:o + D_LRU]; o += D_LRU
    zh_ref[...] = z[:, o:o + 3 * D_HY]; o += 3 * D_HY
    q = z[:, o:o + D_ATTN]; o += D_ATTN
    k = z[:, o:o + D_KV]; o += D_KV
    v = z[:, o:o + D_KV]
    if rope:
        cos_t, sin_a, sin_b = cos_ref[...], sa_ref[...], sb_ref[...]
        q = _rope(q, cos_t, sin_a, sin_b)
        k = _rope(k, cos_t[:, :D_KV], sin_a[:, :D_KV], sin_b[:, :D_KV])
    q_ref[0] = (q * (HEAD_DIM ** -0.5 * LOG2E)).astype(BF16)
    low = lax.broadcasted_iota(jnp.int32, k.shape, 1) < HEAD_DIM
    parts = [jnp.where(low, k, 0.0), jnp.where(low, pltpu.roll(k, HEAD_DIM, axis=1), 0.0),
             jnp.where(low, v, 0.0), jnp.where(low, pltpu.roll(v, HEAD_DIM, axis=1), 0.0)]
    kv_ref[0] = jnp.concatenate(parts, axis=1).astype(BF16)


def _input_proj(x, mod, mod_row, g, w_in, layer, rope_tabs):
    bsz, rows, d = x.shape
    tm = _row_tile(rows)
    rope = rope_tabs is not None
    in_specs = [
        pl.BlockSpec((1, tm, d), lambda t, b: (b, t, 0)),
        pl.BlockSpec((None, N_MOD, d), lambda t, b: (mod_row(b), 0, 0)),
        pl.BlockSpec((1, d), lambda t, b: (0, 0)),
        _resident((None,) + w_in.shape[1:], (layer, 0, 0)),
    ]
    args = [x, mod, g.reshape(1, d), w_in]
    if rope:
        in_specs += [pl.BlockSpec((tm, D_ATTN), lambda t, b: (t, 0))] * 3
        args += list(rope_tabs)
    out_shape = [
        jax.ShapeDtypeStruct((rows, bsz * D_LRU), F32),
        jax.ShapeDtypeStruct((rows, bsz * D_LRU), F32),
        jax.ShapeDtypeStruct((rows, bsz * 3 * D_HY), F32),
        jax.ShapeDtypeStruct((bsz, rows, D_ATTN), BF16),
        jax.ShapeDtypeStruct((bsz, rows, 2 * N_KVH * LANES), BF16),
    ]
    out_specs = [
        pl.BlockSpec((tm, D_LRU), lambda t, b: (t, b)),
        pl.BlockSpec((tm, D_LRU), lambda t, b: (t, b)),
        pl.BlockSpec((tm, 3 * D_HY), lambda t, b: (t, b)),
        pl.BlockSpec((1, tm, D_ATTN), lambda t, b: (b, t, 0)),
        pl.BlockSpec((1, tm, 2 * N_KVH * LANES), lambda t, b: (b, t, 0)),
    ]
    return pl.pallas_call(
        functools.partial(_proj_kernel, rope=rope),
        grid=(rows // tm, bsz),
        in_specs=in_specs,
        out_specs=out_specs,
        out_shape=out_shape,
        compiler_params=_cparams("parallel", "parallel"),
        name="input_proj",
    )(*args)


def _scan_chunk(a, b, row, reverse):
    for s in (1, 2, 4):
        if reverse:
            ok = row < SUBLANES - s
            sh = SUBLANES - s
        else:
            ok = row >= s
            sh = s
        a_sh = pltpu.roll(a, sh, axis=0)
        b_sh = pltpu.roll(b, sh, axis=0)
        b = jnp.where(ok, a * b_sh + b, b)
        a = jnp.where(ok, a * a_sh, a)
    return a, b


def _lru_kernel(xl_ref, gl_ref, xc_ref, gc_ref, cw_ref, cb_ref, wa_ref, ba_ref, wx_ref, bx_ref,
                lam_ref, y_ref, yc_ref, a_s, b_s, *, n_lat, n_ctx, tile):
    width = xl_ref.shape[1]
    cw = cw_ref[...]
    cb = cb_ref[...]
    zeros_i = jnp.zeros((SUBLANES, width), jnp.int32)
    neg_c = [(0.5 * LRU_C) * _softplus(-lam_ref[d]) for d in range(2)]

    def coeffs(x_ref, base, n_rows):
        n_tiles = n_rows // tile

        def body(i, carry):
            r0 = pl.multiple_of(i * tile, tile)
            before = x_ref[pl.ds(pl.multiple_of(jnp.maximum(r0 - SUBLANES, 0), SUBLANES), SUBLANES), :]
            after = x_ref[pl.ds(pl.multiple_of(jnp.minimum(r0 + tile, n_rows - SUBLANES), SUBLANES), SUBLANES), :]
            before = jnp.where(zeros_i + i > 0, before, 0.0)
            after = jnp.where(zeros_i + i < n_tiles - 1, after, 0.0)
            ext = jnp.concatenate([before, x_ref[pl.ds(r0, tile), :], after], axis=0)
            u = jnp.broadcast_to(cb, (tile, width))
            for k in range(cw.shape[0]):
                off = k - LRU_LEFT
                sh = ext if off == 0 else pltpu.roll(ext, (-off) % ext.shape[0], axis=0)
                u = u + sh[SUBLANES:SUBLANES + tile] * cw[k:k + 1, :]
            ub = u.astype(BF16)
            hu = 0.5 * u
            for d in range(2):
                t_a = jnp.tanh(jnp.dot(ub, wa_ref[d], preferred_element_type=F32) + ba_ref[d])
                t_x = jnp.tanh(jnp.dot(ub, wx_ref[d], preferred_element_type=F32) + bx_ref[d])
                neg_log_a = neg_c[d] + neg_c[d] * t_a
                a = jnp.exp(-neg_log_a)
                dst = pl.ds(pl.multiple_of(base + r0, SUBLANES), tile)
                a_s[d, dst, :] = a
                b_s[d, dst, :] = jnp.sqrt(jnp.tanh(neg_log_a) * (1.0 + a * a)) * (hu + hu * t_x)
            return carry
        lax.fori_loop(0, n_tiles, body, 0)

    coeffs(xc_ref, 0, n_ctx)
    coeffs(xl_ref, n_ctx, n_lat)

    row = lax.broadcasted_iota(jnp.int32, (SUBLANES, width), 0)

    def scan_group(d, group, h, reverse):
        r0 = pl.multiple_of(group * SCAN_GROUP, SCAN_GROUP)
        a = a_s[d, pl.ds(r0, SCAN_GROUP), :]
        b = b_s[d, pl.ds(r0, SCAN_GROUP), :]
        order = range(SCAN_GROUP // SUBLANES)
        parts = [_scan_chunk(a[c * SUBLANES:(c + 1) * SUBLANES], b[c * SUBLANES:(c + 1) * SUBLANES], row, reverse)
                 for c in order]
        for c in (reversed(order) if reverse else order):
            hh = parts[c][0] * h + parts[c][1]
            b_s[d, pl.ds(r0 + c * SUBLANES, SUBLANES), :] = hh
            h = hh[0:1, :] if reverse else hh[SUBLANES - 1:SUBLANES, :]
        return h

    ng_ctx = n_ctx // SCAN_GROUP
    ng_all = (n_ctx + n_lat) // SCAN_GROUP
    h0 = jnp.zeros((1, width), F32)

    def ctx_body(j, hs):
        return (scan_group(0, j, hs[0], False), scan_group(1, ng_ctx - 1 - j, hs[1], True))

    def lat_body(j, hs):
        return (scan_group(0, ng_ctx + j, hs[0], False), scan_group(1, ng_all - 1 - j, hs[1], True))

    hs = lax.fori_loop(0, ng_ctx, ctx_body, (h0, h0))
    lax.fori_loop(0, ng_all - ng_ctx, lat_body, hs)

    def finish(o_ref, g_ref, base, n_rows):
        def body(i, carry):
            r0 = pl.multiple_of(i * tile, tile)
            src = pl.ds(pl.multiple_of(base + r0, SUBLANES), tile)
            hsum = b_s[0, src, :] + b_s[1, src, :]
            o_ref[pl.ds(r0, tile), :] = hsum * _gelu_tanh(g_ref[pl.ds(r0, tile), :])
            return carry
        lax.fori_loop(0, n_rows // tile, body, 0)

    finish(yc_ref, gc_ref, 0, n_ctx)
    finish(y_ref, gl_ref, n_ctx, n_lat)


def _block_diag(w):
    two, nb, bs, _ = w.shape
    eye = jnp.eye(nb, dtype=w.dtype)
    return jnp.einsum('dnij,nm->dnimj', w, eye).reshape(two, nb * bs, nb * bs)


def _rglru(xl, gl, xlc, glc, conv_w, conv_b, wa, ba, wx, bx, lam, bsz):
    n_lat, n_ctx = xl.shape[0], xlc.shape[0]
    width = LANES
    per_b = D_LRU // width
    tile = math.gcd(256, math.gcd(n_lat, n_ctx))
    wa_bd = (0.5 * _block_diag(wa)).astype(BF16)
    wx_bd = (0.5 * _block_diag(wx)).astype(BF16)
    ba = 0.5 * ba
    bx = 0.5 * bx
    col = lambda b, j: (0, b * per_b + j)
    par = lambda b, j: (0, j)
    par3 = lambda b, j: (0, 0, j)
    return pl.pallas_call(
        functools.partial(_lru_kernel, n_lat=n_lat, n_ctx=n_ctx, tile=tile),
        grid=(bsz, per_b),
        in_specs=[
            pl.BlockSpec((n_lat, width), col),
            pl.BlockSpec((n_lat, width), col),
            pl.BlockSpec((n_ctx, width), col),
            pl.BlockSpec((n_ctx, width), col),
            pl.BlockSpec((conv_w.shape[0], width), par),
            pl.BlockSpec((1, width), par),
            pl.BlockSpec((2, width, width), lambda b, j: (0, j, j)),
            pl.BlockSpec((2, 1, width), par3),
            pl.BlockSpec((2, width, width), lambda b, j: (0, j, j)),
            pl.BlockSpec((2, 1, width), par3),
            pl.BlockSpec((2, 1, width), par3),
        ],
        out_specs=[pl.BlockSpec((n_lat, width), col), pl.BlockSpec((n_ctx, width), col)],
        out_shape=[jax.ShapeDtypeStruct(xl.shape, F32), jax.ShapeDtypeStruct(xlc.shape, F32)],
        scratch_shapes=[pltpu.VMEM((2, n_ctx + n_lat, width), F32),
                        pltpu.VMEM((2, n_ctx + n_lat, width), F32)],
        compiler_params=_cparams("parallel", "parallel"),
        name="rglru",
    )(xl, gl, xlc, glc, conv_w, conv_b.reshape(1, -1), wa_bd, ba.reshape(2, 1, -1), wx_bd,
      bx.reshape(2, 1, -1), lam.reshape(2, 1, -1))


def _hy_pre_kernel(z0_ref, z1_ref, z2_ref, w0_ref, w1_ref, w2_ref, b0_ref, b1_ref, b2_ref,
                   ub_ref, u_ref, x0_ref):
    x0 = _dwconv_rows(z0_ref[...], w0_ref[...], b0_ref[...], HY_LEFT)
    x1 = _dwconv_rows(z1_ref[...], w1_ref[...], b1_ref[...], HY_LEFT)
    v = _dwconv_rows(z2_ref[...], w2_ref[...], b2_ref[...], HY_LEFT)
    u = x1 * v
    u_ref[...] = u
    ub_ref[...] = u.astype(BF16)
    x0_ref[...] = x0


def _hyena_pre(zh, conv_w, conv_b, bsz):
    rows = zh.shape[0]
    width = LANES
    per_b = D_HY // width
    zspec = lambda part: pl.BlockSpec((rows, width), lambda b, j: (0, b * 3 * per_b + part * per_b + j))
    wspec = lambda part: pl.BlockSpec((conv_w.shape[0], width), lambda b, j: (0, part * per_b + j))
    bspec = lambda part: pl.BlockSpec((1, width), lambda b, j: (0, part * per_b + j))
    ospec = pl.BlockSpec((rows, width), lambda b, j: (0, b * per_b + j))
    cb = conv_b.reshape(1, -1)
    return pl.pallas_call(
        _hy_pre_kernel,
        grid=(bsz, per_b),
        in_specs=[zspec(0), zspec(1), zspec(2), wspec(0), wspec(1), wspec(2), bspec(0), bspec(1), bspec(2)],
        out_specs=[ospec, ospec, ospec],
        out_shape=[jax.ShapeDtypeStruct((rows, bsz * D_HY), BF16),
                   jax.ShapeDtypeStruct((rows, bsz * D_HY), F32),
                   jax.ShapeDtypeStruct((rows, bsz * D_HY), F32)],
        compiler_params=_cparams("parallel", "parallel"),
        name="hyena_pre",
    )(zh, zh, zh, conv_w, conv_w, conv_w, cb, cb, cb)


def _hy_filter_kernel(z_ref, fw0_ref, fb0_ref, fwin_ref, fbin_ref, freq_ref, fwl_ref, dl_ref, o_ref, *, tile):
    hp = lax.Precision.HIGHEST
    z = z_ref[...]
    fr = freq_ref[...]
    hdn = jnp.sin(fr * (jnp.dot(z, fw0_ref[...], preferred_element_type=F32, precision=hp) + fb0_ref[...]))
    for j in range(fwin_ref.shape[0]):
        hdn = jnp.sin(fr * (jnp.dot(hdn, fwin_ref[j], preferred_element_type=F32, precision=hp) + fbin_ref[j]))
    k = jnp.dot(hdn, fwl_ref[...], preferred_element_type=F32, precision=hp)
    decay = jnp.exp(-z[:, 0:1] * dl_ref[...])
    k_fwd = k[:, :D_HY] * decay
    k_bwd = k[:, D_HY:] * decay
    row = lax.broadcasted_iota(jnp.int32, k_bwd.shape, 0) + pl.program_id(0) * tile
    k_bwd = jnp.where(row == 0, 0.0, k_bwd)
    o_ref[...] = jnp.concatenate([k_fwd, k_bwd], axis=-1).astype(BF16)


def _hyena_filter_taps(n, fw0, fb0, fw_in, fb_in, freq, fw_last):
    t = jnp.linspace(0.0, 1.0, n, dtype=F32)[:, None]
    w = 2.0 * math.pi * jnp.arange(n, dtype=F32)[:, None] / n
    f = jnp.linspace(1e-4, HY_BANDS - 1, HY_BANDS, dtype=F32)[None, :]
    z = jnp.concatenate([t, jnp.cos(f * w), -jnp.sin(f * w)], axis=-1)
    z = jnp.pad(z, ((0, 0), (0, LANES - HY_EMB)))
    fw0p = jnp.pad(fw0, ((0, LANES - HY_EMB), (0, 0)))
    max_decay = math.log(HY_TARGET) / HY_FAST
    min_decay = math.log(HY_TARGET) / HY_SLOW
    deltas = jnp.abs(jnp.linspace(min_decay, max_decay, D_HY, dtype=F32))[None, :]
    hid = fw0.shape[1]
    tile = min(512, n)
    full = lambda a: pl.BlockSpec(a.shape, lambda i: (0,) * a.ndim)
    args = [fw0p, fb0.reshape(1, hid), fw_in, fb_in.reshape(-1, 1, hid), freq.reshape(1, hid), fw_last, deltas]
    return pl.pallas_call(
        functools.partial(_hy_filter_kernel, tile=tile),
        grid=(n // tile,),
        in_specs=[pl.BlockSpec((tile, LANES), lambda i: (i, 0))] + [full(a) for a in args],
        out_specs=pl.BlockSpec((tile, 2 * D_HY), lambda i: (i, 0)),
        out_shape=jax.ShapeDtypeStruct((n, 2 * D_HY), BF16),
        compiler_params=_cparams("parallel"),
        name="hyena_filter",
    )(z, *args)


def _dft_expand_kernel(ca_ref, sa_ref, cb_ref, sb_ref, cm_ref, sm_ref, *, tile):
    ca, sa, cb, sb = ca_ref[...], sa_ref[...], cb_ref[...], sb_ref[...]
    row = lax.broadcasted_iota(jnp.int32, cb.shape, 0) + pl.program_id(0) * tile
    lane = lax.broadcasted_iota(jnp.int32, cb.shape, 1)
    alt = jnp.where(lane % 2 == 0, 1.0, -1.0)
    for s1 in range(cm_ref.shape[1] // LANES):
        c1 = ca[:, s1:s1 + 1]
        d1 = sa[:, s1:s1 + 1]
        cols = slice(s1 * LANES, (s1 + 1) * LANES)
        cm_ref[:, cols] = (c1 * cb - d1 * sb).astype(BF16)
        sm_ref[:, cols] = jnp.where(row == 0, alt, -(d1 * cb + c1 * sb)).astype(BF16)


def _dft_matrices(n):
    two_n = 2 * n
    f = jnp.arange(n, dtype=jnp.int32)[:, None]
    s_hi = jnp.arange(n // LANES, dtype=jnp.int32)[None, :] * LANES
    s_lo = jnp.arange(LANES, dtype=jnp.int32)[None, :]
    ang_a = ((f * s_hi) % two_n).astype(F32) * (2.0 * math.pi / two_n)
    ang_b = ((f * s_lo) % two_n).astype(F32) * (2.0 * math.pi / two_n)
    tile = min(DFT_TILE, n)
    hi_spec = pl.BlockSpec((tile, n // LANES), lambda i: (i, 0))
    lo_spec = pl.BlockSpec((tile, LANES), lambda i: (i, 0))
    return pl.pallas_call(
        functools.partial(_dft_expand_kernel, tile=tile),
        grid=(n // tile,),
        in_specs=[hi_spec, hi_spec, lo_spec, lo_spec],
        out_specs=[pl.BlockSpec((tile, n), lambda i: (i, 0))] * 2,
        out_shape=[jax.ShapeDtypeStruct((n, n), BF16)] * 2,
        compiler_params=_cparams("parallel"),
        name="dft_matrices",
    )(jnp.cos(ang_a), jnp.sin(ang_a), jnp.cos(ang_b), jnp.sin(ang_b))


def _dft_filter_kernel(cm_ref, sm_ref, kk_ref, kre_ref, kim_ref, *, tile):
    kk = kk_ref[...]
    xre = jnp.dot(cm_ref[...], kk, preferred_element_type=F32)
    xim = jnp.dot(sm_ref[...], kk, preferred_element_type=F32)
    row = lax.broadcasted_iota(jnp.int32, (tile, D_HY), 0) + pl.program_id(0) * tile
    kre_ref[...] = xre[:, :D_HY] + xre[:, D_HY:]
    kim_ref[...] = jnp.where(row == 0, xim[:, :D_HY] + xim[:, D_HY:], xim[:, :D_HY] - xim[:, D_HY:])


def _dft_filter(cm, sm, kk):
    n = cm.shape[0]
    tile = min(DFT_FILTER_TILE, n)
    return pl.pallas_call(
        functools.partial(_dft_filter_kernel, tile=tile),
        grid=(n // tile,),
        in_specs=[pl.BlockSpec((tile, n), lambda f: (f, 0)),
                  pl.BlockSpec((tile, n), lambda f: (f, 0)),
                  _resident(kk.shape)],
        out_specs=[pl.BlockSpec((tile, D_HY), lambda f: (f, 0))] * 2,
        out_shape=[jax.ShapeDtypeStruct((n, D_HY), F32)] * 2,
        compiler_params=_cparams("parallel"),
        name="dft_filter",
    )(cm, sm, kk)


def _dft_fwd_kernel(cm_ref, sm_ref, u_ref, kre_ref, kim_ref, yre_ref, yim_ref, *, tile):
    n_freq = cm_ref.shape[1]
    u = u_ref[...]
    xre = jnp.dot(cm_ref[...], u, preferred_element_type=F32)
    xim = jnp.dot(sm_ref[...], u, preferred_element_type=F32)
    kre = kre_ref[...]
    kim = kim_ref[...]
    row = lax.broadcasted_iota(jnp.int32, xre.shape, 0) + pl.program_id(0) * tile
    first = row == 0
    scale = jnp.where(first, 0.5 / n_freq, 1.0 / n_freq)
    yre_ref[...] = (scale * (xre * kre - jnp.where(first, 0.0, xim * kim))).astype(BF16)
    yim_ref[...] = (scale * jnp.where(first, xim * kim, xre * kim + xim * kre)).astype(BF16)


def _dft_fwd(cm, sm, ub, kre, kim):
    n = cm.shape[0]
    cols = ub.shape[1]
    tile = min(DFT_TILE, n)
    return pl.pallas_call(
        functools.partial(_dft_fwd_kernel, tile=tile),
        grid=(n // tile, cols // D_HY),
        in_specs=[pl.BlockSpec((tile, n), lambda f, c: (f, 0)),
                  pl.BlockSpec((tile, n), lambda f, c: (f, 0)),
                  pl.BlockSpec((n, D_HY), lambda f, c: (0, c)),
                  pl.BlockSpec((tile, D_HY), lambda f, c: (f, 0)),
                  pl.BlockSpec((tile, D_HY), lambda f, c: (f, 0))],
        out_specs=[pl.BlockSpec((tile, D_HY), lambda f, c: (f, c))] * 2,
        out_shape=[jax.ShapeDtypeStruct((n, cols), BF16)] * 2,
        compiler_params=_cparams("parallel", "parallel"),
        name="dft_fwd",
    )(cm, sm, ub, kre, kim)


def _dft_inv_kernel(ci_ref, si_ref, yre_ref, yim_ref, u_ref, x0_ref, skip_ref, o_ref, *, tile):
    y_cos = jnp.dot(ci_ref[...], yre_ref[...], preferred_element_type=F32)
    y_sin = jnp.dot(si_ref[...], yim_ref[...], preferred_element_type=F32)
    row = lax.broadcasted_iota(jnp.int32, y_cos.shape, 0) + pl.program_id(0) * tile
    nyq = jnp.where(row % 2 == 0, 1.0, -1.0) * yim_ref[0:1, :].astype(F32)
    y = y_cos + jnp.where(row == 0, 0.0, y_sin) + nyq
    o_ref[...] = x0_ref[...] * (y + u_ref[...] * skip_ref[...])


def _dft_inv(ci, si, yre, yim, u, x0, skip):
    n = ci.shape[0]
    cols = yre.shape[1]
    tile = min(DFT_TILE, n)
    return pl.pallas_call(
        functools.partial(_dft_inv_kernel, tile=tile),
        grid=(n // tile, cols // D_HY),
        in_specs=[pl.BlockSpec((tile, n), lambda t, c: (t, 0)),
                  pl.BlockSpec((tile, n), lambda t, c: (t, 0)),
                  pl.BlockSpec((n, D_HY), lambda t, c: (0, c)),
                  pl.BlockSpec((n, D_HY), lambda t, c: (0, c)),
                  pl.BlockSpec((tile, D_HY), lambda t, c: (t, c)),
                  pl.BlockSpec((tile, D_HY), lambda t, c: (t, c)),
                  pl.BlockSpec((1, D_HY), lambda t, c: (0, 0))],
        out_specs=pl.BlockSpec((tile, D_HY), lambda t, c: (t, c)),
        out_shape=jax.ShapeDtypeStruct((n, cols), F32),
        compiler_params=_cparams("parallel", "parallel"),
        name="dft_inv",
    )(ci, si, yre, yim, u, x0, skip.reshape(1, D_HY))


def _hyena(zh, bsz, conv_w, conv_b, filt, dft, skip):
    cm, sm = dft
    ub, u, x0 = _hyena_pre(zh, conv_w, conv_b, bsz)
    kk = _hyena_filter_taps(zh.shape[0], *filt)
    kre, kim = _dft_filter(cm, sm, kk)
    yre, yim = _dft_fwd(cm, sm, ub, kre, kim)
    return _dft_inv(cm, sm, yre, yim, u, x0, skip)


def _attend_block(q, kv, sink_ref, masks):
    k_low = [kv[:, g * LANES:(g + 1) * LANES] for g in range(N_KVH)]
    ones = jnp.ones((kv.shape[0], LANES), BF16)
    v_aug = [jnp.concatenate([kv[:, (N_KVH + g) * LANES:(N_KVH + g + 1) * LANES], ones], axis=1)
             for g in range(N_KVH)]
    nq = q.shape[0]
    windowed = masks is not None
    if windowed:
        valid_prev, valid_next = masks
    cols_per_g = D_ATTN // LANES // N_KVH
    out_cols = []
    for g in range(N_KVH):
        cols = [q[:, c * LANES:(c + 1) * LANES] for c in range(g * cols_per_g, (g + 1) * cols_per_g)]
        qg = jnp.concatenate(cols + [pltpu.roll(cq, HEAD_DIM, axis=1) for cq in cols], axis=0).astype(BF16)
        s_all = lax.dot_general(qg, k_low[g], (((1,), (1,)), ((), ())), preferred_element_type=F32)
        es, sinks = [], []
        for hb in range(2 * cols_per_g):
            h = 2 * (g * cols_per_g + hb % cols_per_g) + hb // cols_per_g
            s = s_all[hb * nq:(hb + 1) * nq]
            if windowed:
                s = jnp.concatenate([
                    jnp.where(valid_prev, s[:, :ATT_BLOCK], NEG_INF),
                    s[:, ATT_BLOCK:2 * ATT_BLOCK],
                    jnp.where(valid_next, s[:, 2 * ATT_BLOCK:3 * ATT_BLOCK], NEG_INF),
                    s[:, 3 * ATT_BLOCK:]], axis=1)
            sk = sink_ref[h:h + 1, 0:1] * LOG2E
            m = jnp.maximum(jnp.max(s, axis=-1, keepdims=True), sk)
            es.append(jnp.exp2(s - m).astype(BF16))
            sinks.append(jnp.exp2(sk - m))
        o_all = jnp.dot(jnp.concatenate(es, axis=0), v_aug[g], preferred_element_type=F32)
        outs = []
        for hb in range(2 * cols_per_g):
            o = o_all[hb * nq:(hb + 1) * nq]
            outs.append(o[:, :LANES] / (o[:, LANES:] + sinks[hb]))
        out_cols += [outs[ci] + pltpu.roll(outs[cols_per_g + ci], HEAD_DIM, axis=1) for ci in range(cols_per_g)]
    return out_cols


def _attn_kernel(*refs, windowed, n_sub):
    if windowed:
        (q_ref, kvp_ref, kvc_ref, kvn_ref, kvx_ref, sink_ref, o_ref) = refs
        i = pl.program_id(1)
        last = pl.num_programs(1) - 1
        r = lax.broadcasted_iota(jnp.int32, (ATT_BLOCK, ATT_BLOCK), 0)
        j = lax.broadcasted_iota(jnp.int32, (ATT_BLOCK, ATT_BLOCK), 1)
        cur = kvc_ref[0]
        blocks = ([kvp_ref[0]] + [cur[n * ATT_BLOCK:(n + 1) * ATT_BLOCK] for n in range(n_sub)]
                  + [kvn_ref[0]])
    else:
        (q_ref, kvx_ref, sink_ref, o_ref) = refs
    ctx = kvx_ref[0]
    for n in range(n_sub):
        rows = slice(n * ATT_BLOCK, (n + 1) * ATT_BLOCK)
        q = q_ref[0, rows, :].astype(F32)
        if windowed:
            no_prev = jnp.where(i == 0, ATT_BLOCK, 0) if n == 0 else 0
            no_next = jnp.where(i == last, ATT_BLOCK, 0) if n == n_sub - 1 else 0
            masks = (j >= r + no_prev, j <= r - no_next)
            kv = jnp.concatenate(blocks[n:n + 3] + [ctx], axis=0)
        else:
            masks = None
            kv = ctx
        for c, col in enumerate(_attend_block(q, kv, sink_ref, masks)):
            o_ref[0, rows, c * LANES:(c + 1) * LANES] = col


def _attention(q, kv, kvx, sink, windowed):
    bsz, lq, _ = q.shape
    n_ctx, kv_w = kvx.shape[1:]
    nb = lq // ATT_BLOCK
    n_sub = ATT_STEP if nb % ATT_STEP == 0 else 1
    step = n_sub * ATT_BLOCK
    sink_t = jnp.broadcast_to(sink.reshape(N_QH, 1), (N_QH, LANES))
    qspec = pl.BlockSpec((1, step, D_ATTN), lambda b, i: (b, i, 0))
    xspec = pl.BlockSpec((1, n_ctx, kv_w), lambda b, i: (b, 0, 0))
    sspec = pl.BlockSpec((N_QH, LANES), lambda b, i: (0, 0))
    if windowed:
        prev = pl.BlockSpec((1, ATT_BLOCK, kv_w), lambda b, i: (b, jnp.maximum(n_sub * i - 1, 0), 0))
        cur = pl.BlockSpec((1, step, kv_w), lambda b, i: (b, i, 0))
        nxt = pl.BlockSpec((1, ATT_BLOCK, kv_w), lambda b, i: (b, jnp.minimum(n_sub * (i + 1), nb - 1), 0))
        in_specs = [qspec, prev, cur, nxt, xspec, sspec]
        args = (q, kv, kv, kv, kvx, sink_t)
    else:
        in_specs = [qspec, xspec, sspec]
        args = (q, kvx, sink_t)
    return pl.pallas_call(
        functools.partial(_attn_kernel, windowed=windowed, n_sub=n_sub),
        grid=(bsz, lq // step),
        in_specs=in_specs,
        out_specs=pl.BlockSpec((1, step, D_ATTN), lambda b, i: (b, i, 0)),
        out_shape=jax.ShapeDtypeStruct(q.shape, F32),
        compiler_params=_cparams("parallel", "parallel"),
        name="attention",
    )(*args)


def _rope_tables(n_lat):
    rows = n_lat // GRID_W
    r = jnp.repeat(jnp.arange(rows, dtype=F32), GRID_W)
    col = jnp.tile(jnp.arange(GRID_W, dtype=F32), rows)
    inv = ROPE_THETA ** (-jnp.arange(ROPE_PAIRS_AXIS, dtype=F32) / ROPE_PAIRS_AXIS)
    ang = jnp.concatenate([r[:, None] * inv, col[:, None] * inv], axis=-1)
    cos, sin = jnp.cos(ang), jnp.sin(ang)
    zero = jnp.zeros_like(sin)
    reps = D_ATTN // HEAD_DIM
    cos_t = jnp.tile(jnp.concatenate([cos, cos], axis=-1), (1, reps))
    sin_a = jnp.tile(jnp.concatenate([-sin, zero], axis=-1), (1, reps))
    sin_b = jnp.tile(jnp.concatenate([zero, sin], axis=-1), (1, reps))
    return cos_t, sin_a, sin_b


def kernel(x, c, ctx, c_ctx, w_mod, b_mod, norm_g, ffn_w1, ffn_w2, w_in, w_out, lru_conv_w, lru_conv_b,
           lru_wa, lru_ba, lru_wx, lru_bx, lru_lam, hy_conv_w, hy_conv_b, hy_fw0, hy_fb0, hy_fw_in,
           hy_fb_in, hy_freq, hy_fw_last, hy_skip, attn_sink, final_g):
    bsz, n_lat, d = x.shape
    n_ctx = ctx.shape[1]
    depth = w_mod.shape[0]
    assert n_lat % ATT_BLOCK == 0 and n_ctx % ATT_BLOCK == 0 and n_lat % GRID_W == 0
    assert ATT_BLOCK % SCAN_GROUP == 0 and ATT_BLOCK % LANES == 0

    mod_rows = -(-(bsz + 1) // SUBLANES) * SUBLANES
    c_rows = jnp.zeros((mod_rows, d), F32).at[:bsz].set(c).at[bsz].set(c_ctx)
    mod_all = _modulation(c_rows, w_mod, b_mod).reshape(depth, mod_rows, N_MOD, d)
    lat_row = lambda b: b
    ctx_row = lambda b: bsz

    rope_tabs = _rope_tables(n_lat)
    dft_lat = _dft_matrices(n_lat)
    w1_b = ffn_w1.astype(BF16)
    w2_b = ffn_w2.astype(BF16)
    w_in_b = w_in.astype(BF16)
    w_out_b = w_out.astype(BF16)

    xc = ctx
    for l in range(depth):
        need_ctx = l < depth - 1
        mod = mod_all[l]
        filt = (hy_fw0[l], hy_fb0[l], hy_fw_in[l], hy_fb_in[l], hy_freq[l], hy_fw_last[l])

        x = _ffn(x, mod, lat_row, norm_g[l, 0], w1_b, w2_b, l, 0, 0)
        xc = _ffn(xc, mod, ctx_row, norm_g[l, 0], w1_b, w2_b, l, 0, 0)

        xl, gl, zh, q, kv = _input_proj(x, mod, lat_row, norm_g[l, 1], w_in_b, l, rope_tabs)
        xlc, glc, zhc, qc, kvc = _input_proj(xc, mod, ctx_row, norm_g[l, 1], w_in_b, l, None)

        y_lru, yc_lru = _rglru(xl, gl, xlc, glc, lru_conv_w[l], lru_conv_b[l], lru_wa[l], lru_ba[l],
                               lru_wx[l], lru_bx[l], lru_lam[l], bsz)
        y_hy = _hyena(zh, bsz, hy_conv_w[l], hy_conv_b[l], filt, dft_lat, hy_skip[l])
        y_att = _attention(q, kv, kvc, attn_sink[l], True)
        x = _ffn(x, mod, lat_row, norm_g[l, 2], w1_b, w2_b, l, 1, 6, mixer=(y_lru, y_hy, y_att, w_out_b),
                 final_g=None if need_ctx else final_g)

        if need_ctx:
            yc_hy = _hyena(zhc, bsz, hy_conv_w[l], hy_conv_b[l], filt, _dft_matrices(n_ctx), hy_skip[l])
            yc_att = _attention(qc, None, kvc, attn_sink[l], False)
            xc = _ffn(xc, mod, ctx_row, norm_g[l, 2], w1_b, w2_b, l, 1, 6,
                      mixer=(yc_lru, yc_hy, yc_att, w_out_b))
    return x
```

```python
import functools
import math

import jax
import jax.numpy as jnp
from jax import lax
from jax.experimental import pallas as pl
from jax.experimental.pallas import tpu as pltpu

F32 = jnp.float32
BF16 = jnp.bfloat16

NORM_EPS = 1e-6
N_MOD = 9
MACARON_W = 0.5
D_LRU = 256
LRU_BLOCKS = 4
LRU_C = 8.0
LRU_LEFT = 2
D_HY = 256
HY_LEFT = 1
HY_EMB = 33
HY_BANDS = (HY_EMB - 1) // 2
HY_FAST = 0.3
HY_SLOW = 1.5
HY_TARGET = 1e-2
N_QH = 8
N_KVH = 2
HEAD_DIM = 64
D_ATTN = N_QH * HEAD_DIM
D_KV = N_KVH * HEAD_DIM
WINDOW = 128
ATT_BLOCK = 128
ATT_STEP = 2
GRID_W = 64
ROPE_THETA = 10000.0
ROPE_PAIRS_AXIS = HEAD_DIM // 4
NEG_INF = -1e30
LOG2E = math.log2(math.e)

LANES = 128
SUBLANES = 8
VMEM_LIMIT_BYTES = 56 * 1024 * 1024
ROW_TILE = 512
DFT_TILE = 1024
DFT_FILTER_TILE = 512
FFT_BLOCK = 128
FFT_SLAB = 16
FFT_COLS = 512
SCAN_GROUP = 64


def _cparams(*sem):
    return pltpu.CompilerParams(dimension_semantics=sem, vmem_limit_bytes=VMEM_LIMIT_BYTES)


def _row_tile(rows):
    return min(ROW_TILE, rows)


def _resident(shape, index=None):
    index = (0,) * len(shape) if index is None else index
    return pl.BlockSpec(shape, lambda *_: index, pipeline_mode=pl.Buffered(1))


def _ada_norm(x, g, shift, scale):
    y = x * lax.rsqrt(jnp.mean(x * x, axis=-1, keepdims=True) + NORM_EPS)
    return (y * g) * (1.0 + scale) + shift


def _sigmoid(x):
    return 0.5 * (1.0 + jnp.tanh(0.5 * x))


def _gelu_tanh(x):
    return 0.5 * x * (1.0 + jnp.tanh(math.sqrt(2.0 / math.pi) * (x + 0.044715 * (x * x * x))))


def _softplus(x):
    return jnp.maximum(x, 0.0) + jnp.log1p(jnp.exp(-jnp.abs(x)))


def _dwconv_rows(x, w, bias, left):
    rows = x.shape[0]
    row = lax.broadcasted_iota(jnp.int32, x.shape, 0)
    out = jnp.broadcast_to(bias, x.shape)
    for k in range(w.shape[0]):
        off = k - left
        if off == 0:
            term = x
        else:
            shifted = pltpu.roll(x, (-off) % rows, axis=0)
            ok = (row + off >= 0) & (row + off < rows)
            term = jnp.where(ok, shifted, 0.0)
        out = out + term * w[k:k + 1, :]
    return out


def _mod_kernel(c_ref, w_ref, b_ref, o_ref):
    cv = c_ref[...]
    s = cv * _sigmoid(cv)
    o_ref[0] = jnp.dot(s, w_ref[0], preferred_element_type=F32,
                       precision=lax.Precision.HIGHEST) + b_ref[0]


def _modulation(c_rows, w_mod, b_mod):
    depth, d, nd = w_mod.shape
    rows = c_rows.shape[0]
    tn = nd // 8
    return pl.pallas_call(
        _mod_kernel,
        grid=(depth, nd // tn),
        in_specs=[
            pl.BlockSpec((rows, d), lambda l, j: (0, 0)),
            pl.BlockSpec((1, d, tn), lambda l, j: (l, 0, j)),
            pl.BlockSpec((1, 1, tn), lambda l, j: (l, 0, j)),
        ],
        out_specs=pl.BlockSpec((1, rows, tn), lambda l, j: (l, 0, j)),
        out_shape=jax.ShapeDtypeStruct((depth, rows, nd), F32),
        compiler_params=_cparams("parallel", "parallel"),
        name="modulation",
    )(c_rows, w_mod, b_mod.reshape(depth, 1, nd))


def _ffn_kernel(*refs, i_mod, d_ff, mixer, final):
    refs = list(refs)
    o_ref = refs.pop()
    x_ref, mod_ref, g_ref, w1_ref, w2_ref = refs[:5]
    rest = refs[5:]
    x = x_ref[0]
    m = mod_ref[...]
    if mixer:
        yl_ref, yh_ref, ya_ref, wo_ref = rest[:4]
        rest = rest[4:]
        y = (jnp.dot(yl_ref[...].astype(BF16), wo_ref[0:D_LRU, :], preferred_element_type=F32)
             + jnp.dot(yh_ref[...].astype(BF16), wo_ref[D_LRU:D_LRU + D_HY, :], preferred_element_type=F32)
             + jnp.dot(ya_ref[0].astype(BF16), wo_ref[D_LRU + D_HY:, :], preferred_element_type=F32))
        x = x + m[5:6] * y
    h = _ada_norm(x, g_ref[...], m[i_mod:i_mod + 1], m[i_mod + 1:i_mod + 2])
    ab = jnp.dot(h.astype(BF16), w1_ref[...], preferred_element_type=F32)
    a = ab[:, :d_ff]
    b = ab[:, d_ff:]
    gated = (a * _sigmoid(a)) * b
    y = jnp.dot(gated.astype(BF16), w2_ref[...], preferred_element_type=F32)
    x = x + (MACARON_W * m[i_mod + 2:i_mod + 3]) * y
    if final:
        x = (x * lax.rsqrt(jnp.mean(x * x, axis=-1, keepdims=True) + NORM_EPS)) * rest[0][...]
    o_ref[0] = x


def _ffn(x, mod, mod_row, g, w1, w2, layer, which, i_mod, mixer=None, final_g=None):
    bsz, rows, d = x.shape
    d_ff = w2.shape[2]
    tm = _row_tile(rows)
    in_specs = [
        pl.BlockSpec((1, tm, d), lambda b, t: (b, t, 0)),
        pl.BlockSpec((None, N_MOD, d), lambda b, t: (mod_row(b), 0, 0)),
        pl.BlockSpec((1, d), lambda b, t: (0, 0)),
        _resident((None, None, d, 2 * d_ff), (layer, which, 0, 0)),
        _resident((None, None, d_ff, d), (layer, which, 0, 0)),
    ]
    args = [x, mod, g.reshape(1, d), w1, w2]
    if mixer is not None:
        y_lru, y_hy, y_att, w_out = mixer
        in_specs += [
            pl.BlockSpec((tm, D_LRU), lambda b, t: (t, b)),
            pl.BlockSpec((tm, D_HY), lambda b, t: (t, b)),
            pl.BlockSpec((1, tm, D_ATTN), lambda b, t: (b, t, 0)),
            _resident((None,) + w_out.shape[1:], (layer, 0, 0)),
        ]
        args += [y_lru, y_hy, y_att, w_out]
    if final_g is not None:
        in_specs.append(pl.BlockSpec((1, d), lambda b, t: (0, 0)))
        args.append(final_g.reshape(1, d))
    return pl.pallas_call(
        functools.partial(_ffn_kernel, i_mod=i_mod, d_ff=d_ff, mixer=mixer is not None,
                          final=final_g is not None),
        grid=(bsz, rows // tm),
        in_specs=in_specs,
        out_specs=pl.BlockSpec((1, tm, d), lambda b, t: (b, t, 0)),
        out_shape=jax.ShapeDtypeStruct(x.shape, F32),
        compiler_params=_cparams("parallel", "parallel"),
        name="ffn",
    )(*args)


def _rope(x, cos_t, sin_a, sin_b):
    width = x.shape[-1]
    half = HEAD_DIM // 2
    up = pltpu.roll(x, width - half, axis=1)
    dn = pltpu.roll(x, half, axis=1)
    return x * cos_t + up * sin_a + dn * sin_b


def _proj_kernel(*refs, rope):
    if rope:
        (x_ref, mod_ref, g_ref, w_ref, cos_ref, sa_ref, sb_ref,
         xl_ref, gl_ref, zh_ref, q_ref, kv_ref) = refs
    else:
        (x_ref, mod_ref, g_ref, w_ref, xl_ref, gl_ref, zh_ref, q_ref, kv_ref) = refs
    x = x_ref[0]
    m = mod_ref[...]
    h = _ada_norm(x, g_ref[...], m[3:4], m[4:5])
    z = jnp.dot(h.astype(BF16), w_ref[...], preferred_element_type=F32)
    o = 0
    xl_ref[...] = z[:, o:o + D_LRU]; o += D_LRU
    gl_ref[...] = z[:, o:o + D_LRU]; o += D_LRU
    zh_ref[...] = z[:, o:o + 3 * D_HY]; o += 3 * D_HY
    q = z[:, o:o + D_ATTN]; o += D_ATTN
    k = z[:, o:o + D_KV]; o += D_KV
    v = z[:, o:o + D_KV]
    if rope:
        cos_t, sin_a, sin_b = cos_ref[...], sa_ref[...], sb_ref[...]
        q = _rope(q, cos_t, sin_a, sin_b)
        k = _rope(k, cos_t[:, :D_KV], sin_a[:, :D_KV], sin_b[:, :D_KV])
    q_ref[0] = (q * (HEAD_DIM ** -0.5 * LOG2E)).astype(BF16)
    low = lax.broadcasted_iota(jnp.int32, k.shape, 1) < HEAD_DIM
    parts = [jnp.where(low, k, 0.0), jnp.where(low, pltpu.roll(k, HEAD_DIM, axis=1), 0.0),
             jnp.where(low, v, 0.0), jnp.where(low, pltpu.roll(v, HEAD_DIM, axis=1), 0.0)]
    kv_ref[0] = jnp.concatenate(parts, axis=1).astype(BF16)


def _input_proj(x, mod, mod_row, g, w_in, layer, rope_tabs):
    bsz, rows, d = x.shape
    tm = _row_tile(rows)
    rope = rope_tabs is not None
    in_specs = [
        pl.BlockSpec((1, tm, d), lambda t, b: (b, t, 0)),
        pl.BlockSpec((None, N_MOD, d), lambda t, b: (mod_row(b), 0, 0)),
        pl.BlockSpec((1, d), lambda t, b: (0, 0)),
        _resident((None,) + w_in.shape[1:], (layer, 0, 0)),
    ]
    args = [x, mod, g.reshape(1, d), w_in]
    if rope:
        in_specs += [pl.BlockSpec((tm, D_ATTN), lambda t, b: (t, 0))] * 3
        args += list(rope_tabs)
    out_shape = [
        jax.ShapeDtypeStruct((rows, bsz * D_LRU), F32),
        jax.ShapeDtypeStruct((rows, bsz * D_LRU), F32),
        jax.ShapeDtypeStruct((rows, bsz * 3 * D_HY), F32),
        jax.ShapeDtypeStruct((bsz, rows, D_ATTN), BF16),
        jax.ShapeDtypeStruct((bsz, rows, 2 * N_KVH * LANES), BF16),
    ]
    out_specs = [
        pl.BlockSpec((tm, D_LRU), lambda t, b: (t, b)),
        pl.BlockSpec((tm, D_LRU), lambda t, b: (t, b)),
        pl.BlockSpec((tm, 3 * D_HY), lambda t, b: (t, b)),
        pl.BlockSpec((1, tm, D_ATTN), lambda t, b: (b, t, 0)),
        pl.BlockSpec((1, tm, 2 * N_KVH * LANES), lambda t, b: (b, t, 0)),
    ]
    return pl.pallas_call(
        functools.partial(_proj_kernel, rope=rope),
        grid=(rows // tm, bsz),
        in_specs=in_specs,
        out_specs=out_specs,
        out_shape=out_shape,
        compiler_params=_cparams("parallel", "parallel"),
        name="input_proj",
    )(*args)


def _scan_chunk(a, b, row, reverse):
    for s in (1, 2, 4):
        if reverse:
            ok = row < SUBLANES - s
            sh = SUBLANES - s
        else:
            ok = row >= s
            sh = s
        a_sh = pltpu.roll(a, sh, axis=0)
        b_sh = pltpu.roll(b, sh, axis=0)
        b = jnp.where(ok, a * b_sh + b, b)
        a = jnp.where(ok, a * a_sh, a)
    return a, b


def _lru_kernel(xl_ref, gl_ref, xc_ref, gc_ref, cw_ref, cb_ref, wa_ref, ba_ref, wx_ref, bx_ref,
                lam_ref, y_ref, yc_ref, a_s, b_s, *, n_lat, n_ctx, tile):
    width = xl_ref.shape[1]
    cw = cw_ref[...]
    cb = cb_ref[...]
    zeros_i = jnp.zeros((SUBLANES, width), jnp.int32)
    neg_c = [(0.5 * LRU_C) * _softplus(-lam_ref[d]) for d in range(2)]

    def coeffs(x_ref, base, n_rows):
        n_tiles = n_rows // tile

        def body(i, carry):
            r0 = pl.multiple_of(i * tile, tile)
            before = x_ref[pl.ds(pl.multiple_of(jnp.maximum(r0 - SUBLANES, 0), SUBLANES), SUBLANES), :]
            after = x_ref[pl.ds(pl.multiple_of(jnp.minimum(r0 + tile, n_rows - SUBLANES), SUBLANES), SUBLANES), :]
            before = jnp.where(zeros_i + i > 0, before, 0.0)
            after = jnp.where(zeros_i + i < n_tiles - 1, after, 0.0)
            ext = jnp.concatenate([before, x_ref[pl.ds(r0, tile), :], after], axis=0)
            u = jnp.broadcast_to(cb, (tile, width))
            for k in range(cw.shape[0]):
                off = k - LRU_LEFT
                sh = ext if off == 0 else pltpu.roll(ext, (-off) % ext.shape[0], axis=0)
                u = u + sh[SUBLANES:SUBLANES + tile] * cw[k:k + 1, :]
            ub = u.astype(BF16)
            hu = 0.5 * u
            for d in range(2):
                t_a = jnp.tanh(jnp.dot(ub, wa_ref[d], preferred_element_type=F32) + ba_ref[d])
                t_x = jnp.tanh(jnp.dot(ub, wx_ref[d], preferred_element_type=F32) + bx_ref[d])
                neg_log_a = neg_c[d] + neg_c[d] * t_a
                a = jnp.exp(-neg_log_a)
                dst = pl.ds(pl.multiple_of(base + r0, SUBLANES), tile)
                a_s[d, dst, :] = a
                b_s[d, dst, :] = jnp.sqrt(jnp.tanh(neg_log_a) * (1.0 + a * a)) * (hu + hu * t_x)
            return carry
        lax.fori_loop(0, n_tiles, body, 0)

    coeffs(xc_ref, 0, n_ctx)
    coeffs(xl_ref, n_ctx, n_lat)

    row = lax.broadcasted_iota(jnp.int32, (SUBLANES, width), 0)

    def scan_group(d, group, h, reverse):
        r0 = pl.multiple_of(group * SCAN_GROUP, SCAN_GROUP)
        a = a_s[d, pl.ds(r0, SCAN_GROUP), :]
        b = b_s[d, pl.ds(r0, SCAN_GROUP), :]
        order = range(SCAN_GROUP // SUBLANES)
        parts = [_scan_chunk(a[c * SUBLANES:(c + 1) * SUBLANES], b[c * SUBLANES:(c + 1) * SUBLANES], row, reverse)
                 for c in order]
        for c in (reversed(order) if reverse else order):
            hh = parts[c][0] * h + parts[c][1]
            b_s[d, pl.ds(r0 + c * SUBLANES, SUBLANES), :] = hh
            h = hh[0:1, :] if reverse else hh[SUBLANES - 1:SUBLANES, :]
        return h

    ng_ctx = n_ctx // SCAN_GROUP
    ng_all = (n_ctx + n_lat) // SCAN_GROUP
    h0 = jnp.zeros((1, width), F32)

    def ctx_body(j, hs):
        return (scan_group(0, j, hs[0], False), scan_group(1, ng_ctx - 1 - j, hs[1], True))

    def lat_body(j, hs):
        return (scan_group(0, ng_ctx + j, hs[0], False), scan_group(1, ng_all - 1 - j, hs[1], True))

    hs = lax.fori_loop(0, ng_ctx, ctx_body, (h0, h0))
    lax.fori_loop(0, ng_all - ng_ctx, lat_body, hs)

    def finish(o_ref, g_ref, base, n_rows):
        def body(i, carry):
            r0 = pl.multiple_of(i * tile, tile)
            src = pl.ds(pl.multiple_of(base + r0, SUBLANES), tile)
            hsum = b_s[0, src, :] + b_s[1, src, :]
            o_ref[pl.ds(r0, tile), :] = hsum * _gelu_tanh(g_ref[pl.ds(r0, tile), :])
            return carry
        lax.fori_loop(0, n_rows // tile, body, 0)

    finish(yc_ref, gc_ref, 0, n_ctx)
    finish(y_ref, gl_ref, n_ctx, n_lat)


def _block_diag(w):
    two, nb, bs, _ = w.shape
    eye = jnp.eye(nb, dtype=w.dtype)
    return jnp.einsum('dnij,nm->dnimj', w, eye).reshape(two, nb * bs, nb * bs)


def _rglru(xl, gl, xlc, glc, conv_w, conv_b, wa, ba, wx, bx, lam, bsz):
    n_lat, n_ctx = xl.shape[0], xlc.shape[0]
    width = LANES
    per_b = D_LRU // width
    tile = math.gcd(256, math.gcd(n_lat, n_ctx))
    wa_bd = (0.5 * _block_diag(wa)).astype(BF16)
    wx_bd = (0.5 * _block_diag(wx)).astype(BF16)
    ba = 0.5 * ba
    bx = 0.5 * bx
    col = lambda b, j: (0, b * per_b + j)
    par = lambda b, j: (0, j)
    par3 = lambda b, j: (0, 0, j)
    return pl.pallas_call(
        functools.partial(_lru_kernel, n_lat=n_lat, n_ctx=n_ctx, tile=tile),
        grid=(bsz, per_b),
        in_specs=[
            pl.BlockSpec((n_lat, width), col),
            pl.BlockSpec((n_lat, width), col),
            pl.BlockSpec((n_ctx, width), col),
            pl.BlockSpec((n_ctx, width), col),
            pl.BlockSpec((conv_w.shape[0], width), par),
            pl.BlockSpec((1, width), par),
            pl.BlockSpec((2, width, width), lambda b, j: (0, j, j)),
            pl.BlockSpec((2, 1, width), par3),
            pl.BlockSpec((2, width, width), lambda b, j: (0, j, j)),
            pl.BlockSpec((2, 1, width), par3),
            pl.BlockSpec((2, 1, width), par3),
        ],
        out_specs=[pl.BlockSpec((n_lat, width), col), pl.BlockSpec((n_ctx, width), col)],
        out_shape=[jax.ShapeDtypeStruct(xl.shape, F32), jax.ShapeDtypeStruct(xlc.shape, F32)],
        scratch_shapes=[pltpu.VMEM((2, n_ctx + n_lat, width), F32),
                        pltpu.VMEM((2, n_ctx + n_lat, width), F32)],
        compiler_params=_cparams("parallel", "parallel"),
        name="rglru",
    )(xl, gl, xlc, glc, conv_w, conv_b.reshape(1, -1), wa_bd, ba.reshape(2, 1, -1), wx_bd,
      bx.reshape(2, 1, -1), lam.reshape(2, 1, -1))


def _hy_pre_kernel(z0_ref, z1_ref, z2_ref, w0_ref, w1_ref, w2_ref, b0_ref, b1_ref, b2_ref,
                   ub_ref, u_ref, x0_ref):
    x0 = _dwconv_rows(z0_ref[...], w0_ref[...], b0_ref[...], HY_LEFT)
    x1 = _dwconv_rows(z1_ref[...], w1_ref[...], b1_ref[...], HY_LEFT)
    v = _dwconv_rows(z2_ref[...], w2_ref[...], b2_ref[...], HY_LEFT)
    u = x1 * v
    u_ref[...] = u
    ub_ref[...] = u.astype(BF16)
    x0_ref[...] = x0


def _hyena_pre(zh, conv_w, conv_b, bsz):
    rows = zh.shape[0]
    width = LANES
    per_b = D_HY // width
    zspec = lambda part: pl.BlockSpec((rows, width), lambda b, j: (0, b * 3 * per_b + part * per_b + j))
    wspec = lambda part: pl.BlockSpec((conv_w.shape[0], width), lambda b, j: (0, part * per_b + j))
    bspec = lambda part: pl.BlockSpec((1, width), lambda b, j: (0, part * per_b + j))
    ospec = pl.BlockSpec((rows, width), lambda b, j: (0, b * per_b + j))
    cb = conv_b.reshape(1, -1)
    return pl.pallas_call(
        _hy_pre_kernel,
        grid=(bsz, per_b),
        in_specs=[zspec(0), zspec(1), zspec(2), wspec(0), wspec(1), wspec(2), bspec(0), bspec(1), bspec(2)],
        out_specs=[ospec, ospec, ospec],
        out_shape=[jax.ShapeDtypeStruct((rows, bsz * D_HY), BF16),
                   jax.ShapeDtypeStruct((rows, bsz * D_HY), F32),
                   jax.ShapeDtypeStruct((rows, bsz * D_HY), F32)],
        compiler_params=_cparams("parallel", "parallel"),
        name="hyena_pre",
    )(zh, zh, zh, conv_w, conv_w, conv_w, cb, cb, cb)


def _hy_filter_kernel(z_ref, fw0_ref, fb0_ref, fwin_ref, fbin_ref, freq_ref, fwl_ref, dl_ref, o_ref, *, tile):
    hp = lax.Precision.HIGHEST
    z = z_ref[...]
    fr = freq_ref[...]
    hdn = jnp.sin(fr * (jnp.dot(z, fw0_ref[...], preferred_element_type=F32, precision=hp) + fb0_ref[...]))
    for j in range(fwin_ref.shape[0]):
        hdn = jnp.sin(fr * (jnp.dot(hdn, fwin_ref[j], preferred_element_type=F32, precision=hp) + fbin_ref[j]))
    k = jnp.dot(hdn, fwl_ref[...], preferred_element_type=F32, precision=hp)
    decay = jnp.exp(-z[:, 0:1] * dl_ref[...])
    k_fwd = k[:, :D_HY] * decay
    k_bwd = k[:, D_HY:] * decay
    row = lax.broadcasted_iota(jnp.int32, k_bwd.shape, 0) + pl.program_id(0) * tile
    k_bwd = jnp.where(row == 0, 0.0, k_bwd)
    o_ref[...] = jnp.concatenate([k_fwd, k_bwd], axis=-1).astype(BF16)


def _hyena_filter_taps(n, fw0, fb0, fw_in, fb_in, freq, fw_last):
    t = jnp.linspace(0.0, 1.0, n, dtype=F32)[:, None]
    w = 2.0 * math.pi * jnp.arange(n, dtype=F32)[:, None] / n
    f = jnp.linspace(1e-4, HY_BANDS - 1, HY_BANDS, dtype=F32)[None, :]
    z = jnp.concatenate([t, jnp.cos(f * w), -jnp.sin(f * w)], axis=-1)
    z = jnp.pad(z, ((0, 0), (0, LANES - HY_EMB)))
    fw0p = jnp.pad(fw0, ((0, LANES - HY_EMB), (0, 0)))
    max_decay = math.log(HY_TARGET) / HY_FAST
    min_decay = math.log(HY_TARGET) / HY_SLOW
    deltas = jnp.abs(jnp.linspace(min_decay, max_decay, D_HY, dtype=F32))[None, :]
    hid = fw0.shape[1]
    tile = min(512, n)
    full = lambda a: pl.BlockSpec(a.shape, lambda i: (0,) * a.ndim)
    args = [fw0p, fb0.reshape(1, hid), fw_in, fb_in.reshape(-1, 1, hid), freq.reshape(1, hid), fw_last, deltas]
    return pl.pallas_call(
        functools.partial(_hy_filter_kernel, tile=tile),
        grid=(n // tile,),
        in_specs=[pl.BlockSpec((tile, LANES), lambda i: (i, 0))] + [full(a) for a in args],
        out_specs=pl.BlockSpec((tile, 2 * D_HY), lambda i: (i, 0)),
        out_shape=jax.ShapeDtypeStruct((n, 2 * D_HY), BF16),
        compiler_params=_cparams("parallel"),
        name="hyena_filter",
    )(z, *args)


def _dft_expand_kernel(ca_ref, sa_ref, cb_ref, sb_ref, cm_ref, sm_ref, *, tile):
    ca, sa, cb, sb = ca_ref[...], sa_ref[...], cb_ref[...], sb_ref[...]
    row = lax.broadcasted_iota(jnp.int32, cb.shape, 0) + pl.program_id(0) * tile
    lane = lax.broadcasted_iota(jnp.int32, cb.shape, 1)
    alt = jnp.where(lane % 2 == 0, 1.0, -1.0)
    for s1 in range(cm_ref.shape[1] // LANES):
        c1 = ca[:, s1:s1 + 1]
        d1 = sa[:, s1:s1 + 1]
        cols = slice(s1 * LANES, (s1 + 1) * LANES)
        cm_ref[:, cols] = (c1 * cb - d1 * sb).astype(BF16)
        sm_ref[:, cols] = jnp.where(row == 0, alt, -(d1 * cb + c1 * sb)).astype(BF16)


def _dft_matrices(n):
    two_n = 2 * n
    f = jnp.arange(n, dtype=jnp.int32)[:, None]
    s_hi = jnp.arange(n // LANES, dtype=jnp.int32)[None, :] * LANES
    s_lo = jnp.arange(LANES, dtype=jnp.int32)[None, :]
    ang_a = ((f * s_hi) % two_n).astype(F32) * (2.0 * math.pi / two_n)
    ang_b = ((f * s_lo) % two_n).astype(F32) * (2.0 * math.pi / two_n)
    tile = min(DFT_TILE, n)
    hi_spec = pl.BlockSpec((tile, n // LANES), lambda i: (i, 0))
    lo_spec = pl.BlockSpec((tile, LANES), lambda i: (i, 0))
    return pl.pallas_call(
        functools.partial(_dft_expand_kernel, tile=tile),
        grid=(n // tile,),
        in_specs=[hi_spec, hi_spec, lo_spec, lo_spec],
        out_specs=[pl.BlockSpec((tile, n), lambda i: (i, 0))] * 2,
        out_shape=[jax.ShapeDtypeStruct((n, n), BF16)] * 2,
        compiler_params=_cparams("parallel"),
        name="dft_matrices",
    )(jnp.cos(ang_a), jnp.sin(ang_a), jnp.cos(ang_b), jnp.sin(ang_b))


def _dft_filter_kernel(cm_ref, sm_ref, kk_ref, kre_ref, kim_ref, *, tile):
    kk = kk_ref[...]
    xre = jnp.dot(cm_ref[...], kk, preferred_element_type=F32)
    xim = jnp.dot(sm_ref[...], kk, preferred_element_type=F32)
    row = lax.broadcasted_iota(jnp.int32, (tile, D_HY), 0) + pl.program_id(0) * tile
    kre_ref[...] = xre[:, :D_HY] + xre[:, D_HY:]
    kim_ref[...] = jnp.where(row == 0, xim[:, :D_HY] + xim[:, D_HY:], xim[:, :D_HY] - xim[:, D_HY:])


def _dft_filter(cm, sm, kk):
    n = cm.shape[0]
    tile = min(DFT_FILTER_TILE, n)
    return pl.pallas_call(
        functools.partial(_dft_filter_kernel, tile=tile),
        grid=(n // tile,),
        in_specs=[pl.BlockSpec((tile, n), lambda f: (f, 0)),
                  pl.BlockSpec((tile, n), lambda f: (f, 0)),
                  _resident(kk.shape)],
        out_specs=[pl.BlockSpec((tile, D_HY), lambda f: (f, 0))] * 2,
        out_shape=[jax.ShapeDtypeStruct((n, D_HY), F32)] * 2,
        compiler_params=_cparams("parallel"),
        name="dft_filter",
    )(cm, sm, kk)


def _dft_fwd_kernel(cm_ref, sm_ref, u_ref, kre_ref, kim_ref, yre_ref, yim_ref, *, tile):
    n_freq = cm_ref.shape[1]
    u = u_ref[...]
    xre = jnp.dot(cm_ref[...], u, preferred_element_type=F32)
    xim = jnp.dot(sm_ref[...], u, preferred_element_type=F32)
    kre = kre_ref[...]
    kim = kim_ref[...]
    row = lax.broadcasted_iota(jnp.int32, xre.shape, 0) + pl.program_id(0) * tile
    first = row == 0
    scale = jnp.where(first, 0.5 / n_freq, 1.0 / n_freq)
    yre_ref[...] = (scale * (xre * kre - jnp.where(first, 0.0, xim * kim))).astype(BF16)
    yim_ref[...] = (scale * jnp.where(first, xim * kim, xre * kim + xim * kre)).astype(BF16)


def _dft_fwd(cm, sm, ub, kre, kim):
    n = cm.shape[0]
    cols = ub.shape[1]
    tile = min(DFT_TILE, n)
    return pl.pallas_call(
        functools.partial(_dft_fwd_kernel, tile=tile),
        grid=(n // tile, cols // D_HY),
        in_specs=[pl.BlockSpec((tile, n), lambda f, c: (f, 0)),
                  pl.BlockSpec((tile, n), lambda f, c: (f, 0)),
                  pl.BlockSpec((n, D_HY), lambda f, c: (0, c)),
                  pl.BlockSpec((tile, D_HY), lambda f, c: (f, 0)),
                  pl.BlockSpec((tile, D_HY), lambda f, c: (f, 0))],
        out_specs=[pl.BlockSpec((tile, D_HY), lambda f, c: (f, c))] * 2,
        out_shape=[jax.ShapeDtypeStruct((n, cols), BF16)] * 2,
        compiler_params=_cparams("parallel", "parallel"),
        name="dft_fwd",
    )(cm, sm, ub, kre, kim)


def _dft_inv_kernel(ci_ref, si_ref, yre_ref, yim_ref, u_ref, x0_ref, skip_ref, o_ref, *, tile):
    y_cos = jnp.dot(ci_ref[...], yre_ref[...], preferred_element_type=F32)
    y_sin = jnp.dot(si_ref[...], yim_ref[...], preferred_element_type=F32)
    row = lax.broadcasted_iota(jnp.int32, y_cos.shape, 0) + pl.program_id(0) * tile
    nyq = jnp.where(row % 2 == 0, 1.0, -1.0) * yim_ref[0:1, :].astype(F32)
    y = y_cos + jnp.where(row == 0, 0.0, y_sin) + nyq
    o_ref[...] = x0_ref[...] * (y + u_ref[...] * skip_ref[...])


def _dft_inv(ci, si, yre, yim, u, x0, skip):
    n = ci.shape[0]
    cols = yre.shape[1]
    tile = min(DFT_TILE, n)
    return pl.pallas_call(
        functools.partial(_dft_inv_kernel, tile=tile),
        grid=(n // tile, cols // D_HY),
        in_specs=[pl.BlockSpec((tile, n), lambda t, c: (t, 0)),
                  pl.BlockSpec((tile, n), lambda t, c: (t, 0)),
                  pl.BlockSpec((n, D_HY), lambda t, c: (0, c)),
                  pl.BlockSpec((n, D_HY), lambda t, c: (0, c)),
                  pl.BlockSpec((tile, D_HY), lambda t, c: (t, c)),
                  pl.BlockSpec((tile, D_HY), lambda t, c: (t, c)),
                  pl.BlockSpec((1, D_HY), lambda t, c: (0, 0))],
        out_specs=pl.BlockSpec((tile, D_HY), lambda t, c: (t, c)),
        out_shape=jax.ShapeDtypeStruct((n, cols), F32),
        compiler_params=_cparams("parallel", "parallel"),
        name="dft_inv",
    )(ci, si, yre, yim, u, x0, skip.reshape(1, D_HY))


def _fft_tables(n):
    n1 = 2 * n // FFT_BLOCK
    half = n1 // 2
    nf = half + 1
    groups = FFT_BLOCK // FFT_SLAB
    f1 = jnp.arange(nf, dtype=F32)[:, None]
    s1 = jnp.arange(half, dtype=F32)[None, :]
    ang1 = (2.0 * math.pi / n1) * f1 * s1
    eye = jnp.eye(FFT_SLAB, dtype=F32)
    m1 = jnp.concatenate([jnp.kron(jnp.cos(ang1), eye), jnp.kron(-jnp.sin(ang1), eye)], axis=0)
    m1_inv = jnp.concatenate([jnp.kron(jnp.cos(ang1).T, eye), jnp.kron(-jnp.sin(ang1).T, eye)], axis=1)
    k = jnp.arange(FFT_BLOCK, dtype=F32)
    ang2 = (2.0 * math.pi / FFT_BLOCK) * k[:, None] * k[None, :]
    c2, d2 = jnp.cos(ang2), jnp.sin(ang2)
    g2 = jnp.block([[c2, d2], [-d2, c2]])
    g2_inv = jnp.block([[c2, -d2], [d2, c2]])
    psi = (2.0 * math.pi / (2 * n)) * f1 * k[None, :]
    tw = jnp.stack([jnp.cos(psi), jnp.sin(psi)])
    tw_s2 = jnp.broadcast_to(tw[..., None], (2, nf, FFT_BLOCK, LANES))
    tw_s1 = tw.reshape(2, nf, groups, FFT_SLAB).transpose(2, 0, 1, 3).reshape(groups, 2, nf * FFT_SLAB)
    tw_s1 = jnp.broadcast_to(tw_s1[..., None], (groups, 2, nf * FFT_SLAB, LANES))
    return dict(n=n, half=half, nf=nf, m1=m1.astype(BF16), m1_inv=m1_inv.astype(BF16), g2=g2.astype(BF16),
                g2_inv=g2_inv.astype(BF16), tw_s1=tw_s1, tw_s2=tw_s2)


def _lane_tile(x, width):
    return jnp.concatenate([x] * (width // x.shape[-1]), axis=-1) if width != x.shape[-1] else x


def _fft_s1_kernel(u_ref, m_ref, tw_ref, o_ref):
    h, slab, tc = u_ref.shape
    r = jnp.dot(m_ref[...], u_ref[...].reshape(h * slab, tc), preferred_element_type=F32)
    rows = r.shape[0] // 2
    re, im = r[:rows], r[rows:]
    cs = _lane_tile(tw_ref[0], tc)
    sn = _lane_tile(tw_ref[1], tc)
    o_ref[0] = (re * cs + im * sn).astype(BF16).reshape(rows // slab, slab, tc)
    o_ref[1] = (im * cs - re * sn).astype(BF16).reshape(rows // slab, slab, tc)


def _fft_s1(ub, tabs):
    n, cols = ub.shape
    half, nf = tabs["half"], tabs["nf"]
    groups = FFT_BLOCK // FFT_SLAB
    tc = min(FFT_COLS, cols)
    out = pl.pallas_call(
        _fft_s1_kernel,
        grid=(groups, cols // tc),
        in_specs=[pl.BlockSpec((half, None, FFT_SLAB, tc), lambda m, c: (0, m, 0, c)),
                  _resident(tabs["m1"].shape),
                  pl.BlockSpec((None, 2, nf * FFT_SLAB, LANES), lambda m, c: (m, 0, 0, 0))],
        out_specs=pl.BlockSpec((2, nf, None, FFT_SLAB, tc), lambda m, c: (0, 0, m, 0, c)),
        out_shape=jax.ShapeDtypeStruct((2, nf, groups, FFT_SLAB, cols), BF16),
        compiler_params=_cparams("parallel", "parallel"),
        name="fft_stage1",
    )(ub.reshape(half, groups, FFT_SLAB, cols), tabs["m1"], tabs["tw_s1"])
    return out.reshape(2, nf, FFT_BLOCK, cols)


def _fft_s2_kernel(a_ref, g_ref, k_ref, y_ref, *, gf, half, scale):
    g2 = g_ref[...]
    for j in range(gf):
        x = jnp.dot(g2, jnp.concatenate([a_ref[0, j], a_ref[1, j]], axis=0), preferred_element_type=F32)
        xre, xim = x[:FFT_BLOCK], x[FFT_BLOCK:]
        kre, kim = k_ref[0, j], k_ref[1, j]
        f1 = pl.program_id(0) * gf + j
        w = jnp.where(jnp.logical_or(f1 == 0, f1 == half), scale, 2.0 * scale)
        y_ref[0, j] = (w * (xre * kre - xim * kim)).astype(BF16)
        y_ref[1, j] = (w * (xre * kim + xim * kre)).astype(BF16)


def _fft_group(nf):
    return max(g for g in range(1, 12) if nf % g == 0)


def _fft_s2(a, tabs, spec):
    _, nf, _, cols = a.shape
    gf = _fft_group(nf)
    blk = lambda w, idx: pl.BlockSpec((2, gf, FFT_BLOCK, w), idx)
    return pl.pallas_call(
        functools.partial(_fft_s2_kernel, gf=gf, half=tabs["half"], scale=0.5 / tabs["n"]),
        grid=(nf // gf, cols // D_HY),
        in_specs=[blk(D_HY, lambda g, c: (0, g, 0, c)), _resident(tabs["g2"].shape),
                  blk(D_HY, lambda g, c: (0, g, 0, 0))],
        out_specs=blk(D_HY, lambda g, c: (0, g, 0, c)),
        out_shape=jax.ShapeDtypeStruct(a.shape, BF16),
        compiler_params=_cparams("parallel", "parallel"),
        name="fft_stage2",
    )(a, tabs["g2"], spec)


def _fft_s2_filter_kernel(a_ref, g_ref, k_ref, *, gf):
    g2 = g_ref[...]
    for j in range(gf):
        x = jnp.dot(g2, jnp.concatenate([a_ref[0, j], a_ref[1, j]], axis=0), preferred_element_type=F32)
        xre, xim = x[:FFT_BLOCK], x[FFT_BLOCK:]
        k_ref[0, j] = xre[:, :D_HY] + xre[:, D_HY:]
        k_ref[1, j] = xim[:, :D_HY] - xim[:, D_HY:]


def _fft_s2_filter(a, tabs):
    _, nf, _, cols = a.shape
    gf = _fft_group(nf)
    return pl.pallas_call(
        functools.partial(_fft_s2_filter_kernel, gf=gf),
        grid=(nf // gf,),
        in_specs=[pl.BlockSpec((2, gf, FFT_BLOCK, cols), lambda g: (0, g, 0, 0)), _resident(tabs["g2"].shape)],
        out_specs=pl.BlockSpec((2, gf, FFT_BLOCK, D_HY), lambda g: (0, g, 0, 0)),
        out_shape=jax.ShapeDtypeStruct((2, nf, FFT_BLOCK, D_HY), F32),
        compiler_params=_cparams("parallel"),
        name="fft_stage2_filter",
    )(a, tabs["g2"])


def _fft_s2_inv_kernel(y_ref, g_ref, tw_ref, b_ref, *, gf):
    g2 = g_ref[...]
    tc = y_ref.shape[-1]
    for j in range(gf):
        b = jnp.dot(g2, jnp.concatenate([y_ref[0, j], y_ref[1, j]], axis=0), preferred_element_type=F32)
        bre, bim = b[:FFT_BLOCK], b[FFT_BLOCK:]
        cs = _lane_tile(tw_ref[0, j], tc)
        sn = _lane_tile(tw_ref[1, j], tc)
        b_ref[0, j] = (bre * cs - bim * sn).astype(BF16)
        b_ref[1, j] = (bre * sn + bim * cs).astype(BF16)


def _fft_s2_inv(y, tabs):
    _, nf, _, cols = y.shape
    gf = _fft_group(nf)
    blk = lambda w, idx: pl.BlockSpec((2, gf, FFT_BLOCK, w), idx)
    return pl.pallas_call(
        functools.partial(_fft_s2_inv_kernel, gf=gf),
        grid=(nf // gf, cols // D_HY),
        in_specs=[blk(D_HY, lambda g, c: (0, g, 0, c)), _resident(tabs["g2_inv"].shape),
                  blk(LANES, lambda g, c: (0, g, 0, 0))],
        out_specs=blk(D_HY, lambda g, c: (0, g, 0, c)),
        out_shape=jax.ShapeDtypeStruct(y.shape, BF16),
        compiler_params=_cparams("parallel", "parallel"),
        name="fft_stage2_inv",
    )(y, tabs["g2_inv"], tabs["tw_s2"])


def _fft_s1_inv_kernel(b_ref, m_ref, u_ref, x0_ref, skip_ref, o_ref):
    two, nf, slab, tc = b_ref.shape
    h = u_ref.shape[0]
    y = jnp.dot(m_ref[...], b_ref[...].reshape(two * nf * slab, tc), preferred_element_type=F32)
    u = u_ref[...].reshape(h * slab, tc)
    x0 = x0_ref[...].reshape(h * slab, tc)
    o_ref[...] = (x0 * (y + u * _lane_tile(skip_ref[...], tc))).reshape(h, slab, tc)


def _fft_s1_inv(b, tabs, u, x0, skip):
    _, nf, _, cols = b.shape
    n, half = tabs["n"], tabs["half"]
    groups = FFT_BLOCK // FFT_SLAB
    tc = min(FFT_COLS, cols)
    rows4 = lambda a: a.reshape(half, groups, FFT_SLAB, cols)
    tspec = pl.BlockSpec((half, None, FFT_SLAB, tc), lambda m, c: (0, m, 0, c))
    out = pl.pallas_call(
        _fft_s1_inv_kernel,
        grid=(groups, cols // tc),
        in_specs=[pl.BlockSpec((2, nf, None, FFT_SLAB, tc), lambda m, c: (0, 0, m, 0, c)),
                  _resident(tabs["m1_inv"].shape), tspec, tspec,
                  pl.BlockSpec((1, D_HY), lambda m, c: (0, 0))],
        out_specs=tspec,
        out_shape=jax.ShapeDtypeStruct((half, groups, FFT_SLAB, cols), F32),
        compiler_params=_cparams("parallel", "parallel"),
        name="fft_stage1_inv",
    )(b.reshape(2, nf, groups, FFT_SLAB, cols), tabs["m1_inv"], rows4(u), rows4(x0), skip.reshape(1, D_HY))
    return out.reshape(n, cols)


def _hyena_fft(zh, bsz, conv_w, conv_b, filt, tabs, skip):
    ub, u, x0 = _hyena_pre(zh, conv_w, conv_b, bsz)
    kk = _hyena_filter_taps(zh.shape[0], *filt)
    spec = _fft_s2_filter(_fft_s1(kk, tabs), tabs)
    y = _fft_s2(_fft_s1(ub, tabs), tabs, spec)
    return _fft_s1_inv(_fft_s2_inv(y, tabs), tabs, u, x0, skip)


def _hyena(zh, bsz, conv_w, conv_b, filt, dft, skip):
    cm, sm = dft
    ub, u, x0 = _hyena_pre(zh, conv_w, conv_b, bsz)
    kk = _hyena_filter_taps(zh.shape[0], *filt)
    kre, kim = _dft_filter(cm, sm, kk)
    yre, yim = _dft_fwd(cm, sm, ub, kre, kim)
    return _dft_inv(cm, sm, yre, yim, u, x0, skip)


def _attend_block(q, kv, sink_ref, masks):
    k_low = [kv[:, g * LANES:(g + 1) * LANES] for g in range(N_KVH)]
    ones = jnp.ones((kv.shape[0], LANES), BF16)
    v_aug = [jnp.concatenate([kv[:, (N_KVH + g) * LANES:(N_KVH + g + 1) * LANES], ones], axis=1)
             for g in range(N_KVH)]
    nq = q.shape[0]
    windowed = masks is not None
    if windowed:
        valid_prev, valid_next = masks
    cols_per_g = D_ATTN // LANES // N_KVH
    out_cols = []
    for g in range(N_KVH):
        cols = [q[:, c * LANES:(c + 1) * LANES] for c in range(g * cols_per_g, (g + 1) * cols_per_g)]
        qg = jnp.concatenate(cols + [pltpu.roll(cq, HEAD_DIM, axis=1) for cq in cols], axis=0).astype(BF16)
        s_all = lax.dot_general(qg, k_low[g], (((1,), (1,)), ((), ())), preferred_element_type=F32)
        es, sinks = [], []
        for hb in range(2 * cols_per_g):
            h = 2 * (g * cols_per_g + hb % cols_per_g) + hb // cols_per_g
            s = s_all[hb * nq:(hb + 1) * nq]
            if windowed:
                s = jnp.concatenate([
                    jnp.where(valid_prev, s[:, :ATT_BLOCK], NEG_INF),
                    s[:, ATT_BLOCK:2 * ATT_BLOCK],
                    jnp.where(valid_next, s[:, 2 * ATT_BLOCK:3 * ATT_BLOCK], NEG_INF),
                    s[:, 3 * ATT_BLOCK:]], axis=1)
            sk = sink_ref[h:h + 1, 0:1] * LOG2E
            m = jnp.maximum(jnp.max(s, axis=-1, keepdims=True), sk)
            es.append(jnp.exp2(s - m).astype(BF16))
            sinks.append(jnp.exp2(sk - m))
        o_all = jnp.dot(jnp.concatenate(es, axis=0), v_aug[g], preferred_element_type=F32)
        outs = []
        for hb in range(2 * cols_per_g):
            o = o_all[hb * nq:(hb + 1) * nq]
            outs.append(o[:, :LANES] / (o[:, LANES:] + sinks[hb]))
        out_cols += [outs[ci] + pltpu.roll(outs[cols_per_g + ci], HEAD_DIM, axis=1) for ci in range(cols_per_g)]
    return out_cols


def _attn_kernel(*refs, windowed, n_sub):
    if windowed:
        (q_ref, kvp_ref, kvc_ref, kvn_ref, kvx_ref, sink_ref, o_ref) = refs
        i = pl.program_id(1)
        last = pl.num_programs(1) - 1
        r = lax.broadcasted_iota(jnp.int32, (ATT_BLOCK, ATT_BLOCK), 0)
        j = lax.broadcasted_iota(jnp.int32, (ATT_BLOCK, ATT_BLOCK), 1)
        cur = kvc_ref[0]
        blocks = ([kvp_ref[0]] + [cur[n * ATT_BLOCK:(n + 1) * ATT_BLOCK] for n in range(n_sub)]
                  + [kvn_ref[0]])
    else:
        (q_ref, kvx_ref, sink_ref, o_ref) = refs
    ctx = kvx_ref[0]
    for n in range(n_sub):
        rows = slice(n * ATT_BLOCK, (n + 1) * ATT_BLOCK)
        q = q_ref[0, rows, :].astype(F32)
        if windowed:
            no_prev = jnp.where(i == 0, ATT_BLOCK, 0) if n == 0 else 0
            no_next = jnp.where(i == last, ATT_BLOCK, 0) if n == n_sub - 1 else 0
            masks = (j >= r + no_prev, j <= r - no_next)
            kv = jnp.concatenate(blocks[n:n + 3] + [ctx], axis=0)
        else:
            masks = None
            kv = ctx
        for c, col in enumerate(_attend_block(q, kv, sink_ref, masks)):
            o_ref[0, rows, c * LANES:(c + 1) * LANES] = col


def _attention(q, kv, kvx, sink, windowed):
    bsz, lq, _ = q.shape
    n_ctx, kv_w = kvx.shape[1:]
    nb = lq // ATT_BLOCK
    n_sub = ATT_STEP if nb % ATT_STEP == 0 else 1
    step = n_sub * ATT_BLOCK
    sink_t = jnp.broadcast_to(sink.reshape(N_QH, 1), (N_QH, LANES))
    qspec = pl.BlockSpec((1, step, D_ATTN), lambda b, i: (b, i, 0))
    xspec = pl.BlockSpec((1, n_ctx, kv_w), lambda b, i: (b, 0, 0))
    sspec = pl.BlockSpec((N_QH, LANES), lambda b, i: (0, 0))
    if windowed:
        prev = pl.BlockSpec((1, ATT_BLOCK, kv_w), lambda b, i: (b, jnp.maximum(n_sub * i - 1, 0), 0))
        cur = pl.BlockSpec((1, step, kv_w), lambda b, i: (b, i, 0))
        nxt = pl.BlockSpec((1, ATT_BLOCK, kv_w), lambda b, i: (b, jnp.minimum(n_sub * (i + 1), nb - 1), 0))
        in_specs = [qspec, prev, cur, nxt, xspec, sspec]
        args = (q, kv, kv, kv, kvx, sink_t)
    else:
        in_specs = [qspec, xspec, sspec]
        args = (q, kvx, sink_t)
    return pl.pallas_call(
        functools.partial(_attn_kernel, windowed=windowed, n_sub=n_sub),
        grid=(bsz, lq // step),
        in_specs=in_specs,
        out_specs=pl.BlockSpec((1, step, D_ATTN), lambda b, i: (b, i, 0)),
        out_shape=jax.ShapeDtypeStruct(q.shape, F32),
        compiler_params=_cparams("parallel", "parallel"),
        name="attention",
    )(*args)


def _rope_tables(n_lat):
    rows = n_lat // GRID_W
    r = jnp.repeat(jnp.arange(rows, dtype=F32), GRID_W)
    col = jnp.tile(jnp.arange(GRID_W, dtype=F32), rows)
    inv = ROPE_THETA ** (-jnp.arange(ROPE_PAIRS_AXIS, dtype=F32) / ROPE_PAIRS_AXIS)
    ang = jnp.concatenate([r[:, None] * inv, col[:, None] * inv], axis=-1)
    cos, sin = jnp.cos(ang), jnp.sin(ang)
    zero = jnp.zeros_like(sin)
    reps = D_ATTN // HEAD_DIM
    cos_t = jnp.tile(jnp.concatenate([cos, cos], axis=-1), (1, reps))
    sin_a = jnp.tile(jnp.concatenate([-sin, zero], axis=-1), (1, reps))
    sin_b = jnp.tile(jnp.concatenate([zero, sin], axis=-1), (1, reps))
    return cos_t, sin_a, sin_b


def kernel(x, c, ctx, c_ctx, w_mod, b_mod, norm_g, ffn_w1, ffn_w2, w_in, w_out, lru_conv_w, lru_conv_b,
           lru_wa, lru_ba, lru_wx, lru_bx, lru_lam, hy_conv_w, hy_conv_b, hy_fw0, hy_fb0, hy_fw_in,
           hy_fb_in, hy_freq, hy_fw_last, hy_skip, attn_sink, final_g):
    bsz, n_lat, d = x.shape
    n_ctx = ctx.shape[1]
    depth = w_mod.shape[0]
    assert n_lat % ATT_BLOCK == 0 and n_ctx % ATT_BLOCK == 0 and n_lat % GRID_W == 0
    assert ATT_BLOCK % SCAN_GROUP == 0 and ATT_BLOCK % LANES == 0

    mod_rows = -(-(bsz + 1) // SUBLANES) * SUBLANES
    c_rows = jnp.zeros((mod_rows, d), F32).at[:bsz].set(c).at[bsz].set(c_ctx)
    mod_all = _modulation(c_rows, w_mod, b_mod).reshape(depth, mod_rows, N_MOD, d)
    lat_row = lambda b: b
    ctx_row = lambda b: bsz

    rope_tabs = _rope_tables(n_lat)
    fft_lat = _fft_tables(n_lat)
    w1_b = ffn_w1.astype(BF16)
    w2_b = ffn_w2.astype(BF16)
    w_in_b = w_in.astype(BF16)
    w_out_b = w_out.astype(BF16)

    xc = ctx
    for l in range(depth):
        need_ctx = l < depth - 1
        mod = mod_all[l]
        filt = (hy_fw0[l], hy_fb0[l], hy_fw_in[l], hy_fb_in[l], hy_freq[l], hy_fw_last[l])

        x = _ffn(x, mod, lat_row, norm_g[l, 0], w1_b, w2_b, l, 0, 0)
        xc = _ffn(xc, mod, ctx_row, norm_g[l, 0], w1_b, w2_b, l, 0, 0)

        xl, gl, zh, q, kv = _input_proj(x, mod, lat_row, norm_g[l, 1], w_in_b, l, rope_tabs)
        xlc, glc, zhc, qc, kvc = _input_proj(xc, mod, ctx_row, norm_g[l, 1], w_in_b, l, None)

        y_lru, yc_lru = _rglru(xl, gl, xlc, glc, lru_conv_w[l], lru_conv_b[l], lru_wa[l], lru_ba[l],
                               lru_wx[l], lru_bx[l], lru_lam[l], bsz)
        y_hy = _hyena_fft(zh, bsz, hy_conv_w[l], hy_conv_b[l], filt, fft_lat, hy_skip[l])
        y_att = _attention(q, kv, kvc, attn_sink[l], True)
        x = _ffn(x, mod, lat_row, norm_g[l, 2], w1_b, w2_b, l, 1, 6, mixer=(y_lru, y_hy, y_att, w_out_b),
                 final_g=None if need_ctx else final_g)

        if need_ctx:
            yc_hy = _hyena(zhc, bsz, hy_conv_w[l], hy_conv_b[l], filt, _dft_matrices(n_ctx), hy_skip[l])
            yc_att = _attention(qc, None, kvc, attn_sink[l], False)
            xc = _ffn(xc, mod, ctx_row, norm_g[l, 2], w1_b, w2_b, l, 1, 6,
                      mixer=(yc_lru, yc_hy, yc_att, w_out_b))
    return x
```

```python
import functools
import math

import jax
import jax.numpy as jnp
from jax import lax
from jax.experimental import pallas as pl
from jax.experimental.pallas import tpu as pltpu

F32 = jnp.float32
BF16 = jnp.bfloat16

NORM_EPS = 1e-6
N_MOD = 9
MACARON_W = 0.5
D_LRU = 256
LRU_BLOCKS = 4
LRU_C = 8.0
LRU_LEFT = 2
D_HY = 256
HY_LEFT = 1
HY_EMB = 33
HY_BANDS = (HY_EMB - 1) // 2
HY_FAST = 0.3
HY_SLOW = 1.5
HY_TARGET = 1e-2
N_QH = 8
N_KVH = 2
HEAD_DIM = 64
D_ATTN = N_QH * HEAD_DIM
D_KV = N_KVH * HEAD_DIM
WINDOW = 128
ATT_BLOCK = 128
ATT_STEP = 2
GRID_W = 64
ROPE_THETA = 10000.0
ROPE_PAIRS_AXIS = HEAD_DIM // 4
NEG_INF = -1e30
LOG2E = math.log2(math.e)

LANES = 128
SUBLANES = 8
VMEM_LIMIT_BYTES = 56 * 1024 * 1024
ROW_TILE = 512
DFT_TILE = 1024
DFT_FILTER_TILE = 512
FFT_BLOCK = 128
FFT_SLAB = 16
FFT_COLS = 512
SCAN_GROUP = 64


def _cparams(*sem):
    return pltpu.CompilerParams(dimension_semantics=sem, vmem_limit_bytes=VMEM_LIMIT_BYTES)


def _row_tile(rows):
    return min(ROW_TILE, rows)


def _resident(shape, index=None):
    index = (0,) * len(shape) if index is None else index
    return pl.BlockSpec(shape, lambda *_: index, pipeline_mode=pl.Buffered(1))


def _ada_norm(x, g, shift, scale):
    y = x * lax.rsqrt(jnp.mean(x * x, axis=-1, keepdims=True) + NORM_EPS)
    return (y * g) * (1.0 + scale) + shift


def _sigmoid(x):
    return 0.5 * (1.0 + jnp.tanh(0.5 * x))


def _gelu_tanh(x):
    return 0.5 * x * (1.0 + jnp.tanh(math.sqrt(2.0 / math.pi) * (x + 0.044715 * (x * x * x))))


def _softplus(x):
    return jnp.maximum(x, 0.0) + jnp.log1p(jnp.exp(-jnp.abs(x)))


def _dwconv_tile(x_ref, w, bias, left, i, tile):
    n_rows, width = x_ref.shape
    n_tiles = n_rows // tile
    r0 = pl.multiple_of(i * tile, tile)
    zeros_i = jnp.zeros((SUBLANES, width), jnp.int32)
    before = x_ref[pl.ds(pl.multiple_of(jnp.maximum(r0 - SUBLANES, 0), SUBLANES), SUBLANES), :]
    after = x_ref[pl.ds(pl.multiple_of(jnp.minimum(r0 + tile, n_rows - SUBLANES), SUBLANES), SUBLANES), :]
    before = jnp.where(zeros_i + i > 0, before, 0.0)
    after = jnp.where(zeros_i + i < n_tiles - 1, after, 0.0)
    ext = jnp.concatenate([before, x_ref[pl.ds(r0, tile), :], after], axis=0)
    out = jnp.broadcast_to(bias, (tile, width))
    for k in range(w.shape[0]):
        off = k - left
        sh = ext if off == 0 else pltpu.roll(ext, (-off) % ext.shape[0], axis=0)
        out = out + sh[SUBLANES:SUBLANES + tile] * w[k:k + 1, :]
    return out


def _mod_kernel(c_ref, w_ref, b_ref, o_ref):
    cv = c_ref[...]
    s = cv * _sigmoid(cv)
    rows = s.shape[0]
    s_hi = s.astype(BF16)
    s_lo = (s - s_hi.astype(F32)).astype(BF16)
    w = w_ref[0]
    w_hi = w.astype(BF16)
    w_lo = (w - w_hi.astype(F32)).astype(BF16)
    p = jnp.dot(jnp.concatenate([s_hi, s_lo], axis=0), w_hi, preferred_element_type=F32)
    o_ref[0] = p[:rows] + p[rows:] + jnp.dot(s_hi, w_lo, preferred_element_type=F32) + b_ref[0]


def _modulation(c_rows, w_mod, b_mod):
    depth, d, nd = w_mod.shape
    rows = c_rows.shape[0]
    tn = nd // 8
    return pl.pallas_call(
        _mod_kernel,
        grid=(depth, nd // tn),
        in_specs=[
            pl.BlockSpec((rows, d), lambda l, j: (0, 0)),
            pl.BlockSpec((1, d, tn), lambda l, j: (l, 0, j)),
            pl.BlockSpec((1, 1, tn), lambda l, j: (l, 0, j)),
        ],
        out_specs=pl.BlockSpec((1, rows, tn), lambda l, j: (l, 0, j)),
        out_shape=jax.ShapeDtypeStruct((depth, rows, nd), F32),
        compiler_params=_cparams("parallel", "parallel"),
        name="modulation",
    )(c_rows, w_mod, b_mod.reshape(depth, 1, nd))


def _ffn_kernel(*refs, i_mod, d_ff, mixer, final):
    refs = list(refs)
    o_ref = refs.pop()
    x_ref, mod_ref, g_ref, w1_ref, w2_ref = refs[:5]
    rest = refs[5:]
    x = x_ref[0]
    m = mod_ref[...]
    if mixer:
        yl_ref, yh_ref, ya_ref, wo_ref = rest[:4]
        rest = rest[4:]
        y = (jnp.dot(yl_ref[...].astype(BF16), wo_ref[0:D_LRU, :], preferred_element_type=F32)
             + jnp.dot(yh_ref[...].astype(BF16), wo_ref[D_LRU:D_LRU + D_HY, :], preferred_element_type=F32)
             + jnp.dot(ya_ref[0].astype(BF16), wo_ref[D_LRU + D_HY:, :], preferred_element_type=F32))
        x = x + m[5:6] * y
    h = _ada_norm(x, g_ref[...], m[i_mod:i_mod + 1], m[i_mod + 1:i_mod + 2])
    ab = jnp.dot(h.astype(BF16), w1_ref[...], preferred_element_type=F32)
    a = ab[:, :d_ff]
    b = ab[:, d_ff:]
    gated = (a * _sigmoid(a)) * b
    y = jnp.dot(gated.astype(BF16), w2_ref[...], preferred_element_type=F32)
    x = x + (MACARON_W * m[i_mod + 2:i_mod + 3]) * y
    if final:
        x = (x * lax.rsqrt(jnp.mean(x * x, axis=-1, keepdims=True) + NORM_EPS)) * rest[0][...]
    o_ref[0] = x


def _ffn(x, mod, mod_row, g, w1, w2, layer, which, i_mod, mixer=None, final_g=None):
    bsz, rows, d = x.shape
    d_ff = w2.shape[2]
    tm = _row_tile(rows)
    in_specs = [
        pl.BlockSpec((1, tm, d), lambda b, t: (b, t, 0)),
        pl.BlockSpec((None, N_MOD, d), lambda b, t: (mod_row(b), 0, 0)),
        pl.BlockSpec((1, d), lambda b, t: (0, 0)),
        _resident((None, None, d, 2 * d_ff), (layer, which, 0, 0)),
        _resident((None, None, d_ff, d), (layer, which, 0, 0)),
    ]
    args = [x, mod, g.reshape(1, d), w1, w2]
    if mixer is not None:
        y_lru, y_hy, y_att, w_out = mixer
        in_specs += [
            pl.BlockSpec((tm, D_LRU), lambda b, t: (t, b)),
            pl.BlockSpec((tm, D_HY), lambda b, t: (t, b)),
            pl.BlockSpec((1, tm, D_ATTN), lambda b, t: (b, t, 0)),
            _resident((None,) + w_out.shape[1:], (layer, 0, 0)),
        ]
        args += [y_lru, y_hy, y_att, w_out]
    if final_g is not None:
        in_specs.append(pl.BlockSpec((1, d), lambda b, t: (0, 0)))
        args.append(final_g.reshape(1, d))
    return pl.pallas_call(
        functools.partial(_ffn_kernel, i_mod=i_mod, d_ff=d_ff, mixer=mixer is not None,
                          final=final_g is not None),
        grid=(bsz, rows // tm),
        in_specs=in_specs,
        out_specs=pl.BlockSpec((1, tm, d), lambda b, t: (b, t, 0)),
        out_shape=jax.ShapeDtypeStruct(x.shape, F32),
        compiler_params=_cparams("parallel", "parallel"),
        name="ffn",
    )(*args)


def _rope(x, cos_t, sin_a, sin_b):
    width = x.shape[-1]
    half = HEAD_DIM // 2
    up = pltpu.roll(x, width - half, axis=1)
    dn = pltpu.roll(x, half, axis=1)
    return x * cos_t + up * sin_a + dn * sin_b


def _proj_kernel(*refs, rope):
    if rope:
        (x_ref, mod_ref, g_ref, w_ref, cos_ref, sa_ref, sb_ref,
         xl_ref, gl_ref, zh_ref, q_ref, kv_ref) = refs
    else:
        (x_ref, mod_ref, g_ref, w_ref, xl_ref, gl_ref, zh_ref, q_ref, kv_ref) = refs
    m = mod_ref[...]
    g = g_ref[...]
    tm = x_ref.shape[1]
    n_sub = 2 if tm % (4 * SUBLANES) == 0 else 1
    for n in range(n_sub):
        rows = slice(n * tm // n_sub, (n + 1) * tm // n_sub)
        h = _ada_norm(x_ref[0, rows, :], g, m[3:4], m[4:5])
        z = jnp.dot(h.astype(BF16), w_ref[...], preferred_element_type=F32)
        o = 0
        xl_ref[rows, :] = z[:, o:o + D_LRU]; o += D_LRU
        gl_ref[rows, :] = z[:, o:o + D_LRU]; o += D_LRU
        zh_ref[rows, :] = z[:, o:o + 3 * D_HY]; o += 3 * D_HY
        q = z[:, o:o + D_ATTN]; o += D_ATTN
        k = z[:, o:o + D_KV]; o += D_KV
        v = z[:, o:o + D_KV]
        if rope:
            cos_t, sin_a, sin_b = cos_ref[rows, :], sa_ref[rows, :], sb_ref[rows, :]
            q = _rope(q, cos_t, sin_a, sin_b)
            k = _rope(k, cos_t[:, :D_KV], sin_a[:, :D_KV], sin_b[:, :D_KV])
        q_ref[0, rows, :] = (q * (HEAD_DIM ** -0.5 * LOG2E)).astype(BF16)
        low = lax.broadcasted_iota(jnp.int32, k.shape, 1) < HEAD_DIM
        parts = [jnp.where(low, k, 0.0), jnp.where(low, pltpu.roll(k, HEAD_DIM, axis=1), 0.0),
                 jnp.where(low, v, 0.0), jnp.where(low, pltpu.roll(v, HEAD_DIM, axis=1), 0.0)]
        kv_ref[0, rows, :] = jnp.concatenate(parts, axis=1).astype(BF16)


def _input_proj(x, mod, mod_row, g, w_in, layer, rope_tabs):
    bsz, rows, d = x.shape
    tm = _row_tile(rows)
    rope = rope_tabs is not None
    in_specs = [
        pl.BlockSpec((1, tm, d), lambda t, b: (b, t, 0)),
        pl.BlockSpec((None, N_MOD, d), lambda t, b: (mod_row(b), 0, 0)),
        pl.BlockSpec((1, d), lambda t, b: (0, 0)),
        _resident((None,) + w_in.shape[1:], (layer, 0, 0)),
    ]
    args = [x, mod, g.reshape(1, d), w_in]
    if rope:
        in_specs += [pl.BlockSpec((tm, D_ATTN), lambda t, b: (t, 0))] * 3
        args += list(rope_tabs)
    out_shape = [
        jax.ShapeDtypeStruct((rows, bsz * D_LRU), F32),
        jax.ShapeDtypeStruct((rows, bsz * D_LRU), F32),
        jax.ShapeDtypeStruct((rows, bsz * 3 * D_HY), F32),
        jax.ShapeDtypeStruct((bsz, rows, D_ATTN), BF16),
        jax.ShapeDtypeStruct((bsz, rows, 2 * N_KVH * LANES), BF16),
    ]
    out_specs = [
        pl.BlockSpec((tm, D_LRU), lambda t, b: (t, b)),
        pl.BlockSpec((tm, D_LRU), lambda t, b: (t, b)),
        pl.BlockSpec((tm, 3 * D_HY), lambda t, b: (t, b)),
        pl.BlockSpec((1, tm, D_ATTN), lambda t, b: (b, t, 0)),
        pl.BlockSpec((1, tm, 2 * N_KVH * LANES), lambda t, b: (b, t, 0)),
    ]
    return pl.pallas_call(
        functools.partial(_proj_kernel, rope=rope),
        grid=(rows // tm, bsz),
        in_specs=in_specs,
        out_specs=out_specs,
        out_shape=out_shape,
        compiler_params=_cparams("parallel", "parallel"),
        name="input_proj",
    )(*args)


def _scan_chunk(a, b, row, reverse):
    for s in (1, 2, 4):
        if reverse:
            ok = row < SUBLANES - s
            sh = SUBLANES - s
        else:
            ok = row >= s
            sh = s
        a_sh = pltpu.roll(a, sh, axis=0)
        b_sh = pltpu.roll(b, sh, axis=0)
        b = jnp.where(ok, a * b_sh + b, b)
        a = jnp.where(ok, a * a_sh, a)
    return a, b


def _lru_kernel(xl_ref, gl_ref, xc_ref, gc_ref, cw_ref, cb_ref, wa_ref, ba_ref, wx_ref, bx_ref,
                lam_ref, y_ref, yc_ref, a_s, b_s, *, n_lat, n_ctx, tile):
    width = xl_ref.shape[1]
    cw = cw_ref[...]
    cb = cb_ref[...]
    neg_c = [(0.5 * LRU_C) * _softplus(-lam_ref[d]) for d in range(2)]

    def coeffs(x_ref, base, n_rows):
        def body(i, carry):
            r0 = pl.multiple_of(i * tile, tile)
            u = _dwconv_tile(x_ref, cw, cb, LRU_LEFT, i, tile)
            ub = u.astype(BF16)
            hu = 0.5 * u
            for d in range(2):
                t_a = jnp.tanh(jnp.dot(ub, wa_ref[d], preferred_element_type=F32) + ba_ref[d])
                t_x = jnp.tanh(jnp.dot(ub, wx_ref[d], preferred_element_type=F32) + bx_ref[d])
                neg_log_a = neg_c[d] + neg_c[d] * t_a
                a = jnp.exp(-neg_log_a)
                dst = pl.ds(pl.multiple_of(base + r0, SUBLANES), tile)
                a_s[d, dst, :] = a
                b_s[d, dst, :] = jnp.sqrt(jnp.tanh(neg_log_a) * (1.0 + a * a)) * (hu + hu * t_x)
            return carry
        lax.fori_loop(0, n_rows // tile, body, 0)

    coeffs(xc_ref, 0, n_ctx)
    coeffs(xl_ref, n_ctx, n_lat)

    row = lax.broadcasted_iota(jnp.int32, (SUBLANES, width), 0)

    def scan_group(d, group, h, reverse):
        r0 = pl.multiple_of(group * SCAN_GROUP, SCAN_GROUP)
        a = a_s[d, pl.ds(r0, SCAN_GROUP), :]
        b = b_s[d, pl.ds(r0, SCAN_GROUP), :]
        order = range(SCAN_GROUP // SUBLANES)
        parts = [_scan_chunk(a[c * SUBLANES:(c + 1) * SUBLANES], b[c * SUBLANES:(c + 1) * SUBLANES], row, reverse)
                 for c in order]
        for c in (reversed(order) if reverse else order):
            hh = parts[c][0] * h + parts[c][1]
            b_s[d, pl.ds(r0 + c * SUBLANES, SUBLANES), :] = hh
            h = hh[0:1, :] if reverse else hh[SUBLANES - 1:SUBLANES, :]
        return h

    ng_ctx = n_ctx // SCAN_GROUP
    ng_all = (n_ctx + n_lat) // SCAN_GROUP
    h0 = jnp.zeros((1, width), F32)

    def ctx_body(j, hs):
        return (scan_group(0, j, hs[0], False), scan_group(1, ng_ctx - 1 - j, hs[1], True))

    def lat_body(j, hs):
        return (scan_group(0, ng_ctx + j, hs[0], False), scan_group(1, ng_all - 1 - j, hs[1], True))

    hs = lax.fori_loop(0, ng_ctx, ctx_body, (h0, h0))
    lax.fori_loop(0, ng_all - ng_ctx, lat_body, hs)

    def finish(o_ref, g_ref, base, n_rows):
        def body(i, carry):
            r0 = pl.multiple_of(i * tile, tile)
            src = pl.ds(pl.multiple_of(base + r0, SUBLANES), tile)
            hsum = b_s[0, src, :] + b_s[1, src, :]
            o_ref[pl.ds(r0, tile), :] = hsum * _gelu_tanh(g_ref[pl.ds(r0, tile), :])
            return carry
        lax.fori_loop(0, n_rows // tile, body, 0)

    finish(yc_ref, gc_ref, 0, n_ctx)
    finish(y_ref, gl_ref, n_ctx, n_lat)


def _block_diag(w):
    two, nb, bs, _ = w.shape
    eye = jnp.eye(nb, dtype=w.dtype)
    return jnp.einsum('dnij,nm->dnimj', w, eye).reshape(two, nb * bs, nb * bs)


def _rglru(xl, gl, xlc, glc, conv_w, conv_b, wa, ba, wx, bx, lam, bsz):
    n_lat, n_ctx = xl.shape[0], xlc.shape[0]
    width = LANES
    per_b = D_LRU // width
    tile = math.gcd(256, math.gcd(n_lat, n_ctx))
    wa_bd = (0.5 * _block_diag(wa)).astype(BF16)
    wx_bd = (0.5 * _block_diag(wx)).astype(BF16)
    ba = 0.5 * ba
    bx = 0.5 * bx
    col = lambda b, j: (0, b * per_b + j)
    par = lambda b, j: (0, j)
    par3 = lambda b, j: (0, 0, j)
    return pl.pallas_call(
        functools.partial(_lru_kernel, n_lat=n_lat, n_ctx=n_ctx, tile=tile),
        grid=(bsz, per_b),
        in_specs=[
            pl.BlockSpec((n_lat, width), col),
            pl.BlockSpec((n_lat, width), col),
            pl.BlockSpec((n_ctx, width), col),
            pl.BlockSpec((n_ctx, width), col),
            pl.BlockSpec((conv_w.shape[0], width), par),
            pl.BlockSpec((1, width), par),
            pl.BlockSpec((2, width, width), lambda b, j: (0, j, j)),
            pl.BlockSpec((2, 1, width), par3),
            pl.BlockSpec((2, width, width), lambda b, j: (0, j, j)),
            pl.BlockSpec((2, 1, width), par3),
            pl.BlockSpec((2, 1, width), par3),
        ],
        out_specs=[pl.BlockSpec((n_lat, width), col), pl.BlockSpec((n_ctx, width), col)],
        out_shape=[jax.ShapeDtypeStruct(xl.shape, F32), jax.ShapeDtypeStruct(xlc.shape, F32)],
        scratch_shapes=[pltpu.VMEM((2, n_ctx + n_lat, width), F32),
                        pltpu.VMEM((2, n_ctx + n_lat, width), F32)],
        compiler_params=_cparams("parallel", "parallel"),
        name="rglru",
    )(xl, gl, xlc, glc, conv_w, conv_b.reshape(1, -1), wa_bd, ba.reshape(2, 1, -1), wx_bd,
      bx.reshape(2, 1, -1), lam.reshape(2, 1, -1))


def _hy_pre_kernel(z0_ref, z1_ref, z2_ref, w0_ref, w1_ref, w2_ref, b0_ref, b1_ref, b2_ref,
                   ub_ref, x0_ref, *, tile):
    w0, w1, w2 = w0_ref[...], w1_ref[...], w2_ref[...]
    b0, b1, b2 = b0_ref[...], b1_ref[...], b2_ref[...]

    def body(i, carry):
        rows = pl.ds(pl.multiple_of(i * tile, tile), tile)
        x1 = _dwconv_tile(z1_ref, w1, b1, HY_LEFT, i, tile)
        v = _dwconv_tile(z2_ref, w2, b2, HY_LEFT, i, tile)
        ub_ref[rows, :] = (x1 * v).astype(BF16)
        x0_ref[rows, :] = _dwconv_tile(z0_ref, w0, b0, HY_LEFT, i, tile)
        return carry
    lax.fori_loop(0, z0_ref.shape[0] // tile, body, 0)


def _hyena_pre(zh, conv_w, conv_b, bsz):
    rows = zh.shape[0]
    width = LANES
    per_b = D_HY // width
    tile = math.gcd(256, rows)
    zspec = lambda part: pl.BlockSpec((rows, width), lambda b, j: (0, b * 3 * per_b + part * per_b + j))
    wspec = lambda part: pl.BlockSpec((conv_w.shape[0], width), lambda b, j: (0, part * per_b + j))
    bspec = lambda part: pl.BlockSpec((1, width), lambda b, j: (0, part * per_b + j))
    ospec = pl.BlockSpec((rows, width), lambda b, j: (0, b * per_b + j))
    cb = conv_b.reshape(1, -1)
    return pl.pallas_call(
        functools.partial(_hy_pre_kernel, tile=tile),
        grid=(bsz, per_b),
        in_specs=[zspec(0), zspec(1), zspec(2), wspec(0), wspec(1), wspec(2), bspec(0), bspec(1), bspec(2)],
        out_specs=[ospec, ospec],
        out_shape=[jax.ShapeDtypeStruct((rows, bsz * D_HY), BF16),
                   jax.ShapeDtypeStruct((rows, bsz * D_HY), F32)],
        compiler_params=_cparams("parallel", "parallel"),
        name="hyena_pre",
    )(zh, zh, zh, conv_w, conv_w, conv_w, cb, cb, cb)


def _hy_filter_kernel(z_ref, fw0_ref, fb0_ref, fwin_ref, fbin_ref, freq_ref, fwl_ref, dl_ref, o_ref, *, tile):
    hp = lax.Precision.HIGHEST
    z = z_ref[...]
    fr = freq_ref[...]
    hdn = jnp.sin(fr * (jnp.dot(z, fw0_ref[...], preferred_element_type=F32, precision=hp) + fb0_ref[...]))
    for j in range(fwin_ref.shape[0]):
        hdn = jnp.sin(fr * (jnp.dot(hdn, fwin_ref[j], preferred_element_type=F32, precision=hp) + fbin_ref[j]))
    k = jnp.dot(hdn, fwl_ref[...], preferred_element_type=F32, precision=hp)
    decay = jnp.exp(-z[:, 0:1] * dl_ref[...])
    k_fwd = k[:, :D_HY] * decay
    k_bwd = k[:, D_HY:] * decay
    row = lax.broadcasted_iota(jnp.int32, k_bwd.shape, 0) + pl.program_id(0) * tile
    k_bwd = jnp.where(row == 0, 0.0, k_bwd)
    o_ref[...] = jnp.concatenate([k_fwd, k_bwd], axis=-1).astype(BF16)


def _hyena_filter_taps(n, fw0, fb0, fw_in, fb_in, freq, fw_last):
    t = jnp.linspace(0.0, 1.0, n, dtype=F32)[:, None]
    w = 2.0 * math.pi * jnp.arange(n, dtype=F32)[:, None] / n
    f = jnp.linspace(1e-4, HY_BANDS - 1, HY_BANDS, dtype=F32)[None, :]
    z = jnp.concatenate([t, jnp.cos(f * w), -jnp.sin(f * w)], axis=-1)
    z = jnp.pad(z, ((0, 0), (0, LANES - HY_EMB)))
    fw0p = jnp.pad(fw0, ((0, LANES - HY_EMB), (0, 0)))
    max_decay = math.log(HY_TARGET) / HY_FAST
    min_decay = math.log(HY_TARGET) / HY_SLOW
    deltas = jnp.abs(jnp.linspace(min_decay, max_decay, D_HY, dtype=F32))[None, :]
    hid = fw0.shape[1]
    tile = min(512, n)
    full = lambda a: pl.BlockSpec(a.shape, lambda i: (0,) * a.ndim)
    args = [fw0p, fb0.reshape(1, hid), fw_in, fb_in.reshape(-1, 1, hid), freq.reshape(1, hid), fw_last, deltas]
    return pl.pallas_call(
        functools.partial(_hy_filter_kernel, tile=tile),
        grid=(n // tile,),
        in_specs=[pl.BlockSpec((tile, LANES), lambda i: (i, 0))] + [full(a) for a in args],
        out_specs=pl.BlockSpec((tile, 2 * D_HY), lambda i: (i, 0)),
        out_shape=jax.ShapeDtypeStruct((n, 2 * D_HY), BF16),
        compiler_params=_cparams("parallel"),
        name="hyena_filter",
    )(z, *args)


def _dft_expand_kernel(ca_ref, sa_ref, cb_ref, sb_ref, cm_ref, sm_ref, *, tile):
    ca, sa, cb, sb = ca_ref[...], sa_ref[...], cb_ref[...], sb_ref[...]
    row = lax.broadcasted_iota(jnp.int32, cb.shape, 0) + pl.program_id(0) * tile
    lane = lax.broadcasted_iota(jnp.int32, cb.shape, 1)
    alt = jnp.where(lane % 2 == 0, 1.0, -1.0)
    for s1 in range(cm_ref.shape[1] // LANES):
        c1 = ca[:, s1:s1 + 1]
        d1 = sa[:, s1:s1 + 1]
        cols = slice(s1 * LANES, (s1 + 1) * LANES)
        cm_ref[:, cols] = (c1 * cb - d1 * sb).astype(BF16)
        sm_ref[:, cols] = jnp.where(row == 0, alt, -(d1 * cb + c1 * sb)).astype(BF16)


def _dft_matrices(n):
    two_n = 2 * n
    f = jnp.arange(n, dtype=jnp.int32)[:, None]
    s_hi = jnp.arange(n // LANES, dtype=jnp.int32)[None, :] * LANES
    s_lo = jnp.arange(LANES, dtype=jnp.int32)[None, :]
    ang_a = ((f * s_hi) % two_n).astype(F32) * (2.0 * math.pi / two_n)
    ang_b = ((f * s_lo) % two_n).astype(F32) * (2.0 * math.pi / two_n)
    tile = min(DFT_TILE, n)
    hi_spec = pl.BlockSpec((tile, n // LANES), lambda i: (i, 0))
    lo_spec = pl.BlockSpec((tile, LANES), lambda i: (i, 0))
    return pl.pallas_call(
        functools.partial(_dft_expand_kernel, tile=tile),
        grid=(n // tile,),
        in_specs=[hi_spec, hi_spec, lo_spec, lo_spec],
        out_specs=[pl.BlockSpec((tile, n), lambda i: (i, 0))] * 2,
        out_shape=[jax.ShapeDtypeStruct((n, n), BF16)] * 2,
        compiler_params=_cparams("parallel"),
        name="dft_matrices",
    )(jnp.cos(ang_a), jnp.sin(ang_a), jnp.cos(ang_b), jnp.sin(ang_b))


def _dft_filter_kernel(cm_ref, sm_ref, kk_ref, kre_ref, kim_ref, *, tile):
    kk = kk_ref[...]
    xre = jnp.dot(cm_ref[...], kk, preferred_element_type=F32)
    xim = jnp.dot(sm_ref[...], kk, preferred_element_type=F32)
    row = lax.broadcasted_iota(jnp.int32, (tile, D_HY), 0) + pl.program_id(0) * tile
    kre_ref[...] = xre[:, :D_HY] + xre[:, D_HY:]
    kim_ref[...] = jnp.where(row == 0, xim[:, :D_HY] + xim[:, D_HY:], xim[:, :D_HY] - xim[:, D_HY:])


def _dft_filter(cm, sm, kk):
    n = cm.shape[0]
    tile = min(DFT_FILTER_TILE, n)
    return pl.pallas_call(
        functools.partial(_dft_filter_kernel, tile=tile),
        grid=(n // tile,),
        in_specs=[pl.BlockSpec((tile, n), lambda f: (f, 0)),
                  pl.BlockSpec((tile, n), lambda f: (f, 0)),
                  _resident(kk.shape)],
        out_specs=[pl.BlockSpec((tile, D_HY), lambda f: (f, 0))] * 2,
        out_shape=[jax.ShapeDtypeStruct((n, D_HY), F32)] * 2,
        compiler_params=_cparams("parallel"),
        name="dft_filter",
    )(cm, sm, kk)


def _dft_fwd_kernel(cm_ref, sm_ref, u_ref, kre_ref, kim_ref, yre_ref, yim_ref, *, tile):
    n_freq = cm_ref.shape[1]
    u = u_ref[...]
    xre = jnp.dot(cm_ref[...], u, preferred_element_type=F32)
    xim = jnp.dot(sm_ref[...], u, preferred_element_type=F32)
    kre = kre_ref[...]
    kim = kim_ref[...]
    row = lax.broadcasted_iota(jnp.int32, xre.shape, 0) + pl.program_id(0) * tile
    first = row == 0
    scale = jnp.where(first, 0.5 / n_freq, 1.0 / n_freq)
    yre_ref[...] = (scale * (xre * kre - jnp.where(first, 0.0, xim * kim))).astype(BF16)
    yim_ref[...] = (scale * jnp.where(first, xim * kim, xre * kim + xim * kre)).astype(BF16)


def _dft_fwd(cm, sm, ub, kre, kim):
    n = cm.shape[0]
    cols = ub.shape[1]
    tile = min(DFT_TILE, n)
    return pl.pallas_call(
        functools.partial(_dft_fwd_kernel, tile=tile),
        grid=(n // tile, cols // D_HY),
        in_specs=[pl.BlockSpec((tile, n), lambda f, c: (f, 0)),
                  pl.BlockSpec((tile, n), lambda f, c: (f, 0)),
                  pl.BlockSpec((n, D_HY), lambda f, c: (0, c)),
                  pl.BlockSpec((tile, D_HY), lambda f, c: (f, 0)),
                  pl.BlockSpec((tile, D_HY), lambda f, c: (f, 0))],
        out_specs=[pl.BlockSpec((tile, D_HY), lambda f, c: (f, c))] * 2,
        out_shape=[jax.ShapeDtypeStruct((n, cols), BF16)] * 2,
        compiler_params=_cparams("parallel", "parallel"),
        name="dft_fwd",
    )(cm, sm, ub, kre, kim)


def _dft_inv_kernel(ci_ref, si_ref, yre_ref, yim_ref, u_ref, x0_ref, skip_ref, o_ref, *, tile):
    y_cos = jnp.dot(ci_ref[...], yre_ref[...], preferred_element_type=F32)
    y_sin = jnp.dot(si_ref[...], yim_ref[...], preferred_element_type=F32)
    row = lax.broadcasted_iota(jnp.int32, y_cos.shape, 0) + pl.program_id(0) * tile
    nyq = jnp.where(row % 2 == 0, 1.0, -1.0) * yim_ref[0:1, :].astype(F32)
    y = y_cos + jnp.where(row == 0, 0.0, y_sin) + nyq
    o_ref[...] = x0_ref[...] * (y + u_ref[...] * skip_ref[...])


def _dft_inv(ci, si, yre, yim, u, x0, skip):
    n = ci.shape[0]
    cols = yre.shape[1]
    tile = min(DFT_TILE, n)
    return pl.pallas_call(
        functools.partial(_dft_inv_kernel, tile=tile),
        grid=(n // tile, cols // D_HY),
        in_specs=[pl.BlockSpec((tile, n), lambda t, c: (t, 0)),
                  pl.BlockSpec((tile, n), lambda t, c: (t, 0)),
                  pl.BlockSpec((n, D_HY), lambda t, c: (0, c)),
                  pl.BlockSpec((n, D_HY), lambda t, c: (0, c)),
                  pl.BlockSpec((tile, D_HY), lambda t, c: (t, c)),
                  pl.BlockSpec((tile, D_HY), lambda t, c: (t, c)),
                  pl.BlockSpec((1, D_HY), lambda t, c: (0, 0))],
        out_specs=pl.BlockSpec((tile, D_HY), lambda t, c: (t, c)),
        out_shape=jax.ShapeDtypeStruct((n, cols), F32),
        compiler_params=_cparams("parallel", "parallel"),
        name="dft_inv",
    )(ci, si, yre, yim, u, x0, skip.reshape(1, D_HY))


def _fft_tables(n):
    n1 = 2 * n // FFT_BLOCK
    half = n1 // 2
    nf = half + 1
    groups = FFT_BLOCK // FFT_SLAB
    f1 = jnp.arange(nf, dtype=F32)[:, None]
    s1 = jnp.arange(half, dtype=F32)[None, :]
    ang1 = (2.0 * math.pi / n1) * f1 * s1
    eye = jnp.eye(FFT_SLAB, dtype=F32)
    m1 = jnp.concatenate([jnp.kron(jnp.cos(ang1), eye), jnp.kron(-jnp.sin(ang1), eye)], axis=0)
    m1_inv = jnp.concatenate([jnp.kron(jnp.cos(ang1).T, eye), jnp.kron(-jnp.sin(ang1).T, eye)], axis=1)
    k = jnp.arange(FFT_BLOCK, dtype=F32)
    ang2 = (2.0 * math.pi / FFT_BLOCK) * k[:, None] * k[None, :]
    c2, d2 = jnp.cos(ang2), jnp.sin(ang2)
    g2 = jnp.block([[c2, d2], [-d2, c2]])
    g2_inv = jnp.block([[c2, -d2], [d2, c2]])
    psi = (2.0 * math.pi / (2 * n)) * f1 * k[None, :]
    tw = jnp.stack([jnp.cos(psi), jnp.sin(psi)])
    tw_s2 = jnp.broadcast_to(tw[..., None], (2, nf, FFT_BLOCK, LANES))
    tw_s1 = tw.reshape(2, nf, groups, FFT_SLAB).transpose(2, 0, 1, 3).reshape(groups, 2, nf * FFT_SLAB)
    tw_s1 = jnp.broadcast_to(tw_s1[..., None], (groups, 2, nf * FFT_SLAB, LANES))
    return dict(n=n, half=half, nf=nf, m1=m1.astype(BF16), m1_inv=m1_inv.astype(BF16), g2=g2.astype(BF16),
                g2_inv=g2_inv.astype(BF16), tw_s1=tw_s1, tw_s2=tw_s2)


def _lane_tile(x, width):
    return jnp.concatenate([x] * (width // x.shape[-1]), axis=-1) if width != x.shape[-1] else x


def _fft_s1_kernel(u_ref, m_ref, tw_ref, o_ref):
    h, slab, tc = u_ref.shape
    r = jnp.dot(m_ref[...], u_ref[...].reshape(h * slab, tc), preferred_element_type=F32)
    rows = r.shape[0] // 2
    re, im = r[:rows], r[rows:]
    cs = _lane_tile(tw_ref[0], tc)
    sn = _lane_tile(tw_ref[1], tc)
    o_ref[0] = (re * cs + im * sn).astype(BF16).reshape(rows // slab, slab, tc)
    o_ref[1] = (im * cs - re * sn).astype(BF16).reshape(rows // slab, slab, tc)


def _fft_s1(ub, tabs):
    n, cols = ub.shape
    half, nf = tabs["half"], tabs["nf"]
    groups = FFT_BLOCK // FFT_SLAB
    tc = min(FFT_COLS, cols)
    out = pl.pallas_call(
        _fft_s1_kernel,
        grid=(groups, cols // tc),
        in_specs=[pl.BlockSpec((half, None, FFT_SLAB, tc), lambda m, c: (0, m, 0, c)),
                  _resident(tabs["m1"].shape),
                  pl.BlockSpec((None, 2, nf * FFT_SLAB, LANES), lambda m, c: (m, 0, 0, 0))],
        out_specs=pl.BlockSpec((2, nf, None, FFT_SLAB, tc), lambda m, c: (0, 0, m, 0, c)),
        out_shape=jax.ShapeDtypeStruct((2, nf, groups, FFT_SLAB, cols), BF16),
        compiler_params=_cparams("parallel", "parallel"),
        name="fft_stage1",
    )(ub.reshape(half, groups, FFT_SLAB, cols), tabs["m1"], tabs["tw_s1"])
    return out.reshape(2, nf, FFT_BLOCK, cols)


def _fft_mid_kernel(a_ref, g_ref, gi_ref, k_ref, tw_ref, b_ref, *, gf, half, scale):
    g2 = g_ref[...]
    g2_inv = gi_ref[...]
    tc = a_ref.shape[-1]
    for j in range(gf):
        x = jnp.dot(g2, jnp.concatenate([a_ref[0, j], a_ref[1, j]], axis=0), preferred_element_type=F32)
        xre, xim = x[:FFT_BLOCK], x[FFT_BLOCK:]
        kre, kim = k_ref[0, j], k_ref[1, j]
        f1 = pl.program_id(0) * gf + j
        w = jnp.where(jnp.logical_or(f1 == 0, f1 == half), scale, 2.0 * scale)
        y = jnp.concatenate([w * (xre * kre - xim * kim), w * (xre * kim + xim * kre)], axis=0).astype(BF16)
        b = jnp.dot(g2_inv, y, preferred_element_type=F32)
        bre, bim = b[:FFT_BLOCK], b[FFT_BLOCK:]
        cs = _lane_tile(tw_ref[0, j], tc)
        sn = _lane_tile(tw_ref[1, j], tc)
        b_ref[0, j] = (bre * cs - bim * sn).astype(BF16)
        b_ref[1, j] = (bre * sn + bim * cs).astype(BF16)


def _fft_group(nf):
    return max(g for g in range(1, 12) if nf % g == 0)


def _fft_mid(a, tabs, spec):
    _, nf, _, cols = a.shape
    gf = _fft_group(nf)
    blk = lambda w, idx: pl.BlockSpec((2, gf, FFT_BLOCK, w), idx)
    return pl.pallas_call(
        functools.partial(_fft_mid_kernel, gf=gf, half=tabs["half"], scale=0.5 / tabs["n"]),
        grid=(nf // gf, cols // D_HY),
        in_specs=[blk(D_HY, lambda g, c: (0, g, 0, c)), _resident(tabs["g2"].shape),
                  _resident(tabs["g2_inv"].shape), blk(D_HY, lambda g, c: (0, g, 0, 0)),
                  blk(LANES, lambda g, c: (0, g, 0, 0))],
        out_specs=blk(D_HY, lambda g, c: (0, g, 0, c)),
        out_shape=jax.ShapeDtypeStruct(a.shape, BF16),
        compiler_params=_cparams("parallel", "parallel"),
        name="fft_mid",
    )(a, tabs["g2"], tabs["g2_inv"], spec, tabs["tw_s2"])


def _fft_s2_filter_kernel(a_ref, g_ref, k_ref, *, gf):
    g2 = g_ref[...]
    for j in range(gf):
        x = jnp.dot(g2, jnp.concatenate([a_ref[0, j], a_ref[1, j]], axis=0), preferred_element_type=F32)
        xre, xim = x[:FFT_BLOCK], x[FFT_BLOCK:]
        k_ref[0, j] = xre[:, :D_HY] + xre[:, D_HY:]
        k_ref[1, j] = xim[:, :D_HY] - xim[:, D_HY:]


def _fft_s2_filter(a, tabs):
    _, nf, _, cols = a.shape
    gf = _fft_group(nf)
    return pl.pallas_call(
        functools.partial(_fft_s2_filter_kernel, gf=gf),
        grid=(nf // gf,),
        in_specs=[pl.BlockSpec((2, gf, FFT_BLOCK, cols), lambda g: (0, g, 0, 0)), _resident(tabs["g2"].shape)],
        out_specs=pl.BlockSpec((2, gf, FFT_BLOCK, D_HY), lambda g: (0, g, 0, 0)),
        out_shape=jax.ShapeDtypeStruct((2, nf, FFT_BLOCK, D_HY), F32),
        compiler_params=_cparams("parallel"),
        name="fft_stage2_filter",
    )(a, tabs["g2"])


def _fft_s1_inv_kernel(b_ref, m_ref, u_ref, x0_ref, skip_ref, o_ref):
    two, nf, slab, tc = b_ref.shape
    h = u_ref.shape[0]
    y = jnp.dot(m_ref[...], b_ref[...].reshape(two * nf * slab, tc), preferred_element_type=F32)
    u = u_ref[...].reshape(h * slab, tc).astype(F32)
    x0 = x0_ref[...].reshape(h * slab, tc)
    o_ref[...] = (x0 * (y + u * _lane_tile(skip_ref[...], tc))).reshape(h, slab, tc)


def _fft_s1_inv(b, tabs, u, x0, skip):
    _, nf, _, cols = b.shape
    n, half = tabs["n"], tabs["half"]
    groups = FFT_BLOCK // FFT_SLAB
    tc = min(FFT_COLS, cols)
    rows4 = lambda a: a.reshape(half, groups, FFT_SLAB, cols)
    tspec = pl.BlockSpec((half, None, FFT_SLAB, tc), lambda m, c: (0, m, 0, c))
    out = pl.pallas_call(
        _fft_s1_inv_kernel,
        grid=(groups, cols // tc),
        in_specs=[pl.BlockSpec((2, nf, None, FFT_SLAB, tc), lambda m, c: (0, 0, m, 0, c)),
                  _resident(tabs["m1_inv"].shape), tspec, tspec,
                  pl.BlockSpec((1, D_HY), lambda m, c: (0, 0))],
        out_specs=tspec,
        out_shape=jax.ShapeDtypeStruct((half, groups, FFT_SLAB, cols), F32),
        compiler_params=_cparams("parallel", "parallel"),
        name="fft_stage1_inv",
    )(b.reshape(2, nf, groups, FFT_SLAB, cols), tabs["m1_inv"], rows4(u), rows4(x0), skip.reshape(1, D_HY))
    return out.reshape(n, cols)


def _hyena_fft(zh, bsz, conv_w, conv_b, filt, tabs, skip):
    ub, x0 = _hyena_pre(zh, conv_w, conv_b, bsz)
    kk = _hyena_filter_taps(zh.shape[0], *filt)
    spec = _fft_s2_filter(_fft_s1(kk, tabs), tabs)
    return _fft_s1_inv(_fft_mid(_fft_s1(ub, tabs), tabs, spec), tabs, ub, x0, skip)


def _hyena(zh, bsz, conv_w, conv_b, filt, dft, skip):
    cm, sm = dft
    ub, x0 = _hyena_pre(zh, conv_w, conv_b, bsz)
    kk = _hyena_filter_taps(zh.shape[0], *filt)
    kre, kim = _dft_filter(cm, sm, kk)
    yre, yim = _dft_fwd(cm, sm, ub, kre, kim)
    return _dft_inv(cm, sm, yre, yim, ub, x0, skip)


def _attend_block(q, kv, sink_ref, masks):
    k_low = [kv[:, g * LANES:(g + 1) * LANES] for g in range(N_KVH)]
    ones = jnp.ones((kv.shape[0], LANES), BF16)
    v_aug = [jnp.concatenate([kv[:, (N_KVH + g) * LANES:(N_KVH + g + 1) * LANES], ones], axis=1)
             for g in range(N_KVH)]
    nq = q.shape[0]
    windowed = masks is not None
    if windowed:
        valid_prev, valid_next = masks
    cols_per_g = D_ATTN // LANES // N_KVH
    out_cols = []
    for g in range(N_KVH):
        cols = [q[:, c * LANES:(c + 1) * LANES] for c in range(g * cols_per_g, (g + 1) * cols_per_g)]
        qg = jnp.concatenate(cols + [pltpu.roll(cq, HEAD_DIM, axis=1) for cq in cols], axis=0).astype(BF16)
        s_all = lax.dot_general(qg, k_low[g], (((1,), (1,)), ((), ())), preferred_element_type=F32)
        es, sinks = [], []
        for hb in range(2 * cols_per_g):
            h = 2 * (g * cols_per_g + hb % cols_per_g) + hb // cols_per_g
            s = s_all[hb * nq:(hb + 1) * nq]
            if windowed:
                s = jnp.concatenate([
                    jnp.where(valid_prev, s[:, :ATT_BLOCK], NEG_INF),
                    s[:, ATT_BLOCK:2 * ATT_BLOCK],
                    jnp.where(valid_next, s[:, 2 * ATT_BLOCK:3 * ATT_BLOCK], NEG_INF),
                    s[:, 3 * ATT_BLOCK:]], axis=1)
            sk = sink_ref[h:h + 1, 0:1] * LOG2E
            m = jnp.maximum(jnp.max(s, axis=-1, keepdims=True), sk)
            es.append(jnp.exp2(s - m).astype(BF16))
            sinks.append(jnp.exp2(sk - m))
        o_all = jnp.dot(jnp.concatenate(es, axis=0), v_aug[g], preferred_element_type=F32)
        outs = []
        for hb in range(2 * cols_per_g):
            o = o_all[hb * nq:(hb + 1) * nq]
            outs.append(o[:, :LANES] / (o[:, LANES:] + sinks[hb]))
        out_cols += [outs[ci] + pltpu.roll(outs[cols_per_g + ci], HEAD_DIM, axis=1) for ci in range(cols_per_g)]
    return out_cols


def _attn_kernel(*refs, windowed, n_sub):
    if windowed:
        (q_ref, kvp_ref, kvc_ref, kvn_ref, kvx_ref, sink_ref, o_ref) = refs
        i = pl.program_id(1)
        last = pl.num_programs(1) - 1
        r = lax.broadcasted_iota(jnp.int32, (ATT_BLOCK, ATT_BLOCK), 0)
        j = lax.broadcasted_iota(jnp.int32, (ATT_BLOCK, ATT_BLOCK), 1)
        cur = kvc_ref[0]
        blocks = ([kvp_ref[0]] + [cur[n * ATT_BLOCK:(n + 1) * ATT_BLOCK] for n in range(n_sub)]
                  + [kvn_ref[0]])
    else:
        (q_ref, kvx_ref, sink_ref, o_ref) = refs
    ctx = kvx_ref[0]
    for n in range(n_sub):
        rows = slice(n * ATT_BLOCK, (n + 1) * ATT_BLOCK)
        q = q_ref[0, rows, :].astype(F32)
        if windowed:
            no_prev = jnp.where(i == 0, ATT_BLOCK, 0) if n == 0 else 0
            no_next = jnp.where(i == last, ATT_BLOCK, 0) if n == n_sub - 1 else 0
            masks = (j >= r + no_prev, j <= r - no_next)
            kv = jnp.concatenate(blocks[n:n + 3] + [ctx], axis=0)
        else:
            masks = None
            kv = ctx
        for c, col in enumerate(_attend_block(q, kv, sink_ref, masks)):
            o_ref[0, rows, c * LANES:(c + 1) * LANES] = col


def _attention(q, kv, kvx, sink, windowed):
    bsz, lq, _ = q.shape
    n_ctx, kv_w = kvx.shape[1:]
    nb = lq // ATT_BLOCK
    n_sub = ATT_STEP if nb % ATT_STEP == 0 else 1
    step = n_sub * ATT_BLOCK
    sink_t = jnp.broadcast_to(sink.reshape(N_QH, 1), (N_QH, LANES))
    qspec = pl.BlockSpec((1, step, D_ATTN), lambda b, i: (b, i, 0))
    xspec = pl.BlockSpec((1, n_ctx, kv_w), lambda b, i: (b, 0, 0))
    sspec = pl.BlockSpec((N_QH, LANES), lambda b, i: (0, 0))
    if windowed:
        prev = pl.BlockSpec((1, ATT_BLOCK, kv_w), lambda b, i: (b, jnp.maximum(n_sub * i - 1, 0), 0))
        cur = pl.BlockSpec((1, step, kv_w), lambda b, i: (b, i, 0))
        nxt = pl.BlockSpec((1, ATT_BLOCK, kv_w), lambda b, i: (b, jnp.minimum(n_sub * (i + 1), nb - 1), 0))
        in_specs = [qspec, prev, cur, nxt, xspec, sspec]
        args = (q, kv, kv, kv, kvx, sink_t)
    else:
        in_specs = [qspec, xspec, sspec]
        args = (q, kvx, sink_t)
    return pl.pallas_call(
        functools.partial(_attn_kernel, windowed=windowed, n_sub=n_sub),
        grid=(bsz, lq // step),
        in_specs=in_specs,
        out_specs=pl.BlockSpec((1, step, D_ATTN), lambda b, i: (b, i, 0)),
        out_shape=jax.ShapeDtypeStruct(q.shape, F32),
        compiler_params=_cparams("parallel", "parallel"),
        name="attention",
    )(*args)


def _rope_tables(n_lat):
    rows = n_lat // GRID_W
    r = jnp.repeat(jnp.arange(rows, dtype=F32), GRID_W)
    col = jnp.tile(jnp.arange(GRID_W, dtype=F32), rows)
    inv = ROPE_THETA ** (-jnp.arange(ROPE_PAIRS_AXIS, dtype=F32) / ROPE_PAIRS_AXIS)
    ang = jnp.concatenate([r[:, None] * inv, col[:, None] * inv], axis=-1)
    cos, sin = jnp.cos(ang), jnp.sin(ang)
    zero = jnp.zeros_like(sin)
    reps = D_ATTN // HEAD_DIM
    cos_t = jnp.tile(jnp.concatenate([cos, cos], axis=-1), (1, reps))
    sin_a = jnp.tile(jnp.concatenate([-sin, zero], axis=-1), (1, reps))
    sin_b = jnp.tile(jnp.concatenate([zero, sin], axis=-1), (1, reps))
    return cos_t, sin_a, sin_b


def kernel(x, c, ctx, c_ctx, w_mod, b_mod, norm_g, ffn_w1, ffn_w2, w_in, w_out, lru_conv_w, lru_conv_b,
           lru_wa, lru_ba, lru_wx, lru_bx, lru_lam, hy_conv_w, hy_conv_b, hy_fw0, hy_fb0, hy_fw_in,
           hy_fb_in, hy_freq, hy_fw_last, hy_skip, attn_sink, final_g):
    bsz, n_lat, d = x.shape
    n_ctx = ctx.shape[1]
    depth = w_mod.shape[0]
    assert n_lat % ATT_BLOCK == 0 and n_ctx % ATT_BLOCK == 0 and n_lat % GRID_W == 0
    assert ATT_BLOCK % SCAN_GROUP == 0 and ATT_BLOCK % LANES == 0

    mod_rows = -(-(bsz + 1) // (2 * SUBLANES)) * (2 * SUBLANES)
    c_rows = jnp.zeros((mod_rows, d), F32).at[:bsz].set(c).at[bsz].set(c_ctx)
    mod_all = _modulation(c_rows, w_mod, b_mod).reshape(depth, mod_rows, N_MOD, d)
    lat_row = lambda b: b
    ctx_row = lambda b: bsz

    rope_tabs = _rope_tables(n_lat)
    fft_lat = _fft_tables(n_lat)
    w1_b = ffn_w1.astype(BF16)
    w2_b = ffn_w2.astype(BF16)
    w_in_b = w_in.astype(BF16)
    w_out_b = w_out.astype(BF16)

    xc = ctx
    for l in range(depth):
        need_ctx = l < depth - 1
        mod = mod_all[l]
        filt = (hy_fw0[l], hy_fb0[l], hy_fw_in[l], hy_fb_in[l], hy_freq[l], hy_fw_last[l])

        x = _ffn(x, mod, lat_row, norm_g[l, 0], w1_b, w2_b, l, 0, 0)
        xc = _ffn(xc, mod, ctx_row, norm_g[l, 0], w1_b, w2_b, l, 0, 0)

        xl, gl, zh, q, kv = _input_proj(x, mod, lat_row, norm_g[l, 1], w_in_b, l, rope_tabs)
        xlc, glc, zhc, qc, kvc = _input_proj(xc, mod, ctx_row, norm_g[l, 1], w_in_b, l, None)

        y_lru, yc_lru = _rglru(xl, gl, xlc, glc, lru_conv_w[l], lru_conv_b[l], lru_wa[l], lru_ba[l],
                               lru_wx[l], lru_bx[l], lru_lam[l], bsz)
        y_hy = _hyena_fft(zh, bsz, hy_conv_w[l], hy_conv_b[l], filt, fft_lat, hy_skip[l])
        y_att = _attention(q, kv, kvc, attn_sink[l], True)
        x = _ffn(x, mod, lat_row, norm_g[l, 2], w1_b, w2_b, l, 1, 6, mixer=(y_lru, y_hy, y_att, w_out_b),
                 final_g=None if need_ctx else final_g)

        if need_ctx:
            yc_hy = _hyena(zhc, bsz, hy_conv_w[l], hy_conv_b[l], filt, _dft_matrices(n_ctx), hy_skip[l])
            yc_att = _attention(qc, None, kvc, attn_sink[l], False)
            xc = _ffn(xc, mod, ctx_row, norm_g[l, 2], w1_b, w2_b, l, 1, 6,
                      mixer=(yc_lru, yc_hy, yc_att, w_out_b))
    return x
```

```python
import functools
import math

import jax
import jax.numpy as jnp
from jax import lax
from jax.experimental import pallas as pl
from jax.experimental.pallas import tpu as pltpu

F32 = jnp.float32
BF16 = jnp.bfloat16

NORM_EPS = 1e-6
N_MOD = 9
MACARON_W = 0.5
D_LRU = 256
LRU_BLOCKS = 4
LRU_C = 8.0
LRU_LEFT = 2
D_HY = 256
HY_LEFT = 1
HY_EMB = 33
HY_BANDS = (HY_EMB - 1) // 2
HY_FAST = 0.3
HY_SLOW = 1.5
HY_TARGET = 1e-2
N_QH = 8
N_KVH = 2
HEAD_DIM = 64
D_ATTN = N_QH * HEAD_DIM
D_KV = N_KVH * HEAD_DIM
WINDOW = 128
ATT_BLOCK = 128
ATT_STEP = 2
GRID_W = 64
ROPE_THETA = 10000.0
ROPE_PAIRS_AXIS = HEAD_DIM // 4
NEG_INF = -1e30
LOG2E = math.log2(math.e)

LANES = 128
SUBLANES = 8
VMEM_LIMIT_BYTES = 56 * 1024 * 1024
ROW_TILE = 512
FFN_SPLIT = 2
DFT_TILE = 1024
DFT_FILTER_TILE = 512
FFT_BLOCK = 128
FFT_SLAB = 16
FFT_COLS = 512
SCAN_GROUP = 64


def _cparams(*sem):
    return pltpu.CompilerParams(dimension_semantics=sem, vmem_limit_bytes=VMEM_LIMIT_BYTES)


def _row_tile(rows):
    return min(ROW_TILE, rows)


def _resident(shape, index=None):
    index = (0,) * len(shape) if index is None else index
    return pl.BlockSpec(shape, lambda *_: index, pipeline_mode=pl.Buffered(1))


def _ada_norm(x, g, shift, scale):
    y = x * lax.rsqrt(jnp.mean(x * x, axis=-1, keepdims=True) + NORM_EPS)
    return (y * g) * (1.0 + scale) + shift


def _sigmoid(x):
    return 0.5 * (1.0 + jnp.tanh(0.5 * x))


def _gelu_tanh(x):
    return 0.5 * x * (1.0 + jnp.tanh(math.sqrt(2.0 / math.pi) * (x + 0.044715 * (x * x * x))))


def _softplus(x):
    return jnp.maximum(x, 0.0) + jnp.log1p(jnp.exp(-jnp.abs(x)))


def _dwconv_tile(x_ref, w, bias, left, i, tile):
    n_rows, width = x_ref.shape
    n_tiles = n_rows // tile
    group = SUBLANES * 4 // x_ref.dtype.itemsize
    r0 = pl.multiple_of(i * tile, tile)
    zeros_i = jnp.zeros((group, width), jnp.int32)
    before = x_ref[pl.ds(pl.multiple_of(jnp.maximum(r0 - group, 0), group), group), :].astype(F32)
    after = x_ref[pl.ds(pl.multiple_of(jnp.minimum(r0 + tile, n_rows - group), group), group), :].astype(F32)
    before = jnp.where(zeros_i + i > 0, before, 0.0)
    after = jnp.where(zeros_i + i < n_tiles - 1, after, 0.0)
    ext = jnp.concatenate([before, x_ref[pl.ds(r0, tile), :].astype(F32), after], axis=0)
    out = jnp.broadcast_to(bias, (tile, width))
    for k in range(w.shape[0]):
        off = k - left
        sh = ext if off == 0 else pltpu.roll(ext, (-off) % ext.shape[0], axis=0)
        out = out + sh[group:group + tile] * w[k:k + 1, :]
    return out


def _mod_kernel(c_ref, w_ref, b_ref, o_ref):
    cv = c_ref[...]
    s = cv * _sigmoid(cv)
    rows = s.shape[0]
    s_hi = s.astype(BF16)
    s_lo = (s - s_hi.astype(F32)).astype(BF16)
    w = w_ref[0]
    w_hi = w.astype(BF16)
    w_lo = (w - w_hi.astype(F32)).astype(BF16)
    p = jnp.dot(jnp.concatenate([s_hi, s_lo], axis=0), w_hi, preferred_element_type=F32)
    o_ref[0] = p[:rows] + p[rows:] + jnp.dot(s_hi, w_lo, preferred_element_type=F32) + b_ref[0]


def _modulation(c_rows, w_mod, b_mod):
    depth, d, nd = w_mod.shape
    rows = c_rows.shape[0]
    tn = nd // 8
    return pl.pallas_call(
        _mod_kernel,
        grid=(depth, nd // tn),
        in_specs=[
            pl.BlockSpec((rows, d), lambda l, j: (0, 0)),
            pl.BlockSpec((1, d, tn), lambda l, j: (l, 0, j)),
            pl.BlockSpec((1, 1, tn), lambda l, j: (l, 0, j)),
        ],
        out_specs=pl.BlockSpec((1, rows, tn), lambda l, j: (l, 0, j)),
        out_shape=jax.ShapeDtypeStruct((depth, rows, nd), F32),
        compiler_params=_cparams("parallel", "parallel"),
        name="modulation",
    )(c_rows, w_mod, b_mod.reshape(depth, 1, nd))


def _ffn_kernel(*refs, i_mod, d_ff, mixer, final):
    refs = list(refs)
    o_ref = refs.pop()
    x_ref, mod_ref, g_ref, w1_ref, w2_ref = refs[:5]
    rest = refs[5:]
    m = mod_ref[...]
    g = g_ref[...]
    tm = x_ref.shape[1]
    n_sub = FFN_SPLIT if tm % (FFN_SPLIT * 2 * SUBLANES) == 0 else 1
    for n in range(n_sub):
        rows = slice(n * tm // n_sub, (n + 1) * tm // n_sub)
        x = x_ref[0, rows, :]
        if mixer:
            yl_ref, yh_ref, ya_ref, wo_ref = rest[:4]
            y = (jnp.dot(yl_ref[rows, :].astype(BF16), wo_ref[0:D_LRU, :], preferred_element_type=F32)
                 + jnp.dot(yh_ref[rows, :].astype(BF16), wo_ref[D_LRU:D_LRU + D_HY, :], preferred_element_type=F32)
                 + jnp.dot(ya_ref[0, rows, :].astype(BF16), wo_ref[D_LRU + D_HY:, :], preferred_element_type=F32))
            x = x + m[5:6] * y
        h = _ada_norm(x, g, m[i_mod:i_mod + 1], m[i_mod + 1:i_mod + 2])
        ab = jnp.dot(h.astype(BF16), w1_ref[...], preferred_element_type=F32)
        a = ab[:, :d_ff]
        b = ab[:, d_ff:]
        gated = (a * _sigmoid(a)) * b
        y = jnp.dot(gated.astype(BF16), w2_ref[...], preferred_element_type=F32)
        x = x + (MACARON_W * m[i_mod + 2:i_mod + 3]) * y
        if final:
            x = (x * lax.rsqrt(jnp.mean(x * x, axis=-1, keepdims=True) + NORM_EPS)) * rest[-1][...]
        o_ref[0, rows, :] = x


def _ffn(x, mod, mod_row, g, w1, w2, layer, which, i_mod, mixer=None, final_g=None):
    bsz, rows, d = x.shape
    d_ff = w2.shape[2]
    tm = _row_tile(rows)
    in_specs = [
        pl.BlockSpec((1, tm, d), lambda b, t: (b, t, 0)),
        pl.BlockSpec((None, N_MOD, d), lambda b, t: (mod_row(b), 0, 0)),
        pl.BlockSpec((1, d), lambda b, t: (0, 0)),
        _resident((None, None, d, 2 * d_ff), (layer, which, 0, 0)),
        _resident((None, None, d_ff, d), (layer, which, 0, 0)),
    ]
    args = [x, mod, g.reshape(1, d), w1, w2]
    if mixer is not None:
        y_lru, y_hy, y_att, w_out = mixer
        in_specs += [
            pl.BlockSpec((tm, D_LRU), lambda b, t: (t, b)),
            pl.BlockSpec((tm, D_HY), lambda b, t: (t, b)),
            pl.BlockSpec((1, tm, D_ATTN), lambda b, t: (b, t, 0)),
            _resident((None,) + w_out.shape[1:], (layer, 0, 0)),
        ]
        args += [y_lru, y_hy, y_att, w_out]
    if final_g is not None:
        in_specs.append(pl.BlockSpec((1, d), lambda b, t: (0, 0)))
        args.append(final_g.reshape(1, d))
    return pl.pallas_call(
        functools.partial(_ffn_kernel, i_mod=i_mod, d_ff=d_ff, mixer=mixer is not None,
                          final=final_g is not None),
        grid=(bsz, rows // tm),
        in_specs=in_specs,
        out_specs=pl.BlockSpec((1, tm, d), lambda b, t: (b, t, 0)),
        out_shape=jax.ShapeDtypeStruct(x.shape, F32),
        compiler_params=_cparams("parallel", "parallel"),
        name="ffn",
    )(*args)


def _rope(x, cos_t, sin_a, sin_b):
    width = x.shape[-1]
    half = HEAD_DIM // 2
    up = pltpu.roll(x, width - half, axis=1)
    dn = pltpu.roll(x, half, axis=1)
    return x * cos_t + up * sin_a + dn * sin_b


def _proj_kernel(*refs, rope):
    if rope:
        (x_ref, mod_ref, g_ref, w_ref, cos_ref, sa_ref, sb_ref,
         xl_ref, gl_ref, zh_ref, q_ref, kv_ref) = refs
    else:
        (x_ref, mod_ref, g_ref, w_ref, xl_ref, gl_ref, zh_ref, q_ref, kv_ref) = refs
    m = mod_ref[...]
    g = g_ref[...]
    tm = x_ref.shape[1]
    n_sub = 2 if tm % (4 * SUBLANES) == 0 else 1
    for n in range(n_sub):
        rows = slice(n * tm // n_sub, (n + 1) * tm // n_sub)
        h = _ada_norm(x_ref[0, rows, :], g, m[3:4], m[4:5])
        z = jnp.dot(h.astype(BF16), w_ref[...], preferred_element_type=F32)
        o = 0
        xl_ref[rows, :] = z[:, o:o + D_LRU].astype(BF16); o += D_LRU
        gl_ref[rows, :] = z[:, o:o + D_LRU].astype(BF16); o += D_LRU
        zh_ref[rows, :] = z[:, o:o + 3 * D_HY].astype(BF16); o += 3 * D_HY
        q = z[:, o:o + D_ATTN]; o += D_ATTN
        k = z[:, o:o + D_KV]; o += D_KV
        v = z[:, o:o + D_KV]
        if rope:
            cos_t, sin_a, sin_b = cos_ref[rows, :], sa_ref[rows, :], sb_ref[rows, :]
            q = _rope(q, cos_t, sin_a, sin_b)
            k = _rope(k, cos_t[:, :D_KV], sin_a[:, :D_KV], sin_b[:, :D_KV])
        q_ref[0, rows, :] = (q * (HEAD_DIM ** -0.5 * LOG2E)).astype(BF16)
        low = lax.broadcasted_iota(jnp.int32, k.shape, 1) < HEAD_DIM
        parts = [jnp.where(low, k, 0.0), jnp.where(low, pltpu.roll(k, HEAD_DIM, axis=1), 0.0),
                 jnp.where(low, v, 0.0), jnp.where(low, pltpu.roll(v, HEAD_DIM, axis=1), 0.0)]
        kv_ref[0, rows, :] = jnp.concatenate(parts, axis=1).astype(BF16)


def _input_proj(x, mod, mod_row, g, w_in, layer, rope_tabs):
    bsz, rows, d = x.shape
    tm = _row_tile(rows)
    rope = rope_tabs is not None
    in_specs = [
        pl.BlockSpec((1, tm, d), lambda t, b: (b, t, 0)),
        pl.BlockSpec((None, N_MOD, d), lambda t, b: (mod_row(b), 0, 0)),
        pl.BlockSpec((1, d), lambda t, b: (0, 0)),
        _resident((None,) + w_in.shape[1:], (layer, 0, 0)),
    ]
    args = [x, mod, g.reshape(1, d), w_in]
    if rope:
        in_specs += [pl.BlockSpec((tm, D_ATTN), lambda t, b: (t, 0))] * 3
        args += list(rope_tabs)
    out_shape = [
        jax.ShapeDtypeStruct((rows, bsz * D_LRU), BF16),
        jax.ShapeDtypeStruct((rows, bsz * D_LRU), BF16),
        jax.ShapeDtypeStruct((rows, bsz * 3 * D_HY), BF16),
        jax.ShapeDtypeStruct((bsz, rows, D_ATTN), BF16),
        jax.ShapeDtypeStruct((bsz, rows, 2 * N_KVH * LANES), BF16),
    ]
    out_specs = [
        pl.BlockSpec((tm, D_LRU), lambda t, b: (t, b)),
        pl.BlockSpec((tm, D_LRU), lambda t, b: (t, b)),
        pl.BlockSpec((tm, 3 * D_HY), lambda t, b: (t, b)),
        pl.BlockSpec((1, tm, D_ATTN), lambda t, b: (b, t, 0)),
        pl.BlockSpec((1, tm, 2 * N_KVH * LANES), lambda t, b: (b, t, 0)),
    ]
    return pl.pallas_call(
        functools.partial(_proj_kernel, rope=rope),
        grid=(rows // tm, bsz),
        in_specs=in_specs,
        out_specs=out_specs,
        out_shape=out_shape,
        compiler_params=_cparams("parallel", "parallel"),
        name="input_proj",
    )(*args)


def _scan_chunk(a, b, row, reverse):
    for s in (1, 2, 4):
        if reverse:
            ok = row < SUBLANES - s
            sh = SUBLANES - s
        else:
            ok = row >= s
            sh = s
        a_sh = pltpu.roll(a, sh, axis=0)
        b_sh = pltpu.roll(b, sh, axis=0)
        b = jnp.where(ok, a * b_sh + b, b)
        a = jnp.where(ok, a * a_sh, a)
    return a, b


def _lru_kernel(xl_ref, gl_ref, xc_ref, gc_ref, cw_ref, cb_ref, wa_ref, ba_ref, wx_ref, bx_ref,
                lam_ref, y_ref, yc_ref, a_s, b_s, *, n_lat, n_ctx, tile):
    width = xl_ref.shape[1]
    cw = cw_ref[...]
    cb = cb_ref[...]
    neg_c = [(0.5 * LRU_C) * _softplus(-lam_ref[d]) for d in range(2)]

    def coeffs(x_ref, base, n_rows):
        def body(i, carry):
            r0 = pl.multiple_of(i * tile, tile)
            u = _dwconv_tile(x_ref, cw, cb, LRU_LEFT, i, tile)
            ub = u.astype(BF16)
            hu = 0.5 * u
            for d in range(2):
                t_a = jnp.tanh(jnp.dot(ub, wa_ref[d], preferred_element_type=F32) + ba_ref[d])
                t_x = jnp.tanh(jnp.dot(ub, wx_ref[d], preferred_element_type=F32) + bx_ref[d])
                neg_log_a = neg_c[d] + neg_c[d] * t_a
                a = jnp.exp(-neg_log_a)
                dst = pl.ds(pl.multiple_of(base + r0, SUBLANES), tile)
                a_s[d, dst, :] = a
                b_s[d, dst, :] = jnp.sqrt(jnp.tanh(neg_log_a) * (1.0 + a * a)) * (hu + hu * t_x)
            return carry
        lax.fori_loop(0, n_rows // tile, body, 0)

    coeffs(xc_ref, 0, n_ctx)
    coeffs(xl_ref, n_ctx, n_lat)

    row = lax.broadcasted_iota(jnp.int32, (SUBLANES, width), 0)

    def scan_group(d, group, h, reverse):
        r0 = pl.multiple_of(group * SCAN_GROUP, SCAN_GROUP)
        a = a_s[d, pl.ds(r0, SCAN_GROUP), :]
        b = b_s[d, pl.ds(r0, SCAN_GROUP), :]
        order = range(SCAN_GROUP // SUBLANES)
        parts = [_scan_chunk(a[c * SUBLANES:(c + 1) * SUBLANES], b[c * SUBLANES:(c + 1) * SUBLANES], row, reverse)
                 for c in order]
        for c in (reversed(order) if reverse else order):
            hh = parts[c][0] * h + parts[c][1]
            b_s[d, pl.ds(r0 + c * SUBLANES, SUBLANES), :] = hh
            h = hh[0:1, :] if reverse else hh[SUBLANES - 1:SUBLANES, :]
        return h

    ng_ctx = n_ctx // SCAN_GROUP
    ng_all = (n_ctx + n_lat) // SCAN_GROUP
    h0 = jnp.zeros((1, width), F32)

    def ctx_body(j, hs):
        return (scan_group(0, j, hs[0], False), scan_group(1, ng_ctx - 1 - j, hs[1], True))

    def lat_body(j, hs):
        return (scan_group(0, ng_ctx + j, hs[0], False), scan_group(1, ng_all - 1 - j, hs[1], True))

    hs = lax.fori_loop(0, ng_ctx, ctx_body, (h0, h0))
    lax.fori_loop(0, ng_all - ng_ctx, lat_body, hs)

    def finish(o_ref, g_ref, base, n_rows):
        def body(i, carry):
            r0 = pl.multiple_of(i * tile, tile)
            src = pl.ds(pl.multiple_of(base + r0, SUBLANES), tile)
            hsum = b_s[0, src, :] + b_s[1, src, :]
            o_ref[pl.ds(r0, tile), :] = (hsum * _gelu_tanh(g_ref[pl.ds(r0, tile), :].astype(F32))).astype(BF16)
            return carry
        lax.fori_loop(0, n_rows // tile, body, 0)

    finish(yc_ref, gc_ref, 0, n_ctx)
    finish(y_ref, gl_ref, n_ctx, n_lat)


def _block_diag(w):
    two, nb, bs, _ = w.shape
    eye = jnp.eye(nb, dtype=w.dtype)
    return jnp.einsum('dnij,nm->dnimj', w, eye).reshape(two, nb * bs, nb * bs)


def _rglru(xl, gl, xlc, glc, conv_w, conv_b, wa, ba, wx, bx, lam, bsz):
    n_lat, n_ctx = xl.shape[0], xlc.shape[0]
    width = LANES
    per_b = D_LRU // width
    tile = math.gcd(256, math.gcd(n_lat, n_ctx))
    wa_bd = (0.5 * _block_diag(wa)).astype(BF16)
    wx_bd = (0.5 * _block_diag(wx)).astype(BF16)
    ba = 0.5 * ba
    bx = 0.5 * bx
    col = lambda b, j: (0, b * per_b + j)
    par = lambda b, j: (0, j)
    par3 = lambda b, j: (0, 0, j)
    return pl.pallas_call(
        functools.partial(_lru_kernel, n_lat=n_lat, n_ctx=n_ctx, tile=tile),
        grid=(bsz, per_b),
        in_specs=[
            pl.BlockSpec((n_lat, width), col),
            pl.BlockSpec((n_lat, width), col),
            pl.BlockSpec((n_ctx, width), col),
            pl.BlockSpec((n_ctx, width), col),
            pl.BlockSpec((conv_w.shape[0], width), par),
            pl.BlockSpec((1, width), par),
            pl.BlockSpec((2, width, width), lambda b, j: (0, j, j)),
            pl.BlockSpec((2, 1, width), par3),
            pl.BlockSpec((2, width, width), lambda b, j: (0, j, j)),
            pl.BlockSpec((2, 1, width), par3),
            pl.BlockSpec((2, 1, width), par3),
        ],
        out_specs=[pl.BlockSpec((n_lat, width), col), pl.BlockSpec((n_ctx, width), col)],
        out_shape=[jax.ShapeDtypeStruct(xl.shape, BF16), jax.ShapeDtypeStruct(xlc.shape, BF16)],
        scratch_shapes=[pltpu.VMEM((2, n_ctx + n_lat, width), F32),
                        pltpu.VMEM((2, n_ctx + n_lat, width), F32)],
        compiler_params=_cparams("parallel", "parallel"),
        name="rglru",
    )(xl, gl, xlc, glc, conv_w, conv_b.reshape(1, -1), wa_bd, ba.reshape(2, 1, -1), wx_bd,
      bx.reshape(2, 1, -1), lam.reshape(2, 1, -1))


def _hy_pre_kernel(z0_ref, z1_ref, z2_ref, w0_ref, w1_ref, w2_ref, b0_ref, b1_ref, b2_ref,
                   ub_ref, x0_ref, *, tile):
    w0, w1, w2 = w0_ref[...], w1_ref[...], w2_ref[...]
    b0, b1, b2 = b0_ref[...], b1_ref[...], b2_ref[...]

    def body(i, carry):
        rows = pl.ds(pl.multiple_of(i * tile, tile), tile)
        x1 = _dwconv_tile(z1_ref, w1, b1, HY_LEFT, i, tile)
        v = _dwconv_tile(z2_ref, w2, b2, HY_LEFT, i, tile)
        ub_ref[rows, :] = (x1 * v).astype(BF16)
        x0_ref[rows, :] = _dwconv_tile(z0_ref, w0, b0, HY_LEFT, i, tile).astype(BF16)
        return carry
    lax.fori_loop(0, z0_ref.shape[0] // tile, body, 0)


def _hyena_pre(zh, conv_w, conv_b, bsz):
    rows = zh.shape[0]
    width = LANES
    per_b = D_HY // width
    tile = math.gcd(256, rows)
    zspec = lambda part: pl.BlockSpec((rows, width), lambda b, j: (0, b * 3 * per_b + part * per_b + j))
    wspec = lambda part: pl.BlockSpec((conv_w.shape[0], width), lambda b, j: (0, part * per_b + j))
    bspec = lambda part: pl.BlockSpec((1, width), lambda b, j: (0, part * per_b + j))
    ospec = pl.BlockSpec((rows, width), lambda b, j: (0, b * per_b + j))
    cb = conv_b.reshape(1, -1)
    return pl.pallas_call(
        functools.partial(_hy_pre_kernel, tile=tile),
        grid=(bsz, per_b),
        in_specs=[zspec(0), zspec(1), zspec(2), wspec(0), wspec(1), wspec(2), bspec(0), bspec(1), bspec(2)],
        out_specs=[ospec, ospec],
        out_shape=[jax.ShapeDtypeStruct((rows, bsz * D_HY), BF16),
                   jax.ShapeDtypeStruct((rows, bsz * D_HY), BF16)],
        compiler_params=_cparams("parallel", "parallel"),
        name="hyena_pre",
    )(zh, zh, zh, conv_w, conv_w, conv_w, cb, cb, cb)


def _hy_filter_kernel(z_ref, fw0_ref, fb0_ref, fwin_ref, fbin_ref, freq_ref, fwl_ref, dl_ref, o_ref, *, tile):
    hp = lax.Precision.HIGHEST
    z = z_ref[...]
    fr = freq_ref[...]
    hdn = jnp.sin(fr * (jnp.dot(z, fw0_ref[...], preferred_element_type=F32, precision=hp) + fb0_ref[...]))
    for j in range(fwin_ref.shape[0]):
        hdn = jnp.sin(fr * (jnp.dot(hdn, fwin_ref[j], preferred_element_type=F32, precision=hp) + fbin_ref[j]))
    k = jnp.dot(hdn, fwl_ref[...], preferred_element_type=F32, precision=hp)
    decay = jnp.exp(-z[:, 0:1] * dl_ref[...])
    k_fwd = k[:, :D_HY] * decay
    k_bwd = k[:, D_HY:] * decay
    row = lax.broadcasted_iota(jnp.int32, k_bwd.shape, 0) + pl.program_id(0) * tile
    k_bwd = jnp.where(row == 0, 0.0, k_bwd)
    o_ref[...] = jnp.concatenate([k_fwd, k_bwd], axis=-1).astype(BF16)


def _hyena_filter_taps(n, fw0, fb0, fw_in, fb_in, freq, fw_last):
    t = jnp.linspace(0.0, 1.0, n, dtype=F32)[:, None]
    w = 2.0 * math.pi * jnp.arange(n, dtype=F32)[:, None] / n
    f = jnp.linspace(1e-4, HY_BANDS - 1, HY_BANDS, dtype=F32)[None, :]
    z = jnp.concatenate([t, jnp.cos(f * w), -jnp.sin(f * w)], axis=-1)
    z = jnp.pad(z, ((0, 0), (0, LANES - HY_EMB)))
    fw0p = jnp.pad(fw0, ((0, LANES - HY_EMB), (0, 0)))
    max_decay = math.log(HY_TARGET) / HY_FAST
    min_decay = math.log(HY_TARGET) / HY_SLOW
    deltas = jnp.abs(jnp.linspace(min_decay, max_decay, D_HY, dtype=F32))[None, :]
    hid = fw0.shape[1]
    tile = min(512, n)
    full = lambda a: pl.BlockSpec(a.shape, lambda i: (0,) * a.ndim)
    args = [fw0p, fb0.reshape(1, hid), fw_in, fb_in.reshape(-1, 1, hid), freq.reshape(1, hid), fw_last, deltas]
    return pl.pallas_call(
        functools.partial(_hy_filter_kernel, tile=tile),
        grid=(n // tile,),
        in_specs=[pl.BlockSpec((tile, LANES), lambda i: (i, 0))] + [full(a) for a in args],
        out_specs=pl.BlockSpec((tile, 2 * D_HY), lambda i: (i, 0)),
        out_shape=jax.ShapeDtypeStruct((n, 2 * D_HY), BF16),
        compiler_params=_cparams("parallel"),
        name="hyena_filter",
    )(z, *args)


def _dft_expand_kernel(ca_ref, sa_ref, cb_ref, sb_ref, cm_ref, sm_ref, *, tile):
    ca, sa, cb, sb = ca_ref[...], sa_ref[...], cb_ref[...], sb_ref[...]
    row = lax.broadcasted_iota(jnp.int32, cb.shape, 0) + pl.program_id(0) * tile
    lane = lax.broadcasted_iota(jnp.int32, cb.shape, 1)
    alt = jnp.where(lane % 2 == 0, 1.0, -1.0)
    for s1 in range(cm_ref.shape[1] // LANES):
        c1 = ca[:, s1:s1 + 1]
        d1 = sa[:, s1:s1 + 1]
        cols = slice(s1 * LANES, (s1 + 1) * LANES)
        cm_ref[:, cols] = (c1 * cb - d1 * sb).astype(BF16)
        sm_ref[:, cols] = jnp.where(row == 0, alt, -(d1 * cb + c1 * sb)).astype(BF16)


def _dft_matrices(n):
    two_n = 2 * n
    f = jnp.arange(n, dtype=jnp.int32)[:, None]
    s_hi = jnp.arange(n // LANES, dtype=jnp.int32)[None, :] * LANES
    s_lo = jnp.arange(LANES, dtype=jnp.int32)[None, :]
    ang_a = ((f * s_hi) % two_n).astype(F32) * (2.0 * math.pi / two_n)
    ang_b = ((f * s_lo) % two_n).astype(F32) * (2.0 * math.pi / two_n)
    tile = min(DFT_TILE, n)
    hi_spec = pl.BlockSpec((tile, n // LANES), lambda i: (i, 0))
    lo_spec = pl.BlockSpec((tile, LANES), lambda i: (i, 0))
    return pl.pallas_call(
        functools.partial(_dft_expand_kernel, tile=tile),
        grid=(n // tile,),
        in_specs=[hi_spec, hi_spec, lo_spec, lo_spec],
        out_specs=[pl.BlockSpec((tile, n), lambda i: (i, 0))] * 2,
        out_shape=[jax.ShapeDtypeStruct((n, n), BF16)] * 2,
        compiler_params=_cparams("parallel"),
        name="dft_matrices",
    )(jnp.cos(ang_a), jnp.sin(ang_a), jnp.cos(ang_b), jnp.sin(ang_b))


def _dft_filter_kernel(cm_ref, sm_ref, kk_ref, kre_ref, kim_ref, *, tile):
    kk = kk_ref[...]
    xre = jnp.dot(cm_ref[...], kk, preferred_element_type=F32)
    xim = jnp.dot(sm_ref[...], kk, preferred_element_type=F32)
    row = lax.broadcasted_iota(jnp.int32, (tile, D_HY), 0) + pl.program_id(0) * tile
    kre_ref[...] = xre[:, :D_HY] + xre[:, D_HY:]
    kim_ref[...] = jnp.where(row == 0, xim[:, :D_HY] + xim[:, D_HY:], xim[:, :D_HY] - xim[:, D_HY:])


def _dft_filter(cm, sm, kk):
    n = cm.shape[0]
    tile = min(DFT_FILTER_TILE, n)
    return pl.pallas_call(
        functools.partial(_dft_filter_kernel, tile=tile),
        grid=(n // tile,),
        in_specs=[pl.BlockSpec((tile, n), lambda f: (f, 0)),
                  pl.BlockSpec((tile, n), lambda f: (f, 0)),
                  _resident(kk.shape)],
        out_specs=[pl.BlockSpec((tile, D_HY), lambda f: (f, 0))] * 2,
        out_shape=[jax.ShapeDtypeStruct((n, D_HY), F32)] * 2,
        compiler_params=_cparams("parallel"),
        name="dft_filter",
    )(cm, sm, kk)


def _dft_fwd_kernel(cm_ref, sm_ref, u_ref, kre_ref, kim_ref, yre_ref, yim_ref, *, tile):
    n_freq = cm_ref.shape[1]
    u = u_ref[...]
    xre = jnp.dot(cm_ref[...], u, preferred_element_type=F32)
    xim = jnp.dot(sm_ref[...], u, preferred_element_type=F32)
    kre = kre_ref[...]
    kim = kim_ref[...]
    row = lax.broadcasted_iota(jnp.int32, xre.shape, 0) + pl.program_id(0) * tile
    first = row == 0
    scale = jnp.where(first, 0.5 / n_freq, 1.0 / n_freq)
    yre_ref[...] = (scale * (xre * kre - jnp.where(first, 0.0, xim * kim))).astype(BF16)
    yim_ref[...] = (scale * jnp.where(first, xim * kim, xre * kim + xim * kre)).astype(BF16)


def _dft_fwd(cm, sm, ub, kre, kim):
    n = cm.shape[0]
    cols = ub.shape[1]
    tile = min(DFT_TILE, n)
    return pl.pallas_call(
        functools.partial(_dft_fwd_kernel, tile=tile),
        grid=(n // tile, cols // D_HY),
        in_specs=[pl.BlockSpec((tile, n), lambda f, c: (f, 0)),
                  pl.BlockSpec((tile, n), lambda f, c: (f, 0)),
                  pl.BlockSpec((n, D_HY), lambda f, c: (0, c)),
                  pl.BlockSpec((tile, D_HY), lambda f, c: (f, 0)),
                  pl.BlockSpec((tile, D_HY), lambda f, c: (f, 0))],
        out_specs=[pl.BlockSpec((tile, D_HY), lambda f, c: (f, c))] * 2,
        out_shape=[jax.ShapeDtypeStruct((n, cols), BF16)] * 2,
        compiler_params=_cparams("parallel", "parallel"),
        name="dft_fwd",
    )(cm, sm, ub, kre, kim)


def _dft_inv_kernel(ci_ref, si_ref, yre_ref, yim_ref, u_ref, x0_ref, skip_ref, o_ref, *, tile):
    y_cos = jnp.dot(ci_ref[...], yre_ref[...], preferred_element_type=F32)
    y_sin = jnp.dot(si_ref[...], yim_ref[...], preferred_element_type=F32)
    row = lax.broadcasted_iota(jnp.int32, y_cos.shape, 0) + pl.program_id(0) * tile
    nyq = jnp.where(row % 2 == 0, 1.0, -1.0) * yim_ref[0:1, :].astype(F32)
    y = y_cos + jnp.where(row == 0, 0.0, y_sin) + nyq
    o_ref[...] = (x0_ref[...] * (y + u_ref[...] * skip_ref[...])).astype(BF16)


def _dft_inv(ci, si, yre, yim, u, x0, skip):
    n = ci.shape[0]
    cols = yre.shape[1]
    tile = min(DFT_TILE, n)
    return pl.pallas_call(
        functools.partial(_dft_inv_kernel, tile=tile),
        grid=(n // tile, cols // D_HY),
        in_specs=[pl.BlockSpec((tile, n), lambda t, c: (t, 0)),
                  pl.BlockSpec((tile, n), lambda t, c: (t, 0)),
                  pl.BlockSpec((n, D_HY), lambda t, c: (0, c)),
                  pl.BlockSpec((n, D_HY), lambda t, c: (0, c)),
                  pl.BlockSpec((tile, D_HY), lambda t, c: (t, c)),
                  pl.BlockSpec((tile, D_HY), lambda t, c: (t, c)),
                  pl.BlockSpec((1, D_HY), lambda t, c: (0, 0))],
        out_specs=pl.BlockSpec((tile, D_HY), lambda t, c: (t, c)),
        out_shape=jax.ShapeDtypeStruct((n, cols), BF16),
        compiler_params=_cparams("parallel", "parallel"),
        name="dft_inv",
    )(ci, si, yre, yim, u, x0, skip.reshape(1, D_HY))


def _fft_tables(n):
    n1 = 2 * n // FFT_BLOCK
    half = n1 // 2
    nf = half + 1
    groups = FFT_BLOCK // FFT_SLAB
    f1 = jnp.arange(nf, dtype=F32)[:, None]
    s1 = jnp.arange(half, dtype=F32)[None, :]
    ang1 = (2.0 * math.pi / n1) * f1 * s1
    eye = jnp.eye(FFT_SLAB, dtype=F32)
    m1 = jnp.concatenate([jnp.kron(jnp.cos(ang1), eye), jnp.kron(-jnp.sin(ang1), eye)], axis=0)
    m1_inv = jnp.concatenate([jnp.kron(jnp.cos(ang1).T, eye), jnp.kron(-jnp.sin(ang1).T, eye)], axis=1)
    k = jnp.arange(FFT_BLOCK, dtype=F32)
    ang2 = (2.0 * math.pi / FFT_BLOCK) * k[:, None] * k[None, :]
    c2, d2 = jnp.cos(ang2), jnp.sin(ang2)
    g2 = jnp.block([[c2, d2], [-d2, c2]])
    g2_inv = jnp.block([[c2, -d2], [d2, c2]])
    psi = (2.0 * math.pi / (2 * n)) * f1 * k[None, :]
    tw = jnp.stack([jnp.cos(psi), jnp.sin(psi)])
    tw_s2 = jnp.broadcast_to(tw[..., None], (2, nf, FFT_BLOCK, LANES))
    tw_s1 = tw.reshape(2, nf, groups, FFT_SLAB).transpose(2, 0, 1, 3).reshape(groups, 2, nf * FFT_SLAB)
    tw_s1 = jnp.broadcast_to(tw_s1[..., None], (groups, 2, nf * FFT_SLAB, LANES))
    return dict(n=n, half=half, nf=nf, m1=m1.astype(BF16), m1_inv=m1_inv.astype(BF16), g2=g2.astype(BF16),
                g2_inv=g2_inv.astype(BF16), tw_s1=tw_s1, tw_s2=tw_s2)


def _lane_tile(x, width):
    return jnp.concatenate([x] * (width // x.shape[-1]), axis=-1) if width != x.shape[-1] else x


def _fft_s1_kernel(u_ref, m_ref, tw_ref, o_ref):
    h, slab, tc = u_ref.shape
    r = jnp.dot(m_ref[...], u_ref[...].reshape(h * slab, tc), preferred_element_type=F32)
    rows = r.shape[0] // 2
    re, im = r[:rows], r[rows:]
    cs = _lane_tile(tw_ref[0], tc)
    sn = _lane_tile(tw_ref[1], tc)
    o_ref[0] = (re * cs + im * sn).astype(BF16).reshape(rows // slab, slab, tc)
    o_ref[1] = (im * cs - re * sn).astype(BF16).reshape(rows // slab, slab, tc)


def _fft_s1(ub, tabs):
    n, cols = ub.shape
    half, nf = tabs["half"], tabs["nf"]
    groups = FFT_BLOCK // FFT_SLAB
    tc = min(FFT_COLS, cols)
    out = pl.pallas_call(
        _fft_s1_kernel,
        grid=(groups, cols // tc),
        in_specs=[pl.BlockSpec((half, None, FFT_SLAB, tc), lambda m, c: (0, m, 0, c)),
                  _resident(tabs["m1"].shape),
                  pl.BlockSpec((None, 2, nf * FFT_SLAB, LANES), lambda m, c: (m, 0, 0, 0))],
        out_specs=pl.BlockSpec((2, nf, None, FFT_SLAB, tc), lambda m, c: (0, 0, m, 0, c)),
        out_shape=jax.ShapeDtypeStruct((2, nf, groups, FFT_SLAB, cols), BF16),
        compiler_params=_cparams("parallel", "parallel"),
        name="fft_stage1",
    )(ub.reshape(half, groups, FFT_SLAB, cols), tabs["m1"], tabs["tw_s1"])
    return out.reshape(2, nf, FFT_BLOCK, cols)


def _fft_mid_kernel(a_ref, g_ref, gi_ref, k_ref, tw_ref, b_ref, *, gf, half, scale):
    g2 = g_ref[...]
    g2_inv = gi_ref[...]
    tc = a_ref.shape[-1]
    for j in range(gf):
        x = jnp.dot(g2, jnp.concatenate([a_ref[0, j], a_ref[1, j]], axis=0), preferred_element_type=F32)
        xre, xim = x[:FFT_BLOCK], x[FFT_BLOCK:]
        kre, kim = k_ref[0, j], k_ref[1, j]
        f1 = pl.program_id(0) * gf + j
        w = jnp.where(jnp.logical_or(f1 == 0, f1 == half), scale, 2.0 * scale)
        y = jnp.concatenate([w * (xre * kre - xim * kim), w * (xre * kim + xim * kre)], axis=0).astype(BF16)
        b = jnp.dot(g2_inv, y, preferred_element_type=F32)
        bre, bim = b[:FFT_BLOCK], b[FFT_BLOCK:]
        cs = _lane_tile(tw_ref[0, j], tc)
        sn = _lane_tile(tw_ref[1, j], tc)
        b_ref[0, j] = (bre * cs - bim * sn).astype(BF16)
        b_ref[1, j] = (bre * sn + bim * cs).astype(BF16)


def _fft_group(nf):
    return max(g for g in range(1, 12) if nf % g == 0)


def _fft_mid(a, tabs, spec):
    _, nf, _, cols = a.shape
    gf = _fft_group(nf)
    blk = lambda w, idx: pl.BlockSpec((2, gf, FFT_BLOCK, w), idx)
    return pl.pallas_call(
        functools.partial(_fft_mid_kernel, gf=gf, half=tabs["half"], scale=0.5 / tabs["n"]),
        grid=(nf // gf, cols // D_HY),
        in_specs=[blk(D_HY, lambda g, c: (0, g, 0, c)), _resident(tabs["g2"].shape),
                  _resident(tabs["g2_inv"].shape), blk(D_HY, lambda g, c: (0, g, 0, 0)),
                  blk(LANES, lambda g, c: (0, g, 0, 0))],
        out_specs=blk(D_HY, lambda g, c: (0, g, 0, c)),
        out_shape=jax.ShapeDtypeStruct(a.shape, BF16),
        compiler_params=_cparams("parallel", "parallel"),
        name="fft_mid",
    )(a, tabs["g2"], tabs["g2_inv"], spec, tabs["tw_s2"])


def _fft_s2_filter_kernel(a_ref, g_ref, k_ref, *, gf):
    g2 = g_ref[...]
    for j in range(gf):
        x = jnp.dot(g2, jnp.concatenate([a_ref[0, j], a_ref[1, j]], axis=0), preferred_element_type=F32)
        xre, xim = x[:FFT_BLOCK], x[FFT_BLOCK:]
        k_ref[0, j] = xre[:, :D_HY] + xre[:, D_HY:]
        k_ref[1, j] = xim[:, :D_HY] - xim[:, D_HY:]


def _fft_s2_filter(a, tabs):
    _, nf, _, cols = a.shape
    gf = _fft_group(nf)
    return pl.pallas_call(
        functools.partial(_fft_s2_filter_kernel, gf=gf),
        grid=(nf // gf,),
        in_specs=[pl.BlockSpec((2, gf, FFT_BLOCK, cols), lambda g: (0, g, 0, 0)), _resident(tabs["g2"].shape)],
        out_specs=pl.BlockSpec((2, gf, FFT_BLOCK, D_HY), lambda g: (0, g, 0, 0)),
        out_shape=jax.ShapeDtypeStruct((2, nf, FFT_BLOCK, D_HY), F32),
        compiler_params=_cparams("parallel"),
        name="fft_stage2_filter",
    )(a, tabs["g2"])


def _fft_s1_inv_kernel(b_ref, m_ref, u_ref, x0_ref, skip_ref, o_ref):
    two, nf, slab, tc = b_ref.shape
    h = u_ref.shape[0]
    y = jnp.dot(m_ref[...], b_ref[...].reshape(two * nf * slab, tc), preferred_element_type=F32)
    u = u_ref[...].reshape(h * slab, tc).astype(F32)
    x0 = x0_ref[...].reshape(h * slab, tc).astype(F32)
    o_ref[...] = (x0 * (y + u * _lane_tile(skip_ref[...], tc))).astype(BF16).reshape(h, slab, tc)


def _fft_s1_inv(b, tabs, u, x0, skip):
    _, nf, _, cols = b.shape
    n, half = tabs["n"], tabs["half"]
    groups = FFT_BLOCK // FFT_SLAB
    tc = min(FFT_COLS, cols)
    rows4 = lambda a: a.reshape(half, groups, FFT_SLAB, cols)
    tspec = pl.BlockSpec((half, None, FFT_SLAB, tc), lambda m, c: (0, m, 0, c))
    out = pl.pallas_call(
        _fft_s1_inv_kernel,
        grid=(groups, cols // tc),
        in_specs=[pl.BlockSpec((2, nf, None, FFT_SLAB, tc), lambda m, c: (0, 0, m, 0, c)),
                  _resident(tabs["m1_inv"].shape), tspec, tspec,
                  pl.BlockSpec((1, D_HY), lambda m, c: (0, 0))],
        out_specs=tspec,
        out_shape=jax.ShapeDtypeStruct((half, groups, FFT_SLAB, cols), BF16),
        compiler_params=_cparams("parallel", "parallel"),
        name="fft_stage1_inv",
    )(b.reshape(2, nf, groups, FFT_SLAB, cols), tabs["m1_inv"], rows4(u), rows4(x0), skip.reshape(1, D_HY))
    return out.reshape(n, cols)


def _hyena_fft(zh, bsz, conv_w, conv_b, filt, tabs, skip):
    ub, x0 = _hyena_pre(zh, conv_w, conv_b, bsz)
    kk = _hyena_filter_taps(zh.shape[0], *filt)
    spec = _fft_s2_filter(_fft_s1(kk, tabs), tabs)
    return _fft_s1_inv(_fft_mid(_fft_s1(ub, tabs), tabs, spec), tabs, ub, x0, skip)


def _hyena(zh, bsz, conv_w, conv_b, filt, dft, skip):
    cm, sm = dft
    ub, x0 = _hyena_pre(zh, conv_w, conv_b, bsz)
    kk = _hyena_filter_taps(zh.shape[0], *filt)
    kre, kim = _dft_filter(cm, sm, kk)
    yre, yim = _dft_fwd(cm, sm, ub, kre, kim)
    return _dft_inv(cm, sm, yre, yim, ub, x0, skip)


def _attend_block(q, kv, sink_ref, masks):
    k_low = [kv[:, g * LANES:(g + 1) * LANES] for g in range(N_KVH)]
    ones = jnp.ones((kv.shape[0], LANES), BF16)
    v_aug = [jnp.concatenate([kv[:, (N_KVH + g) * LANES:(N_KVH + g + 1) * LANES], ones], axis=1)
             for g in range(N_KVH)]
    nq = q.shape[0]
    windowed = masks is not None
    if windowed:
        valid_prev, valid_next = masks
    cols_per_g = D_ATTN // LANES // N_KVH
    out_cols = []
    for g in range(N_KVH):
        cols = [q[:, c * LANES:(c + 1) * LANES] for c in range(g * cols_per_g, (g + 1) * cols_per_g)]
        qg = jnp.concatenate(cols + [pltpu.roll(cq, HEAD_DIM, axis=1) for cq in cols], axis=0).astype(BF16)
        s_all = lax.dot_general(qg, k_low[g], (((1,), (1,)), ((), ())), preferred_element_type=F32)
        es, sinks = [], []
        for hb in range(2 * cols_per_g):
            h = 2 * (g * cols_per_g + hb % cols_per_g) + hb // cols_per_g
            s = s_all[hb * nq:(hb + 1) * nq]
            if windowed:
                s = jnp.concatenate([
                    jnp.where(valid_prev, s[:, :ATT_BLOCK], NEG_INF),
                    s[:, ATT_BLOCK:2 * ATT_BLOCK],
                    jnp.where(valid_next, s[:, 2 * ATT_BLOCK:3 * ATT_BLOCK], NEG_INF),
                    s[:, 3 * ATT_BLOCK:]], axis=1)
            sk = sink_ref[h:h + 1, 0:1] * LOG2E
            m = jnp.maximum(jnp.max(s, axis=-1, keepdims=True), sk)
            es.append(jnp.exp2(s - m).astype(BF16))
            sinks.append(jnp.exp2(sk - m))
        o_all = jnp.dot(jnp.concatenate(es, axis=0), v_aug[g], preferred_element_type=F32)
        outs = []
        for hb in range(2 * cols_per_g):
            o = o_all[hb * nq:(hb + 1) * nq]
            outs.append(o[:, :LANES] / (o[:, LANES:] + sinks[hb]))
        out_cols += [outs[ci] + pltpu.roll(outs[cols_per_g + ci], HEAD_DIM, axis=1) for ci in range(cols_per_g)]
    return out_cols


def _attn_kernel(*refs, windowed, n_sub):
    if windowed:
        (q_ref, kvp_ref, kvc_ref, kvn_ref, kvx_ref, sink_ref, o_ref) = refs
        i = pl.program_id(1)
        last = pl.num_programs(1) - 1
        r = lax.broadcasted_iota(jnp.int32, (ATT_BLOCK, ATT_BLOCK), 0)
        j = lax.broadcasted_iota(jnp.int32, (ATT_BLOCK, ATT_BLOCK), 1)
        cur = kvc_ref[0]
        blocks = ([kvp_ref[0]] + [cur[n * ATT_BLOCK:(n + 1) * ATT_BLOCK] for n in range(n_sub)]
                  + [kvn_ref[0]])
    else:
        (q_ref, kvx_ref, sink_ref, o_ref) = refs
    ctx = kvx_ref[0]
    for n in range(n_sub):
        rows = slice(n * ATT_BLOCK, (n + 1) * ATT_BLOCK)
        q = q_ref[0, rows, :].astype(F32)
        if windowed:
            no_prev = jnp.where(i == 0, ATT_BLOCK, 0) if n == 0 else 0
            no_next = jnp.where(i == last, ATT_BLOCK, 0) if n == n_sub - 1 else 0
            masks = (j >= r + no_prev, j <= r - no_next)
            kv = jnp.concatenate(blocks[n:n + 3] + [ctx], axis=0)
        else:
            masks = None
            kv = ctx
        for c, col in enumerate(_attend_block(q, kv, sink_ref, masks)):
            o_ref[0, rows, c * LANES:(c + 1) * LANES] = col.astype(BF16)


def _attention(q, kv, kvx, sink, windowed):
    bsz, lq, _ = q.shape
    n_ctx, kv_w = kvx.shape[1:]
    nb = lq // ATT_BLOCK
    n_sub = ATT_STEP if nb % ATT_STEP == 0 else 1
    step = n_sub * ATT_BLOCK
    sink_t = jnp.broadcast_to(sink.reshape(N_QH, 1), (N_QH, LANES))
    qspec = pl.BlockSpec((1, step, D_ATTN), lambda b, i: (b, i, 0))
    xspec = pl.BlockSpec((1, n_ctx, kv_w), lambda b, i: (b, 0, 0))
    sspec = pl.BlockSpec((N_QH, LANES), lambda b, i: (0, 0))
    if windowed:
        prev = pl.BlockSpec((1, ATT_BLOCK, kv_w), lambda b, i: (b, jnp.maximum(n_sub * i - 1, 0), 0))
        cur = pl.BlockSpec((1, step, kv_w), lambda b, i: (b, i, 0))
        nxt = pl.BlockSpec((1, ATT_BLOCK, kv_w), lambda b, i: (b, jnp.minimum(n_sub * (i + 1), nb - 1), 0))
        in_specs = [qspec, prev, cur, nxt, xspec, sspec]
        args = (q, kv, kv, kv, kvx, sink_t)
    else:
        in_specs = [qspec, xspec, sspec]
        args = (q, kvx, sink_t)
    return pl.pallas_call(
        functools.partial(_attn_kernel, windowed=windowed, n_sub=n_sub),
        grid=(bsz, lq // step),
        in_specs=in_specs,
        out_specs=pl.BlockSpec((1, step, D_ATTN), lambda b, i: (b, i, 0)),
        out_shape=jax.ShapeDtypeStruct(q.shape, BF16),
        compiler_params=_cparams("parallel", "parallel"),
        name="attention",
    )(*args)


def _rope_tables(n_lat):
    rows = n_lat // GRID_W
    r = jnp.repeat(jnp.arange(rows, dtype=F32), GRID_W)
    col = jnp.tile(jnp.arange(GRID_W, dtype=F32), rows)
    inv = ROPE_THETA ** (-jnp.arange(ROPE_PAIRS_AXIS, dtype=F32) / ROPE_PAIRS_AXIS)
    ang = jnp.concatenate([r[:, None] * inv, col[:, None] * inv], axis=-1)
    cos, sin = jnp.cos(ang), jnp.sin(ang)
    zero = jnp.zeros_like(sin)
    reps = D_ATTN // HEAD_DIM
    cos_t = jnp.tile(jnp.concatenate([cos, cos], axis=-1), (1, reps))
    sin_a = jnp.tile(jnp.concatenate([-sin, zero], axis=-1), (1, reps))
    sin_b = jnp.tile(jnp.concatenate([zero, sin], axis=-1), (1, reps))
    return cos_t, sin_a, sin_b


def kernel(x, c, ctx, c_ctx, w_mod, b_mod, norm_g, ffn_w1, ffn_w2, w_in, w_out, lru_conv_w, lru_conv_b,
           lru_wa, lru_ba, lru_wx, lru_bx, lru_lam, hy_conv_w, hy_conv_b, hy_fw0, hy_fb0, hy_fw_in,
           hy_fb_in, hy_freq, hy_fw_last, hy_skip, attn_sink, final_g):
    bsz, n_lat, d = x.shape
    n_ctx = ctx.shape[1]
    depth = w_mod.shape[0]
    assert n_lat % ATT_BLOCK == 0 and n_ctx % ATT_BLOCK == 0 and n_lat % GRID_W == 0
    assert ATT_BLOCK % SCAN_GROUP == 0 and ATT_BLOCK % LANES == 0

    mod_rows = -(-(bsz + 1) // (2 * SUBLANES)) * (2 * SUBLANES)
    c_rows = jnp.zeros((mod_rows, d), F32).at[:bsz].set(c).at[bsz].set(c_ctx)
    mod_all = _modulation(c_rows, w_mod, b_mod).reshape(depth, mod_rows, N_MOD, d)
    lat_row = lambda b: b
    ctx_row = lambda b: bsz

    rope_tabs = _rope_tables(n_lat)
    fft_lat = _fft_tables(n_lat)
    w1_b = ffn_w1.astype(BF16)
    w2_b = ffn_w2.astype(BF16)
    w_in_b = w_in.astype(BF16)
    w_out_b = w_out.astype(BF16)

    xc = ctx
    for l in range(depth):
        need_ctx = l < depth - 1
        mod = mod_all[l]
        filt = (hy_fw0[l], hy_fb0[l], hy_fw_in[l], hy_fb_in[l], hy_freq[l], hy_fw_last[l])

        x = _ffn(x, mod, lat_row, norm_g[l, 0], w1_b, w2_b, l, 0, 0)
        xc = _ffn(xc, mod, ctx_row, norm_g[l, 0], w1_b, w2_b, l, 0, 0)

        xl, gl, zh, q, kv = _input_proj(x, mod, lat_row, norm_g[l, 1], w_in_b, l, rope_tabs)
        xlc, glc, zhc, qc, kvc = _input_proj(xc, mod, ctx_row, norm_g[l, 1], w_in_b, l, None)

        y_lru, yc_lru = _rglru(xl, gl, xlc, glc, lru_conv_w[l], lru_conv_b[l], lru_wa[l], lru_ba[l],
                               lru_wx[l], lru_bx[l], lru_lam[l], bsz)
        y_hy = _hyena_fft(zh, bsz, hy_conv_w[l], hy_conv_b[l], filt, fft_lat, hy_skip[l])
        y_att = _attention(q, kv, kvc, attn_sink[l], True)
        x = _ffn(x, mod, lat_row, norm_g[l, 2], w1_b, w2_b, l, 1, 6, mixer=(y_lru, y_hy, y_att, w_out_b),
                 final_g=None if need_ctx else final_g)

        if need_ctx:
            yc_hy = _hyena(zhc, bsz, hy_conv_w[l], hy_conv_b[l], filt, _dft_matrices(n_ctx), hy_skip[l])
            yc_att = _attention(qc, None, kvc, attn_sink[l], False)
            xc = _ffn(xc, mod, ctx_row, norm_g[l, 2], w1_b, w2_b, l, 1, 6,
                      mixer=(yc_lru, yc_hy, yc_att, w_out_b))
    return x
```

```python
import functools
import math

import jax
import jax.numpy as jnp
from jax import lax
from jax.experimental import pallas as pl
from jax.experimental.pallas import tpu as pltpu

F32 = jnp.float32
BF16 = jnp.bfloat16

NORM_EPS = 1e-6
N_MOD = 9
MACARON_W = 0.5
D_LRU = 256
LRU_BLOCKS = 4
LRU_C = 8.0
LRU_LEFT = 2
D_HY = 256
HY_LEFT = 1
HY_EMB = 33
HY_BANDS = (HY_EMB - 1) // 2
HY_FAST = 0.3
HY_SLOW = 1.5
HY_TARGET = 1e-2
N_QH = 8
N_KVH = 2
HEAD_DIM = 64
D_ATTN = N_QH * HEAD_DIM
D_KV = N_KVH * HEAD_DIM
WINDOW = 128
ATT_BLOCK = 128
ATT_STEP = 2
GRID_W = 64
ROPE_THETA = 10000.0
ROPE_PAIRS_AXIS = HEAD_DIM // 4
NEG_INF = -1e30
LOG2E = math.log2(math.e)
F32_TINY = 1.1754944e-38

LANES = 128
SUBLANES = 8
VMEM_LIMIT_BYTES = 56 * 1024 * 1024
ROW_TILE = 512
FFN_SPLIT = 4
FFN_MIN_ROWS = 128
PROJ_SPLIT = 2
DFT_TILE = 1024
DFT_FILTER_TILE = 512
FFT_BLOCK = 128
FFT_SLAB = 16
FFT_COLS = 512
SCAN_GROUP = 64


def _cparams(*sem):
    return pltpu.CompilerParams(dimension_semantics=sem, vmem_limit_bytes=VMEM_LIMIT_BYTES)


def _row_tile(rows):
    return min(ROW_TILE, rows)


def _resident(shape, index=None):
    index = (0,) * len(shape) if index is None else index
    return pl.BlockSpec(shape, lambda *_: index, pipeline_mode=pl.Buffered(1))


def _ada_norm(x, g, shift, scale):
    y = x * lax.rsqrt(jnp.mean(x * x, axis=-1, keepdims=True) + NORM_EPS)
    return (y * g) * (1.0 + scale) + shift


def _sigmoid(x):
    return 0.5 * (1.0 + jnp.tanh(0.5 * x))


def _gelu_tanh(x):
    return 0.5 * x * (1.0 + jnp.tanh(math.sqrt(2.0 / math.pi) * (x + 0.044715 * (x * x * x))))


def _softplus(x):
    return jnp.maximum(x, 0.0) + jnp.log1p(jnp.exp(-jnp.abs(x)))


def _dwconv_tile(x_ref, w, bias, left, i, tile):
    n_rows, width = x_ref.shape
    n_tiles = n_rows // tile
    group = SUBLANES * 4 // x_ref.dtype.itemsize
    r0 = pl.multiple_of(i * tile, tile)
    zeros_i = jnp.zeros((group, width), jnp.int32)
    before = x_ref[pl.ds(pl.multiple_of(jnp.maximum(r0 - group, 0), group), group), :].astype(F32)
    after = x_ref[pl.ds(pl.multiple_of(jnp.minimum(r0 + tile, n_rows - group), group), group), :].astype(F32)
    before = jnp.where(zeros_i + i > 0, before, 0.0)
    after = jnp.where(zeros_i + i < n_tiles - 1, after, 0.0)
    ext = jnp.concatenate([before, x_ref[pl.ds(r0, tile), :].astype(F32), after], axis=0)
    out = jnp.broadcast_to(bias, (tile, width))
    for k in range(w.shape[0]):
        off = k - left
        sh = ext if off == 0 else pltpu.roll(ext, (-off) % ext.shape[0], axis=0)
        out = out + sh[group:group + tile] * w[k:k + 1, :]
    return out


def _mod_kernel(c_ref, w_ref, b_ref, o_ref):
    cv = c_ref[...]
    s = cv * _sigmoid(cv)
    rows = s.shape[0]
    s_hi = s.astype(BF16)
    s_lo = (s - s_hi.astype(F32)).astype(BF16)
    w = w_ref[0]
    w_hi = w.astype(BF16)
    w_lo = (w - w_hi.astype(F32)).astype(BF16)
    p = jnp.dot(jnp.concatenate([s_hi, s_lo], axis=0), w_hi, preferred_element_type=F32)
    o_ref[0] = p[:rows] + p[rows:] + jnp.dot(s_hi, w_lo, preferred_element_type=F32) + b_ref[0]


def _modulation(c_rows, w_mod, b_mod):
    depth, d, nd = w_mod.shape
    rows = c_rows.shape[0]
    tn = nd // 8
    return pl.pallas_call(
        _mod_kernel,
        grid=(depth, nd // tn),
        in_specs=[
            pl.BlockSpec((rows, d), lambda l, j: (0, 0)),
            pl.BlockSpec((1, d, tn), lambda l, j: (l, 0, j)),
            pl.BlockSpec((1, 1, tn), lambda l, j: (l, 0, j)),
        ],
        out_specs=pl.BlockSpec((1, rows, tn), lambda l, j: (l, 0, j)),
        out_shape=jax.ShapeDtypeStruct((depth, rows, nd), F32),
        compiler_params=_cparams("parallel", "parallel"),
        name="modulation",
    )(c_rows, w_mod, b_mod.reshape(depth, 1, nd))


def _ffn_kernel(*refs, i_mod, d_ff, mixer, final):
    refs = list(refs)
    o_ref = refs.pop()
    x_ref, mod_ref, g_ref, w1_ref, w2_ref = refs[:5]
    rest = refs[5:]
    m = mod_ref[...]
    g = g_ref[...]
    tm = x_ref.shape[1]
    n_sub = 1 if mixer else max(n for n in range(1, FFN_SPLIT + 1) if tm % (n * FFN_MIN_ROWS) == 0)
    for n in range(n_sub):
        rows = slice(n * tm // n_sub, (n + 1) * tm // n_sub)
        x = x_ref[0, rows, :]
        if mixer:
            yl_ref, yh_ref, ya_ref, wo_ref = rest[:4]
            y = (jnp.dot(yl_ref[rows, :].astype(BF16), wo_ref[0:D_LRU, :], preferred_element_type=F32)
                 + jnp.dot(yh_ref[rows, :].astype(BF16), wo_ref[D_LRU:D_LRU + D_HY, :], preferred_element_type=F32)
                 + jnp.dot(ya_ref[0, rows, :].astype(BF16), wo_ref[D_LRU + D_HY:, :], preferred_element_type=F32))
            x = x + m[5:6] * y
        h = _ada_norm(x, g, m[i_mod:i_mod + 1], m[i_mod + 1:i_mod + 2])
        ab = jnp.dot(h.astype(BF16), w1_ref[...], preferred_element_type=F32)
        a = ab[:, :d_ff]
        b = ab[:, d_ff:]
        gated = (a * _sigmoid(a)) * b
        y = jnp.dot(gated.astype(BF16), w2_ref[...], preferred_element_type=F32)
        x = x + (MACARON_W * m[i_mod + 2:i_mod + 3]) * y
        if final:
            x = (x * lax.rsqrt(jnp.mean(x * x, axis=-1, keepdims=True) + NORM_EPS)) * rest[-1][...]
        o_ref[0, rows, :] = x


def _ffn(x, mod, mod_row, g, w1, w2, layer, which, i_mod, mixer=None, final_g=None):
    bsz, rows, d = x.shape
    d_ff = w2.shape[2]
    tm = _row_tile(rows)
    in_specs = [
        pl.BlockSpec((1, tm, d), lambda b, t: (b, t, 0)),
        pl.BlockSpec((None, N_MOD, d), lambda b, t: (mod_row(b), 0, 0)),
        pl.BlockSpec((1, d), lambda b, t: (0, 0)),
        _resident((None, None, d, 2 * d_ff), (layer, which, 0, 0)),
        _resident((None, None, d_ff, d), (layer, which, 0, 0)),
    ]
    args = [x, mod, g.reshape(1, d), w1, w2]
    if mixer is not None:
        y_lru, y_hy, y_att, w_out = mixer
        in_specs += [
            pl.BlockSpec((tm, D_LRU), lambda b, t: (t, b)),
            pl.BlockSpec((tm, D_HY), lambda b, t: (t, b)),
            pl.BlockSpec((1, tm, D_ATTN), lambda b, t: (b, t, 0)),
            _resident((None,) + w_out.shape[1:], (layer, 0, 0)),
        ]
        args += [y_lru, y_hy, y_att, w_out]
    if final_g is not None:
        in_specs.append(pl.BlockSpec((1, d), lambda b, t: (0, 0)))
        args.append(final_g.reshape(1, d))
    return pl.pallas_call(
        functools.partial(_ffn_kernel, i_mod=i_mod, d_ff=d_ff, mixer=mixer is not None,
                          final=final_g is not None),
        grid=(bsz, rows // tm),
        in_specs=in_specs,
        out_specs=pl.BlockSpec((1, tm, d), lambda b, t: (b, t, 0)),
        out_shape=jax.ShapeDtypeStruct(x.shape, F32),
        compiler_params=_cparams("parallel", "parallel"),
        name="ffn",
    )(*args)


def _rope(x, cos_t, sin_a, sin_b):
    width = x.shape[-1]
    half = HEAD_DIM // 2
    up = pltpu.roll(x, width - half, axis=1)
    dn = pltpu.roll(x, half, axis=1)
    return x * cos_t + up * sin_a + dn * sin_b


def _proj_kernel(*refs, rope):
    if rope:
        (x_ref, mod_ref, g_ref, w_ref, cos_ref, sa_ref, sb_ref,
         xl_ref, gl_ref, zh_ref, q_ref, kv_ref) = refs
    else:
        (x_ref, mod_ref, g_ref, w_ref, xl_ref, gl_ref, zh_ref, q_ref, kv_ref) = refs
    m = mod_ref[...]
    g = g_ref[...]
    tm = x_ref.shape[1]
    n_sub = max(n for n in range(1, PROJ_SPLIT + 1) if tm % (n * FFN_MIN_ROWS) == 0)
    for n in range(n_sub):
        rows = slice(n * tm // n_sub, (n + 1) * tm // n_sub)
        h = _ada_norm(x_ref[0, rows, :], g, m[3:4], m[4:5])
        z = jnp.dot(h.astype(BF16), w_ref[...], preferred_element_type=F32)
        o = 0
        xl_ref[rows, :] = z[:, o:o + D_LRU].astype(BF16); o += D_LRU
        gl_ref[rows, :] = z[:, o:o + D_LRU].astype(BF16); o += D_LRU
        zh_ref[rows, :] = z[:, o:o + 3 * D_HY].astype(BF16); o += 3 * D_HY
        q = z[:, o:o + D_ATTN]; o += D_ATTN
        k = z[:, o:o + D_KV]; o += D_KV
        v = z[:, o:o + D_KV]
        if rope:
            cos_t, sin_a, sin_b = cos_ref[rows, :], sa_ref[rows, :], sb_ref[rows, :]
            q = _rope(q, cos_t, sin_a, sin_b)
            k = _rope(k, cos_t[:, :D_KV], sin_a[:, :D_KV], sin_b[:, :D_KV])
        q_ref[0, rows, :] = (q * (HEAD_DIM ** -0.5 * LOG2E)).astype(BF16)
        low = lax.broadcasted_iota(jnp.int32, k.shape, 1) < HEAD_DIM
        parts = [jnp.where(low, k, 0.0), jnp.where(low, pltpu.roll(k, HEAD_DIM, axis=1), 0.0),
                 jnp.where(low, v, 0.0), jnp.where(low, pltpu.roll(v, HEAD_DIM, axis=1), 0.0)]
        kv_ref[0, rows, :] = jnp.concatenate(parts, axis=1).astype(BF16)


def _input_proj(x, mod, mod_row, g, w_in, layer, rope_tabs):
    bsz, rows, d = x.shape
    tm = _row_tile(rows)
    rope = rope_tabs is not None
    in_specs = [
        pl.BlockSpec((1, tm, d), lambda t, b: (b, t, 0)),
        pl.BlockSpec((None, N_MOD, d), lambda t, b: (mod_row(b), 0, 0)),
        pl.BlockSpec((1, d), lambda t, b: (0, 0)),
        _resident((None,) + w_in.shape[1:], (layer, 0, 0)),
    ]
    args = [x, mod, g.reshape(1, d), w_in]
    if rope:
        in_specs += [pl.BlockSpec((tm, D_ATTN), lambda t, b: (t, 0))] * 3
        args += list(rope_tabs)
    out_shape = [
        jax.ShapeDtypeStruct((rows, bsz * D_LRU), BF16),
        jax.ShapeDtypeStruct((rows, bsz * D_LRU), BF16),
        jax.ShapeDtypeStruct((rows, bsz * 3 * D_HY), BF16),
        jax.ShapeDtypeStruct((bsz, rows, D_ATTN), BF16),
        jax.ShapeDtypeStruct((bsz, rows, 2 * N_KVH * LANES), BF16),
    ]
    out_specs = [
        pl.BlockSpec((tm, D_LRU), lambda t, b: (t, b)),
        pl.BlockSpec((tm, D_LRU), lambda t, b: (t, b)),
        pl.BlockSpec((tm, 3 * D_HY), lambda t, b: (t, b)),
        pl.BlockSpec((1, tm, D_ATTN), lambda t, b: (b, t, 0)),
        pl.BlockSpec((1, tm, 2 * N_KVH * LANES), lambda t, b: (b, t, 0)),
    ]
    return pl.pallas_call(
        functools.partial(_proj_kernel, rope=rope),
        grid=(rows // tm, bsz),
        in_specs=in_specs,
        out_specs=out_specs,
        out_shape=out_shape,
        compiler_params=_cparams("parallel", "parallel"),
        name="input_proj",
    )(*args)


def _scan_chunk(a, b, row, reverse):
    for s in (1, 2, 4):
        if reverse:
            ok = row < SUBLANES - s
            sh = SUBLANES - s
        else:
            ok = row >= s
            sh = s
        a_sh = pltpu.roll(a, sh, axis=0)
        b_sh = pltpu.roll(b, sh, axis=0)
        b = jnp.where(ok, a * b_sh + b, b)
        a = jnp.where(ok, a * a_sh, a)
    return a, b


def _lru_kernel(xl_ref, gl_ref, xc_ref, gc_ref, cw_ref, cb_ref, wa_ref, ba_ref, wx_ref, bx_ref,
                lam_ref, y_ref, yc_ref, a_s, b_s, *, n_lat, n_ctx, tile):
    width = xl_ref.shape[1]
    cw = cw_ref[...]
    cb = cb_ref[...]
    neg_c = [(0.5 * LRU_C) * _softplus(-lam_ref[d]) for d in range(2)]

    def coeffs(x_ref, base, n_rows):
        def body(i, carry):
            r0 = pl.multiple_of(i * tile, tile)
            u = _dwconv_tile(x_ref, cw, cb, LRU_LEFT, i, tile)
            ub = u.astype(BF16)
            hu = 0.5 * u
            for d in range(2):
                t_a = jnp.tanh(jnp.dot(ub, wa_ref[d], preferred_element_type=F32) + ba_ref[d])
                t_x = jnp.tanh(jnp.dot(ub, wx_ref[d], preferred_element_type=F32) + bx_ref[d])
                neg_log_a = neg_c[d] + neg_c[d] * t_a
                a = jnp.exp(-neg_log_a)
                dst = pl.ds(pl.multiple_of(base + r0, SUBLANES), tile)
                a_s[d, dst, :] = a
                var = jnp.tanh(neg_log_a) * (1.0 + a * a)
                b_s[d, dst, :] = (var * lax.rsqrt(jnp.maximum(var, F32_TINY))) * (hu + hu * t_x)
            return carry
        lax.fori_loop(0, n_rows // tile, body, 0)

    coeffs(xc_ref, 0, n_ctx)
    coeffs(xl_ref, n_ctx, n_lat)

    row = lax.broadcasted_iota(jnp.int32, (SUBLANES, width), 0)

    def scan_group(d, group, h, reverse):
        r0 = pl.multiple_of(group * SCAN_GROUP, SCAN_GROUP)
        a = a_s[d, pl.ds(r0, SCAN_GROUP), :]
        b = b_s[d, pl.ds(r0, SCAN_GROUP), :]
        order = range(SCAN_GROUP // SUBLANES)
        parts = [_scan_chunk(a[c * SUBLANES:(c + 1) * SUBLANES], b[c * SUBLANES:(c + 1) * SUBLANES], row, reverse)
                 for c in order]
        for c in (reversed(order) if reverse else order):
            hh = parts[c][0] * h + parts[c][1]
            b_s[d, pl.ds(r0 + c * SUBLANES, SUBLANES), :] = hh
            h = hh[0:1, :] if reverse else hh[SUBLANES - 1:SUBLANES, :]
        return h

    ng_ctx = n_ctx // SCAN_GROUP
    ng_all = (n_ctx + n_lat) // SCAN_GROUP
    h0 = jnp.zeros((1, width), F32)

    def ctx_body(j, hs):
        return (scan_group(0, j, hs[0], False), scan_group(1, ng_ctx - 1 - j, hs[1], True))

    def lat_body(j, hs):
        return (scan_group(0, ng_ctx + j, hs[0], False), scan_group(1, ng_all - 1 - j, hs[1], True))

    hs = lax.fori_loop(0, ng_ctx, ctx_body, (h0, h0))
    lax.fori_loop(0, ng_all - ng_ctx, lat_body, hs)

    def finish(o_ref, g_ref, base, n_rows):
        def body(i, carry):
            r0 = pl.multiple_of(i * tile, tile)
            src = pl.ds(pl.multiple_of(base + r0, SUBLANES), tile)
            hsum = b_s[0, src, :] + b_s[1, src, :]
            o_ref[pl.ds(r0, tile), :] = (hsum * _gelu_tanh(g_ref[pl.ds(r0, tile), :].astype(F32))).astype(BF16)
            return carry
        lax.fori_loop(0, n_rows // tile, body, 0)

    finish(yc_ref, gc_ref, 0, n_ctx)
    finish(y_ref, gl_ref, n_ctx, n_lat)


def _block_diag(w):
    two, nb, bs, _ = w.shape
    eye = jnp.eye(nb, dtype=w.dtype)
    return jnp.einsum('dnij,nm->dnimj', w, eye).reshape(two, nb * bs, nb * bs)


def _rglru(xl, gl, xlc, glc, conv_w, conv_b, wa, ba, wx, bx, lam, bsz):
    n_lat, n_ctx = xl.shape[0], xlc.shape[0]
    width = LANES
    per_b = D_LRU // width
    tile = math.gcd(256, math.gcd(n_lat, n_ctx))
    wa_bd = (0.5 * _block_diag(wa)).astype(BF16)
    wx_bd = (0.5 * _block_diag(wx)).astype(BF16)
    ba = 0.5 * ba
    bx = 0.5 * bx
    col = lambda b, j: (0, b * per_b + j)
    par = lambda b, j: (0, j)
    par3 = lambda b, j: (0, 0, j)
    return pl.pallas_call(
        functools.partial(_lru_kernel, n_lat=n_lat, n_ctx=n_ctx, tile=tile),
        grid=(bsz, per_b),
        in_specs=[
            pl.BlockSpec((n_lat, width), col),
            pl.BlockSpec((n_lat, width), col),
            pl.BlockSpec((n_ctx, width), col),
            pl.BlockSpec((n_ctx, width), col),
            pl.BlockSpec((conv_w.shape[0], width), par),
            pl.BlockSpec((1, width), par),
            pl.BlockSpec((2, width, width), lambda b, j: (0, j, j)),
            pl.BlockSpec((2, 1, width), par3),
            pl.BlockSpec((2, width, width), lambda b, j: (0, j, j)),
            pl.BlockSpec((2, 1, width), par3),
            pl.BlockSpec((2, 1, width), par3),
        ],
        out_specs=[pl.BlockSpec((n_lat, width), col), pl.BlockSpec((n_ctx, width), col)],
        out_shape=[jax.ShapeDtypeStruct(xl.shape, BF16), jax.ShapeDtypeStruct(xlc.shape, BF16)],
        scratch_shapes=[pltpu.VMEM((2, n_ctx + n_lat, width), F32),
                        pltpu.VMEM((2, n_ctx + n_lat, width), F32)],
        compiler_params=_cparams("parallel", "parallel"),
        name="rglru",
    )(xl, gl, xlc, glc, conv_w, conv_b.reshape(1, -1), wa_bd, ba.reshape(2, 1, -1), wx_bd,
      bx.reshape(2, 1, -1), lam.reshape(2, 1, -1))


def _hy_pre_kernel(z0_ref, z1_ref, z2_ref, w0_ref, w1_ref, w2_ref, b0_ref, b1_ref, b2_ref,
                   ub_ref, x0_ref, *, tile):
    w0, w1, w2 = w0_ref[...], w1_ref[...], w2_ref[...]
    b0, b1, b2 = b0_ref[...], b1_ref[...], b2_ref[...]

    def body(i, carry):
        rows = pl.ds(pl.multiple_of(i * tile, tile), tile)
        x1 = _dwconv_tile(z1_ref, w1, b1, HY_LEFT, i, tile)
        v = _dwconv_tile(z2_ref, w2, b2, HY_LEFT, i, tile)
        ub_ref[rows, :] = (x1 * v).astype(BF16)
        x0_ref[rows, :] = _dwconv_tile(z0_ref, w0, b0, HY_LEFT, i, tile).astype(BF16)
        return carry
    lax.fori_loop(0, z0_ref.shape[0] // tile, body, 0)


def _hyena_pre(zh, conv_w, conv_b, bsz):
    rows = zh.shape[0]
    width = LANES
    per_b = D_HY // width
    tile = math.gcd(256, rows)
    zspec = lambda part: pl.BlockSpec((rows, width), lambda b, j: (0, b * 3 * per_b + part * per_b + j))
    wspec = lambda part: pl.BlockSpec((conv_w.shape[0], width), lambda b, j: (0, part * per_b + j))
    bspec = lambda part: pl.BlockSpec((1, width), lambda b, j: (0, part * per_b + j))
    ospec = pl.BlockSpec((rows, width), lambda b, j: (0, b * per_b + j))
    cb = conv_b.reshape(1, -1)
    return pl.pallas_call(
        functools.partial(_hy_pre_kernel, tile=tile),
        grid=(bsz, per_b),
        in_specs=[zspec(0), zspec(1), zspec(2), wspec(0), wspec(1), wspec(2), bspec(0), bspec(1), bspec(2)],
        out_specs=[ospec, ospec],
        out_shape=[jax.ShapeDtypeStruct((rows, bsz * D_HY), BF16),
                   jax.ShapeDtypeStruct((rows, bsz * D_HY), BF16)],
        compiler_params=_cparams("parallel", "parallel"),
        name="hyena_pre",
    )(zh, zh, zh, conv_w, conv_w, conv_w, cb, cb, cb)


def _hy_filter_kernel(z_ref, fw0_ref, fb0_ref, fwin_ref, fbin_ref, freq_ref, fwl_ref, dl_ref, o_ref, *, tile):
    hp = lax.Precision.HIGHEST
    z = z_ref[...]
    fr = freq_ref[...]
    hdn = jnp.sin(fr * (jnp.dot(z, fw0_ref[...], preferred_element_type=F32, precision=hp) + fb0_ref[...]))
    for j in range(fwin_ref.shape[0]):
        hdn = jnp.sin(fr * (jnp.dot(hdn, fwin_ref[j], preferred_element_type=F32, precision=hp) + fbin_ref[j]))
    k = jnp.dot(hdn, fwl_ref[...], preferred_element_type=F32, precision=hp)
    decay = jnp.exp(-z[:, 0:1] * dl_ref[...])
    k_fwd = k[:, :D_HY] * decay
    k_bwd = k[:, D_HY:] * decay
    row = lax.broadcasted_iota(jnp.int32, k_bwd.shape, 0) + pl.program_id(0) * tile
    k_bwd = jnp.where(row == 0, 0.0, k_bwd)
    o_ref[...] = jnp.concatenate([k_fwd, k_bwd], axis=-1).astype(BF16)


def _hyena_filter_taps(n, fw0, fb0, fw_in, fb_in, freq, fw_last):
    t = jnp.linspace(0.0, 1.0, n, dtype=F32)[:, None]
    w = 2.0 * math.pi * jnp.arange(n, dtype=F32)[:, None] / n
    f = jnp.linspace(1e-4, HY_BANDS - 1, HY_BANDS, dtype=F32)[None, :]
    z = jnp.concatenate([t, jnp.cos(f * w), -jnp.sin(f * w)], axis=-1)
    z = jnp.pad(z, ((0, 0), (0, LANES - HY_EMB)))
    fw0p = jnp.pad(fw0, ((0, LANES - HY_EMB), (0, 0)))
    max_decay = math.log(HY_TARGET) / HY_FAST
    min_decay = math.log(HY_TARGET) / HY_SLOW
    deltas = jnp.abs(jnp.linspace(min_decay, max_decay, D_HY, dtype=F32))[None, :]
    hid = fw0.shape[1]
    tile = min(512, n)
    full = lambda a: pl.BlockSpec(a.shape, lambda i: (0,) * a.ndim)
    args = [fw0p, fb0.reshape(1, hid), fw_in, fb_in.reshape(-1, 1, hid), freq.reshape(1, hid), fw_last, deltas]
    return pl.pallas_call(
        functools.partial(_hy_filter_kernel, tile=tile),
        grid=(n // tile,),
        in_specs=[pl.BlockSpec((tile, LANES), lambda i: (i, 0))] + [full(a) for a in args],
        out_specs=pl.BlockSpec((tile, 2 * D_HY), lambda i: (i, 0)),
        out_shape=jax.ShapeDtypeStruct((n, 2 * D_HY), BF16),
        compiler_params=_cparams("parallel"),
        name="hyena_filter",
    )(z, *args)


def _dft_expand_kernel(ca_ref, sa_ref, cb_ref, sb_ref, cm_ref, sm_ref, *, tile):
    ca, sa, cb, sb = ca_ref[...], sa_ref[...], cb_ref[...], sb_ref[...]
    row = lax.broadcasted_iota(jnp.int32, cb.shape, 0) + pl.program_id(0) * tile
    lane = lax.broadcasted_iota(jnp.int32, cb.shape, 1)
    alt = jnp.where(lane % 2 == 0, 1.0, -1.0)
    for s1 in range(cm_ref.shape[1] // LANES):
        c1 = ca[:, s1:s1 + 1]
        d1 = sa[:, s1:s1 + 1]
        cols = slice(s1 * LANES, (s1 + 1) * LANES)
        cm_ref[:, cols] = (c1 * cb - d1 * sb).astype(BF16)
        sm_ref[:, cols] = jnp.where(row == 0, alt, -(d1 * cb + c1 * sb)).astype(BF16)


def _dft_matrices(n):
    two_n = 2 * n
    f = jnp.arange(n, dtype=jnp.int32)[:, None]
    s_hi = jnp.arange(n // LANES, dtype=jnp.int32)[None, :] * LANES
    s_lo = jnp.arange(LANES, dtype=jnp.int32)[None, :]
    ang_a = ((f * s_hi) % two_n).astype(F32) * (2.0 * math.pi / two_n)
    ang_b = ((f * s_lo) % two_n).astype(F32) * (2.0 * math.pi / two_n)
    tile = min(DFT_TILE, n)
    hi_spec = pl.BlockSpec((tile, n // LANES), lambda i: (i, 0))
    lo_spec = pl.BlockSpec((tile, LANES), lambda i: (i, 0))
    return pl.pallas_call(
        functools.partial(_dft_expand_kernel, tile=tile),
        grid=(n // tile,),
        in_specs=[hi_spec, hi_spec, lo_spec, lo_spec],
        out_specs=[pl.BlockSpec((tile, n), lambda i: (i, 0))] * 2,
        out_shape=[jax.ShapeDtypeStruct((n, n), BF16)] * 2,
        compiler_params=_cparams("parallel"),
        name="dft_matrices",
    )(jnp.cos(ang_a), jnp.sin(ang_a), jnp.cos(ang_b), jnp.sin(ang_b))


def _dft_filter_kernel(cm_ref, sm_ref, kk_ref, kre_ref, kim_ref, *, tile):
    kk = kk_ref[...]
    xre = jnp.dot(cm_ref[...], kk, preferred_element_type=F32)
    xim = jnp.dot(sm_ref[...], kk, preferred_element_type=F32)
    row = lax.broadcasted_iota(jnp.int32, (tile, D_HY), 0) + pl.program_id(0) * tile
    kre_ref[...] = xre[:, :D_HY] + xre[:, D_HY:]
    kim_ref[...] = jnp.where(row == 0, xim[:, :D_HY] + xim[:, D_HY:], xim[:, :D_HY] - xim[:, D_HY:])


def _dft_filter(cm, sm, kk):
    n = cm.shape[0]
    tile = min(DFT_FILTER_TILE, n)
    return pl.pallas_call(
        functools.partial(_dft_filter_kernel, tile=tile),
        grid=(n // tile,),
        in_specs=[pl.BlockSpec((tile, n), lambda f: (f, 0)),
                  pl.BlockSpec((tile, n), lambda f: (f, 0)),
                  _resident(kk.shape)],
        out_specs=[pl.BlockSpec((tile, D_HY), lambda f: (f, 0))] * 2,
        out_shape=[jax.ShapeDtypeStruct((n, D_HY), F32)] * 2,
        compiler_params=_cparams("parallel"),
        name="dft_filter",
    )(cm, sm, kk)


def _dft_fwd_kernel(cm_ref, sm_ref, u_ref, kre_ref, kim_ref, yre_ref, yim_ref, *, tile):
    n_freq = cm_ref.shape[1]
    u = u_ref[...]
    xre = jnp.dot(cm_ref[...], u, preferred_element_type=F32)
    xim = jnp.dot(sm_ref[...], u, preferred_element_type=F32)
    kre = kre_ref[...]
    kim = kim_ref[...]
    row = lax.broadcasted_iota(jnp.int32, xre.shape, 0) + pl.program_id(0) * tile
    first = row == 0
    scale = jnp.where(first, 0.5 / n_freq, 1.0 / n_freq)
    yre_ref[...] = (scale * (xre * kre - jnp.where(first, 0.0, xim * kim))).astype(BF16)
    yim_ref[...] = (scale * jnp.where(first, xim * kim, xre * kim + xim * kre)).astype(BF16)


def _dft_fwd(cm, sm, ub, kre, kim):
    n = cm.shape[0]
    cols = ub.shape[1]
    tile = min(DFT_TILE, n)
    return pl.pallas_call(
        functools.partial(_dft_fwd_kernel, tile=tile),
        grid=(n // tile, cols // D_HY),
        in_specs=[pl.BlockSpec((tile, n), lambda f, c: (f, 0)),
                  pl.BlockSpec((tile, n), lambda f, c: (f, 0)),
                  pl.BlockSpec((n, D_HY), lambda f, c: (0, c)),
                  pl.BlockSpec((tile, D_HY), lambda f, c: (f, 0)),
                  pl.BlockSpec((tile, D_HY), lambda f, c: (f, 0))],
        out_specs=[pl.BlockSpec((tile, D_HY), lambda f, c: (f, c))] * 2,
        out_shape=[jax.ShapeDtypeStruct((n, cols), BF16)] * 2,
        compiler_params=_cparams("parallel", "parallel"),
        name="dft_fwd",
    )(cm, sm, ub, kre, kim)


def _dft_inv_kernel(ci_ref, si_ref, yre_ref, yim_ref, u_ref, x0_ref, skip_ref, o_ref, *, tile):
    y_cos = jnp.dot(ci_ref[...], yre_ref[...], preferred_element_type=F32)
    y_sin = jnp.dot(si_ref[...], yim_ref[...], preferred_element_type=F32)
    row = lax.broadcasted_iota(jnp.int32, y_cos.shape, 0) + pl.program_id(0) * tile
    nyq = jnp.where(row % 2 == 0, 1.0, -1.0) * yim_ref[0:1, :].astype(F32)
    y = y_cos + jnp.where(row == 0, 0.0, y_sin) + nyq
    o_ref[...] = (x0_ref[...] * (y + u_ref[...] * skip_ref[...])).astype(BF16)


def _dft_inv(ci, si, yre, yim, u, x0, skip):
    n = ci.shape[0]
    cols = yre.shape[1]
    tile = min(DFT_TILE, n)
    return pl.pallas_call(
        functools.partial(_dft_inv_kernel, tile=tile),
        grid=(n // tile, cols // D_HY),
        in_specs=[pl.BlockSpec((tile, n), lambda t, c: (t, 0)),
                  pl.BlockSpec((tile, n), lambda t, c: (t, 0)),
                  pl.BlockSpec((n, D_HY), lambda t, c: (0, c)),
                  pl.BlockSpec((n, D_HY), lambda t, c: (0, c)),
                  pl.BlockSpec((tile, D_HY), lambda t, c: (t, c)),
                  pl.BlockSpec((tile, D_HY), lambda t, c: (t, c)),
                  pl.BlockSpec((1, D_HY), lambda t, c: (0, 0))],
        out_specs=pl.BlockSpec((tile, D_HY), lambda t, c: (t, c)),
        out_shape=jax.ShapeDtypeStruct((n, cols), BF16),
        compiler_params=_cparams("parallel", "parallel"),
        name="dft_inv",
    )(ci, si, yre, yim, u, x0, skip.reshape(1, D_HY))


def _fft_tables(n):
    n1 = 2 * n // FFT_BLOCK
    half = n1 // 2
    nf = half + 1
    groups = FFT_BLOCK // FFT_SLAB
    f1 = jnp.arange(nf, dtype=F32)[:, None]
    s1 = jnp.arange(half, dtype=F32)[None, :]
    ang1 = (2.0 * math.pi / n1) * f1 * s1
    eye = jnp.eye(FFT_SLAB, dtype=F32)
    m1 = jnp.concatenate([jnp.kron(jnp.cos(ang1), eye), jnp.kron(-jnp.sin(ang1), eye)], axis=0)
    m1_inv = jnp.concatenate([jnp.kron(jnp.cos(ang1).T, eye), jnp.kron(-jnp.sin(ang1).T, eye)], axis=1)
    k = jnp.arange(FFT_BLOCK, dtype=F32)
    ang2 = (2.0 * math.pi / FFT_BLOCK) * k[:, None] * k[None, :]
    c2, d2 = jnp.cos(ang2), jnp.sin(ang2)
    g2 = jnp.block([[c2, d2], [-d2, c2]])
    g2_inv = jnp.block([[c2, -d2], [d2, c2]])
    psi = (2.0 * math.pi / (2 * n)) * f1 * k[None, :]
    tw = jnp.stack([jnp.cos(psi), jnp.sin(psi)])
    tw_s2 = jnp.broadcast_to(tw[..., None], (2, nf, FFT_BLOCK, LANES))
    tw_s1 = tw.reshape(2, nf, groups, FFT_SLAB).transpose(2, 0, 1, 3).reshape(groups, 2, nf * FFT_SLAB)
    tw_s1 = jnp.broadcast_to(tw_s1[..., None], (groups, 2, nf * FFT_SLAB, LANES))
    return dict(n=n, half=half, nf=nf, m1=m1.astype(BF16), m1_inv=m1_inv.astype(BF16), g2=g2.astype(BF16),
                g2_inv=g2_inv.astype(BF16), tw_s1=tw_s1, tw_s2=tw_s2)


def _lane_tile(x, width):
    return jnp.concatenate([x] * (width // x.shape[-1]), axis=-1) if width != x.shape[-1] else x


def _fft_s1_kernel(u_ref, m_ref, tw_ref, o_ref):
    h, slab, tc = u_ref.shape
    r = jnp.dot(m_ref[...], u_ref[...].reshape(h * slab, tc), preferred_element_type=F32)
    rows = r.shape[0] // 2
    re, im = r[:rows], r[rows:]
    cs = _lane_tile(tw_ref[0], tc)
    sn = _lane_tile(tw_ref[1], tc)
    o_ref[0] = (re * cs + im * sn).astype(BF16).reshape(rows // slab, slab, tc)
    o_ref[1] = (im * cs - re * sn).astype(BF16).reshape(rows // slab, slab, tc)


def _fft_s1(ub, tabs):
    n, cols = ub.shape
    half, nf = tabs["half"], tabs["nf"]
    groups = FFT_BLOCK // FFT_SLAB
    tc = min(FFT_COLS, cols)
    out = pl.pallas_call(
        _fft_s1_kernel,
        grid=(groups, cols // tc),
        in_specs=[pl.BlockSpec((half, None, FFT_SLAB, tc), lambda m, c: (0, m, 0, c)),
                  _resident(tabs["m1"].shape),
                  pl.BlockSpec((None, 2, nf * FFT_SLAB, LANES), lambda m, c: (m, 0, 0, 0))],
        out_specs=pl.BlockSpec((2, nf, None, FFT_SLAB, tc), lambda m, c: (0, 0, m, 0, c)),
        out_shape=jax.ShapeDtypeStruct((2, nf, groups, FFT_SLAB, cols), BF16),
        compiler_params=_cparams("parallel", "parallel"),
        name="fft_stage1",
    )(ub.reshape(half, groups, FFT_SLAB, cols), tabs["m1"], tabs["tw_s1"])
    return out.reshape(2, nf, FFT_BLOCK, cols)


def _fft_mid_kernel(a_ref, g_ref, gi_ref, k_ref, tw_ref, b_ref, *, gf, half, scale):
    g2 = g_ref[...]
    g2_inv = gi_ref[...]
    tc = a_ref.shape[-1]
    for j in range(gf):
        x = jnp.dot(g2, jnp.concatenate([a_ref[0, j], a_ref[1, j]], axis=0), preferred_element_type=F32)
        xre, xim = x[:FFT_BLOCK], x[FFT_BLOCK:]
        kre, kim = k_ref[0, j], k_ref[1, j]
        f1 = pl.program_id(0) * gf + j
        w = jnp.where(jnp.logical_or(f1 == 0, f1 == half), scale, 2.0 * scale)
        y = jnp.concatenate([w * (xre * kre - xim * kim), w * (xre * kim + xim * kre)], axis=0).astype(BF16)
        b = jnp.dot(g2_inv, y, preferred_element_type=F32)
        bre, bim = b[:FFT_BLOCK], b[FFT_BLOCK:]
        cs = _lane_tile(tw_ref[0, j], tc)
        sn = _lane_tile(tw_ref[1, j], tc)
        b_ref[0, j] = (bre * cs - bim * sn).astype(BF16)
        b_ref[1, j] = (bre * sn + bim * cs).astype(BF16)


def _fft_group(nf):
    return max(g for g in range(1, 12) if nf % g == 0)


def _fft_mid(a, tabs, spec):
    _, nf, _, cols = a.shape
    gf = _fft_group(nf)
    blk = lambda w, idx: pl.BlockSpec((2, gf, FFT_BLOCK, w), idx)
    return pl.pallas_call(
        functools.partial(_fft_mid_kernel, gf=gf, half=tabs["half"], scale=0.5 / tabs["n"]),
        grid=(nf // gf, cols // D_HY),
        in_specs=[blk(D_HY, lambda g, c: (0, g, 0, c)), _resident(tabs["g2"].shape),
                  _resident(tabs["g2_inv"].shape), blk(D_HY, lambda g, c: (0, g, 0, 0)),
                  blk(LANES, lambda g, c: (0, g, 0, 0))],
        out_specs=blk(D_HY, lambda g, c: (0, g, 0, c)),
        out_shape=jax.ShapeDtypeStruct(a.shape, BF16),
        compiler_params=_cparams("parallel", "parallel"),
        name="fft_mid",
    )(a, tabs["g2"], tabs["g2_inv"], spec, tabs["tw_s2"])


def _fft_s2_filter_kernel(a_ref, g_ref, k_ref, *, gf):
    g2 = g_ref[...]
    for j in range(gf):
        x = jnp.dot(g2, jnp.concatenate([a_ref[0, j], a_ref[1, j]], axis=0), preferred_element_type=F32)
        xre, xim = x[:FFT_BLOCK], x[FFT_BLOCK:]
        k_ref[0, j] = xre[:, :D_HY] + xre[:, D_HY:]
        k_ref[1, j] = xim[:, :D_HY] - xim[:, D_HY:]


def _fft_s2_filter(a, tabs):
    _, nf, _, cols = a.shape
    gf = _fft_group(nf)
    return pl.pallas_call(
        functools.partial(_fft_s2_filter_kernel, gf=gf),
        grid=(nf // gf,),
        in_specs=[pl.BlockSpec((2, gf, FFT_BLOCK, cols), lambda g: (0, g, 0, 0)), _resident(tabs["g2"].shape)],
        out_specs=pl.BlockSpec((2, gf, FFT_BLOCK, D_HY), lambda g: (0, g, 0, 0)),
        out_shape=jax.ShapeDtypeStruct((2, nf, FFT_BLOCK, D_HY), F32),
        compiler_params=_cparams("parallel"),
        name="fft_stage2_filter",
    )(a, tabs["g2"])


def _fft_s1_inv_kernel(b_ref, m_ref, u_ref, x0_ref, skip_ref, o_ref):
    two, nf, slab, tc = b_ref.shape
    h = u_ref.shape[0]
    y = jnp.dot(m_ref[...], b_ref[...].reshape(two * nf * slab, tc), preferred_element_type=F32)
    u = u_ref[...].reshape(h * slab, tc).astype(F32)
    x0 = x0_ref[...].reshape(h * slab, tc).astype(F32)
    o_ref[...] = (x0 * (y + u * _lane_tile(skip_ref[...], tc))).astype(BF16).reshape(h, slab, tc)


def _fft_s1_inv(b, tabs, u, x0, skip):
    _, nf, _, cols = b.shape
    n, half = tabs["n"], tabs["half"]
    groups = FFT_BLOCK // FFT_SLAB
    tc = min(FFT_COLS, cols)
    rows4 = lambda a: a.reshape(half, groups, FFT_SLAB, cols)
    tspec = pl.BlockSpec((half, None, FFT_SLAB, tc), lambda m, c: (0, m, 0, c))
    out = pl.pallas_call(
        _fft_s1_inv_kernel,
        grid=(groups, cols // tc),
        in_specs=[pl.BlockSpec((2, nf, None, FFT_SLAB, tc), lambda m, c: (0, 0, m, 0, c)),
                  _resident(tabs["m1_inv"].shape), tspec, tspec,
                  pl.BlockSpec((1, D_HY), lambda m, c: (0, 0))],
        out_specs=tspec,
        out_shape=jax.ShapeDtypeStruct((half, groups, FFT_SLAB, cols), BF16),
        compiler_params=_cparams("parallel", "parallel"),
        name="fft_stage1_inv",
    )(b.reshape(2, nf, groups, FFT_SLAB, cols), tabs["m1_inv"], rows4(u), rows4(x0), skip.reshape(1, D_HY))
    return out.reshape(n, cols)


def _hyena_fft(zh, bsz, conv_w, conv_b, filt, tabs, skip):
    ub, x0 = _hyena_pre(zh, conv_w, conv_b, bsz)
    kk = _hyena_filter_taps(zh.shape[0], *filt)
    spec = _fft_s2_filter(_fft_s1(kk, tabs), tabs)
    return _fft_s1_inv(_fft_mid(_fft_s1(ub, tabs), tabs, spec), tabs, ub, x0, skip)


def _hyena(zh, bsz, conv_w, conv_b, filt, dft, skip):
    cm, sm = dft
    ub, x0 = _hyena_pre(zh, conv_w, conv_b, bsz)
    kk = _hyena_filter_taps(zh.shape[0], *filt)
    kre, kim = _dft_filter(cm, sm, kk)
    yre, yim = _dft_fwd(cm, sm, ub, kre, kim)
    return _dft_inv(cm, sm, yre, yim, ub, x0, skip)


def _attend_block(q, kv, sink_ref, masks):
    k_low = [kv[:, g * LANES:(g + 1) * LANES] for g in range(N_KVH)]
    ones = jnp.ones((kv.shape[0], LANES), BF16)
    v_aug = [jnp.concatenate([kv[:, (N_KVH + g) * LANES:(N_KVH + g + 1) * LANES], ones], axis=1)
             for g in range(N_KVH)]
    nq = q.shape[0]
    windowed = masks is not None
    if windowed:
        valid_prev, valid_next = masks
    cols_per_g = D_ATTN // LANES // N_KVH
    out_cols = []
    for g in range(N_KVH):
        cols = [q[:, c * LANES:(c + 1) * LANES] for c in range(g * cols_per_g, (g + 1) * cols_per_g)]
        qg = jnp.concatenate(cols + [pltpu.roll(cq, HEAD_DIM, axis=1) for cq in cols], axis=0).astype(BF16)
        s_all = lax.dot_general(qg, k_low[g], (((1,), (1,)), ((), ())), preferred_element_type=F32)
        es, sinks = [], []
        for hb in range(2 * cols_per_g):
            h = 2 * (g * cols_per_g + hb % cols_per_g) + hb // cols_per_g
            s = s_all[hb * nq:(hb + 1) * nq]
            if windowed:
                s = jnp.concatenate([
                    jnp.where(valid_prev, s[:, :ATT_BLOCK], NEG_INF),
                    s[:, ATT_BLOCK:2 * ATT_BLOCK],
                    jnp.where(valid_next, s[:, 2 * ATT_BLOCK:3 * ATT_BLOCK], NEG_INF),
                    s[:, 3 * ATT_BLOCK:]], axis=1)
            sk = sink_ref[h:h + 1, 0:1] * LOG2E
            m = jnp.maximum(jnp.max(s, axis=-1, keepdims=True), sk)
            es.append(jnp.exp2(s - m).astype(BF16))
            sinks.append(jnp.exp2(sk - m))
        o_all = jnp.dot(jnp.concatenate(es, axis=0), v_aug[g], preferred_element_type=F32)
        outs = []
        for hb in range(2 * cols_per_g):
            o = o_all[hb * nq:(hb + 1) * nq]
            outs.append(o[:, :LANES] / (o[:, LANES:] + sinks[hb]))
        out_cols += [outs[ci] + pltpu.roll(outs[cols_per_g + ci], HEAD_DIM, axis=1) for ci in range(cols_per_g)]
    return out_cols


def _attn_kernel(*refs, windowed, n_sub):
    if windowed:
        (q_ref, kvp_ref, kvc_ref, kvn_ref, kvx_ref, sink_ref, o_ref) = refs
        i = pl.program_id(1)
        last = pl.num_programs(1) - 1
        r = lax.broadcasted_iota(jnp.int32, (ATT_BLOCK, ATT_BLOCK), 0)
        j = lax.broadcasted_iota(jnp.int32, (ATT_BLOCK, ATT_BLOCK), 1)
        cur = kvc_ref[0]
        blocks = ([kvp_ref[0]] + [cur[n * ATT_BLOCK:(n + 1) * ATT_BLOCK] for n in range(n_sub)]
                  + [kvn_ref[0]])
    else:
        (q_ref, kvx_ref, sink_ref, o_ref) = refs
    ctx = kvx_ref[0]
    for n in range(n_sub):
        rows = slice(n * ATT_BLOCK, (n + 1) * ATT_BLOCK)
        q = q_ref[0, rows, :].astype(F32)
        if windowed:
            no_prev = jnp.where(i == 0, ATT_BLOCK, 0) if n == 0 else 0
            no_next = jnp.where(i == last, ATT_BLOCK, 0) if n == n_sub - 1 else 0
            masks = (j >= r + no_prev, j <= r - no_next)
            kv = jnp.concatenate(blocks[n:n + 3] + [ctx], axis=0)
        else:
            masks = None
            kv = ctx
        for c, col in enumerate(_attend_block(q, kv, sink_ref, masks)):
            o_ref[0, rows, c * LANES:(c + 1) * LANES] = col.astype(BF16)


def _attention(q, kv, kvx, sink, windowed):
    bsz, lq, _ = q.shape
    n_ctx, kv_w = kvx.shape[1:]
    nb = lq // ATT_BLOCK
    n_sub = ATT_STEP if nb % ATT_STEP == 0 else 1
    step = n_sub * ATT_BLOCK
    sink_t = jnp.broadcast_to(sink.reshape(N_QH, 1), (N_QH, LANES))
    qspec = pl.BlockSpec((1, step, D_ATTN), lambda b, i: (b, i, 0))
    xspec = pl.BlockSpec((1, n_ctx, kv_w), lambda b, i: (b, 0, 0))
    sspec = pl.BlockSpec((N_QH, LANES), lambda b, i: (0, 0))
    if windowed:
        prev = pl.BlockSpec((1, ATT_BLOCK, kv_w), lambda b, i: (b, jnp.maximum(n_sub * i - 1, 0), 0))
        cur = pl.BlockSpec((1, step, kv_w), lambda b, i: (b, i, 0))
        nxt = pl.BlockSpec((1, ATT_BLOCK, kv_w), lambda b, i: (b, jnp.minimum(n_sub * (i + 1), nb - 1), 0))
        in_specs = [qspec, prev, cur, nxt, xspec, sspec]
        args = (q, kv, kv, kv, kvx, sink_t)
    else:
        in_specs = [qspec, xspec, sspec]
        args = (q, kvx, sink_t)
    return pl.pallas_call(
        functools.partial(_attn_kernel, windowed=windowed, n_sub=n_sub),
        grid=(bsz, lq // step),
        in_specs=in_specs,
        out_specs=pl.BlockSpec((1, step, D_ATTN), lambda b, i: (b, i, 0)),
        out_shape=jax.ShapeDtypeStruct(q.shape, BF16),
        compiler_params=_cparams("parallel", "parallel"),
        name="attention",
    )(*args)


def _rope_tables(n_lat):
    rows = n_lat // GRID_W
    r = jnp.repeat(jnp.arange(rows, dtype=F32), GRID_W)
    col = jnp.tile(jnp.arange(GRID_W, dtype=F32), rows)
    inv = ROPE_THETA ** (-jnp.arange(ROPE_PAIRS_AXIS, dtype=F32) / ROPE_PAIRS_AXIS)
    ang = jnp.concatenate([r[:, None] * inv, col[:, None] * inv], axis=-1)
    cos, sin = jnp.cos(ang), jnp.sin(ang)
    zero = jnp.zeros_like(sin)
    reps = D_ATTN // HEAD_DIM
    cos_t = jnp.tile(jnp.concatenate([cos, cos], axis=-1), (1, reps))
    sin_a = jnp.tile(jnp.concatenate([-sin, zero], axis=-1), (1, reps))
    sin_b = jnp.tile(jnp.concatenate([zero, sin], axis=-1), (1, reps))
    return cos_t, sin_a, sin_b


def kernel(x, c, ctx, c_ctx, w_mod, b_mod, norm_g, ffn_w1, ffn_w2, w_in, w_out, lru_conv_w, lru_conv_b,
           lru_wa, lru_ba, lru_wx, lru_bx, lru_lam, hy_conv_w, hy_conv_b, hy_fw0, hy_fb0, hy_fw_in,
           hy_fb_in, hy_freq, hy_fw_last, hy_skip, attn_sink, final_g):
    bsz, n_lat, d = x.shape
    n_ctx = ctx.shape[1]
    depth = w_mod.shape[0]
    assert n_lat % ATT_BLOCK == 0 and n_ctx % ATT_BLOCK == 0 and n_lat % GRID_W == 0
    assert ATT_BLOCK % SCAN_GROUP == 0 and ATT_BLOCK % LANES == 0

    mod_rows = -(-(bsz + 1) // (2 * SUBLANES)) * (2 * SUBLANES)
    c_rows = jnp.zeros((mod_rows, d), F32).at[:bsz].set(c).at[bsz].set(c_ctx)
    mod_all = _modulation(c_rows, w_mod, b_mod).reshape(depth, mod_rows, N_MOD, d)
    lat_row = lambda b: b
    ctx_row = lambda b: bsz

    rope_tabs = _rope_tables(n_lat)
    fft_lat = _fft_tables(n_lat)
    w1_b = ffn_w1.astype(BF16)
    w2_b = ffn_w2.astype(BF16)
    w_in_b = w_in.astype(BF16)
    w_out_b = w_out.astype(BF16)

    xc = ctx
    for l in range(depth):
        need_ctx = l < depth - 1
        mod = mod_all[l]
        filt = (hy_fw0[l], hy_fb0[l], hy_fw_in[l], hy_fb_in[l], hy_freq[l], hy_fw_last[l])

        x = _ffn(x, mod, lat_row, norm_g[l, 0], w1_b, w2_b, l, 0, 0)
        xc = _ffn(xc, mod, ctx_row, norm_g[l, 0], w1_b, w2_b, l, 0, 0)

        xl, gl, zh, q, kv = _input_proj(x, mod, lat_row, norm_g[l, 1], w_in_b, l, rope_tabs)
        xlc, glc, zhc, qc, kvc = _input_proj(xc, mod, ctx_row, norm_g[l, 1], w_in_b, l, None)

        y_lru, yc_lru = _rglru(xl, gl, xlc, glc, lru_conv_w[l], lru_conv_b[l], lru_wa[l], lru_ba[l],
                               lru_wx[l], lru_bx[l], lru_lam[l], bsz)
        y_hy = _hyena_fft(zh, bsz, hy_conv_w[l], hy_conv_b[l], filt, fft_lat, hy_skip[l])
        y_att = _attention(q, kv, kvc, attn_sink[l], True)
        x = _ffn(x, mod, lat_row, norm_g[l, 2], w1_b, w2_b, l, 1, 6, mixer=(y_lru, y_hy, y_att, w_out_b),
                 final_g=None if need_ctx else final_g)

        if need_ctx:
            yc_hy = _hyena(zhc, bsz, hy_conv_w[l], hy_conv_b[l], filt, _dft_matrices(n_ctx), hy_skip[l])
            yc_att = _attention(qc, None, kvc, attn_sink[l], False)
            xc = _ffn(xc, mod, ctx_row, norm_g[l, 2], w1_b, w2_b, l, 1, 6,
                      mixer=(yc_lru, yc_hy, yc_att, w_out_b))
    return x
```

```python
import functools
import math

import jax
import jax.numpy as jnp
from jax import lax
from jax.experimental import pallas as pl
from jax.experimental.pallas import tpu as pltpu

F32 = jnp.float32
BF16 = jnp.bfloat16

NORM_EPS = 1e-6
N_MOD = 9
MACARON_W = 0.5
D_LRU = 256
LRU_BLOCKS = 4
LRU_C = 8.0
LRU_LEFT = 2
D_HY = 256
HY_LEFT = 1
HY_EMB = 33
HY_BANDS = (HY_EMB - 1) // 2
HY_FAST = 0.3
HY_SLOW = 1.5
HY_TARGET = 1e-2
N_QH = 8
N_KVH = 2
HEAD_DIM = 64
D_ATTN = N_QH * HEAD_DIM
D_KV = N_KVH * HEAD_DIM
WINDOW = 128
ATT_BLOCK = 128
ATT_STEP = 2
GRID_W = 64
ROPE_THETA = 10000.0
ROPE_PAIRS_AXIS = HEAD_DIM // 4
NEG_INF = -1e30
LOG2E = math.log2(math.e)
F32_TINY = 1.1754944e-38

LANES = 128
SUBLANES = 8
KV_WIDTH = 2 * N_KVH * LANES
VMEM_LIMIT_BYTES = 56 * 1024 * 1024
ROW_TILE = 512
FFN_SPLIT = 4
FFN_MIN_ROWS = 128
PROJ_SPLIT = 2
DFT_TILE = 1024
DFT_FILTER_TILE = 512
FFT_BLOCK = 128
FFT_SLAB = 16
FFT_COLS = 512
SCAN_GROUP = 64


def _cparams(*sem):
    return pltpu.CompilerParams(dimension_semantics=sem, vmem_limit_bytes=VMEM_LIMIT_BYTES)


def _row_tile(rows):
    return min(ROW_TILE, rows)


def _resident(shape, index=None):
    index = (0,) * len(shape) if index is None else index
    return pl.BlockSpec(shape, lambda *_: index, pipeline_mode=pl.Buffered(1))


def _ada_norm(x, g, shift, scale):
    y = x * lax.rsqrt(jnp.mean(x * x, axis=-1, keepdims=True) + NORM_EPS)
    return (y * g) * (1.0 + scale) + shift


def _sigmoid(x):
    return 0.5 * (1.0 + jnp.tanh(0.5 * x))


def _gelu_tanh(x):
    return 0.5 * x * (1.0 + jnp.tanh(math.sqrt(2.0 / math.pi) * (x + 0.044715 * (x * x * x))))


def _softplus(x):
    return jnp.maximum(x, 0.0) + jnp.log1p(jnp.exp(-jnp.abs(x)))


def _dwconv_tile(x_ref, w, bias, left, i, tile):
    n_rows, width = x_ref.shape
    n_tiles = n_rows // tile
    group = SUBLANES * 4 // x_ref.dtype.itemsize
    r0 = pl.multiple_of(i * tile, tile)
    zeros_i = jnp.zeros((group, width), jnp.int32)
    before = x_ref[pl.ds(pl.multiple_of(jnp.maximum(r0 - group, 0), group), group), :].astype(F32)
    after = x_ref[pl.ds(pl.multiple_of(jnp.minimum(r0 + tile, n_rows - group), group), group), :].astype(F32)
    before = jnp.where(zeros_i + i > 0, before, 0.0)
    after = jnp.where(zeros_i + i < n_tiles - 1, after, 0.0)
    ext = jnp.concatenate([before, x_ref[pl.ds(r0, tile), :].astype(F32), after], axis=0)
    out = jnp.broadcast_to(bias, (tile, width))
    for k in range(w.shape[0]):
        off = k - left
        sh = ext if off == 0 else pltpu.roll(ext, (-off) % ext.shape[0], axis=0)
        out = out + sh[group:group + tile] * w[k:k + 1, :]
    return out


def _mod_kernel(c_ref, w_ref, b_ref, o_ref):
    cv = c_ref[...]
    s = cv * _sigmoid(cv)
    rows = s.shape[0]
    s_hi = s.astype(BF16)
    s_lo = (s - s_hi.astype(F32)).astype(BF16)
    w = w_ref[0]
    w_hi = w.astype(BF16)
    w_lo = (w - w_hi.astype(F32)).astype(BF16)
    p = jnp.dot(jnp.concatenate([s_hi, s_lo], axis=0), w_hi, preferred_element_type=F32)
    o_ref[0] = p[:rows] + p[rows:] + jnp.dot(s_hi, w_lo, preferred_element_type=F32) + b_ref[0]


def _modulation(c_rows, w_mod, b_mod):
    depth, d, nd = w_mod.shape
    rows = c_rows.shape[0]
    tn = nd // 8
    return pl.pallas_call(
        _mod_kernel,
        grid=(depth, nd // tn),
        in_specs=[
            pl.BlockSpec((rows, d), lambda l, j: (0, 0)),
            pl.BlockSpec((1, d, tn), lambda l, j: (l, 0, j)),
            pl.BlockSpec((1, 1, tn), lambda l, j: (l, 0, j)),
        ],
        out_specs=pl.BlockSpec((1, rows, tn), lambda l, j: (l, 0, j)),
        out_shape=jax.ShapeDtypeStruct((depth, rows, nd), F32),
        compiler_params=_cparams("parallel", "parallel"),
        name="modulation",
    )(c_rows, w_mod, b_mod.reshape(depth, 1, nd))


def _ffn_kernel(*refs, i_mod, d_ff, mixer, final):
    refs = list(refs)
    o_ref = refs.pop()
    x_ref, mod_ref, g_ref, w1_ref, w2_ref = refs[:5]
    rest = refs[5:]
    m = mod_ref[...]
    g = g_ref[...]
    tm = x_ref.shape[1]
    n_sub = 1 if mixer else max(n for n in range(1, FFN_SPLIT + 1) if tm % (n * FFN_MIN_ROWS) == 0)
    for n in range(n_sub):
        rows = slice(n * tm // n_sub, (n + 1) * tm // n_sub)
        x = x_ref[0, rows, :]
        if mixer:
            yl_ref, yh_ref, ya_ref, wo_ref = rest[:4]
            y = (jnp.dot(yl_ref[0, rows, :].astype(BF16), wo_ref[0:D_LRU, :], preferred_element_type=F32)
                 + jnp.dot(yh_ref[0, rows, :].astype(BF16), wo_ref[D_LRU:D_LRU + D_HY, :], preferred_element_type=F32)
                 + jnp.dot(ya_ref[0, rows, :].astype(BF16), wo_ref[D_LRU + D_HY:, :], preferred_element_type=F32))
            x = x + m[5:6] * y
        h = _ada_norm(x, g, m[i_mod:i_mod + 1], m[i_mod + 1:i_mod + 2])
        ab = jnp.dot(h.astype(BF16), w1_ref[...], preferred_element_type=F32)
        a = ab[:, :d_ff]
        b = ab[:, d_ff:]
        gated = (a * _sigmoid(a)) * b
        y = jnp.dot(gated.astype(BF16), w2_ref[...], preferred_element_type=F32)
        x = x + (MACARON_W * m[i_mod + 2:i_mod + 3]) * y
        if final:
            x = (x * lax.rsqrt(jnp.mean(x * x, axis=-1, keepdims=True) + NORM_EPS)) * rest[-1][...]
        o_ref[0, rows, :] = x


def _ffn(x, mod, mod_row, g, w1, w2, layer, which, i_mod, mixer=None, final_g=None):
    bsz, rows, d = x.shape
    d_ff = w2.shape[2]
    tm = _row_tile(rows)
    in_specs = [
        pl.BlockSpec((1, tm, d), lambda b, t: (b, t, 0)),
        pl.BlockSpec((None, N_MOD, d), lambda b, t: (mod_row(b), 0, 0)),
        pl.BlockSpec((1, d), lambda b, t: (0, 0)),
        _resident((None, None, d, 2 * d_ff), (layer, which, 0, 0)),
        _resident((None, None, d_ff, d), (layer, which, 0, 0)),
    ]
    args = [x, mod, g.reshape(1, d), w1, w2]
    if mixer is not None:
        y_lru, y_hy, y_att, w_out = mixer
        in_specs += [
            pl.BlockSpec((1, tm, D_LRU), lambda b, t: (b, t, 0)),
            pl.BlockSpec((1, tm, D_HY), lambda b, t: (b, t, 0)),
            pl.BlockSpec((1, tm, D_ATTN), lambda b, t: (b, t, 0)),
            _resident((None,) + w_out.shape[1:], (layer, 0, 0)),
        ]
        args += [y_lru, y_hy, y_att, w_out]
    if final_g is not None:
        in_specs.append(pl.BlockSpec((1, d), lambda b, t: (0, 0)))
        args.append(final_g.reshape(1, d))
    return pl.pallas_call(
        functools.partial(_ffn_kernel, i_mod=i_mod, d_ff=d_ff, mixer=mixer is not None,
                          final=final_g is not None),
        grid=(bsz, rows // tm),
        in_specs=in_specs,
        out_specs=pl.BlockSpec((1, tm, d), lambda b, t: (b, t, 0)),
        out_shape=jax.ShapeDtypeStruct(x.shape, F32),
        compiler_params=_cparams("parallel", "parallel"),
        name="ffn",
    )(*args)


def _rope(x, cos_t, sin_a, sin_b):
    width = x.shape[-1]
    half = HEAD_DIM // 2
    up = pltpu.roll(x, width - half, axis=1)
    dn = pltpu.roll(x, half, axis=1)
    return x * cos_t + up * sin_a + dn * sin_b


def _proj_kernel(*refs, rope):
    if rope:
        (x_ref, mod_ref, g_ref, w_ref, cos_ref, sa_ref, sb_ref,
         xl_ref, gl_ref, zh_ref, q_ref, kv_ref) = refs
    else:
        (x_ref, mod_ref, g_ref, w_ref, xl_ref, gl_ref, zh_ref, q_ref, kv_ref) = refs
    m = mod_ref[...]
    g = g_ref[...]
    tm = x_ref.shape[1]
    n_sub = max(n for n in range(1, PROJ_SPLIT + 1) if tm % (n * FFN_MIN_ROWS) == 0)
    for n in range(n_sub):
        rows = slice(n * tm // n_sub, (n + 1) * tm // n_sub)
        h = _ada_norm(x_ref[0, rows, :], g, m[3:4], m[4:5])
        z = jnp.dot(h.astype(BF16), w_ref[...], preferred_element_type=F32)
        o = 0
        xl_ref[rows, :] = z[:, o:o + D_LRU].astype(BF16); o += D_LRU
        gl_ref[rows, :] = z[:, o:o + D_LRU].astype(BF16); o += D_LRU
        zh_ref[rows, :] = z[:, o:o + 3 * D_HY].astype(BF16); o += 3 * D_HY
        q = z[:, o:o + D_ATTN]; o += D_ATTN
        k = z[:, o:o + D_KV]; o += D_KV
        v = z[:, o:o + D_KV]
        if rope:
            cos_t, sin_a, sin_b = cos_ref[rows, :], sa_ref[rows, :], sb_ref[rows, :]
            q = _rope(q, cos_t, sin_a, sin_b)
            k = _rope(k, cos_t[:, :D_KV], sin_a[:, :D_KV], sin_b[:, :D_KV])
        q_ref[0, rows, :] = (q * (HEAD_DIM ** -0.5 * LOG2E)).astype(BF16)
        low = lax.broadcasted_iota(jnp.int32, k.shape, 1) < HEAD_DIM
        parts = [jnp.where(low, k, 0.0), jnp.where(low, pltpu.roll(k, HEAD_DIM, axis=1), 0.0),
                 jnp.where(low, v, 0.0), jnp.where(low, pltpu.roll(v, HEAD_DIM, axis=1), 0.0)]
        kv_ref[0, rows, :] = jnp.concatenate(parts, axis=1).astype(BF16)


def _input_proj(x, mod, mod_row, g, w_in, layer, rope_tabs):
    bsz, rows, d = x.shape
    tm = _row_tile(rows)
    rope = rope_tabs is not None
    in_specs = [
        pl.BlockSpec((1, tm, d), lambda t, b: (b, t, 0)),
        pl.BlockSpec((None, N_MOD, d), lambda t, b: (mod_row(b), 0, 0)),
        pl.BlockSpec((1, d), lambda t, b: (0, 0)),
        _resident((None,) + w_in.shape[1:], (layer, 0, 0)),
    ]
    args = [x, mod, g.reshape(1, d), w_in]
    if rope:
        in_specs += [pl.BlockSpec((tm, D_ATTN), lambda t, b: (t, 0))] * 3
        args += list(rope_tabs)
    out_shape = [
        jax.ShapeDtypeStruct((rows, bsz * D_LRU), BF16),
        jax.ShapeDtypeStruct((rows, bsz * D_LRU), BF16),
        jax.ShapeDtypeStruct((rows, bsz * 3 * D_HY), BF16),
        jax.ShapeDtypeStruct((bsz, rows, D_ATTN), BF16),
        jax.ShapeDtypeStruct((bsz, rows, KV_WIDTH), BF16),
    ]
    out_specs = [
        pl.BlockSpec((tm, D_LRU), lambda t, b: (t, b)),
        pl.BlockSpec((tm, D_LRU), lambda t, b: (t, b)),
        pl.BlockSpec((tm, 3 * D_HY), lambda t, b: (t, b)),
        pl.BlockSpec((1, tm, D_ATTN), lambda t, b: (b, t, 0)),
        pl.BlockSpec((1, tm, KV_WIDTH), lambda t, b: (b, t, 0)),
    ]
    return pl.pallas_call(
        functools.partial(_proj_kernel, rope=rope),
        grid=(rows // tm, bsz),
        in_specs=in_specs,
        out_specs=out_specs,
        out_shape=out_shape,
        compiler_params=_cparams("parallel", "parallel"),
        name="input_proj",
    )(*args)


def _scan_chunk(a, b, row, reverse):
    for s in (1, 2, 4):
        if reverse:
            ok = row < SUBLANES - s
            sh = SUBLANES - s
        else:
            ok = row >= s
            sh = s
        a_sh = pltpu.roll(a, sh, axis=0)
        b_sh = pltpu.roll(b, sh, axis=0)
        b = jnp.where(ok, a * b_sh + b, b)
        a = jnp.where(ok, a * a_sh, a)
    return a, b


def _lru_kernel(xl_ref, gl_ref, xc_ref, gc_ref, cw_ref, cb_ref, wa_ref, ba_ref, wx_ref, bx_ref,
                lam_ref, y_ref, yc_ref, a_s, b_s, *, n_lat, n_ctx, tile):
    width = xl_ref.shape[1]
    cw = cw_ref[...]
    cb = cb_ref[...]
    neg_c = [(0.5 * LRU_C) * _softplus(-lam_ref[d]) for d in range(2)]

    def coeffs(x_ref, base, n_rows):
        def body(i, carry):
            r0 = pl.multiple_of(i * tile, tile)
            u = _dwconv_tile(x_ref, cw, cb, LRU_LEFT, i, tile)
            ub = u.astype(BF16)
            hu = 0.5 * u
            for d in range(2):
                t_a = jnp.tanh(jnp.dot(ub, wa_ref[d], preferred_element_type=F32) + ba_ref[d])
                t_x = jnp.tanh(jnp.dot(ub, wx_ref[d], preferred_element_type=F32) + bx_ref[d])
                neg_log_a = neg_c[d] + neg_c[d] * t_a
                a = jnp.exp(-neg_log_a)
                dst = pl.ds(pl.multiple_of(base + r0, SUBLANES), tile)
                a_s[d, dst, :] = a
                var = jnp.tanh(neg_log_a) * (1.0 + a * a)
                b_s[d, dst, :] = (var * lax.rsqrt(jnp.maximum(var, F32_TINY))) * (hu + hu * t_x)
            return carry
        lax.fori_loop(0, n_rows // tile, body, 0)

    coeffs(xc_ref, 0, n_ctx)
    coeffs(xl_ref, n_ctx, n_lat)

    row = lax.broadcasted_iota(jnp.int32, (SUBLANES, width), 0)

    def scan_group(d, group, h, reverse):
        r0 = pl.multiple_of(group * SCAN_GROUP, SCAN_GROUP)
        a = a_s[d, pl.ds(r0, SCAN_GROUP), :]
        b = b_s[d, pl.ds(r0, SCAN_GROUP), :]
        order = range(SCAN_GROUP // SUBLANES)
        parts = [_scan_chunk(a[c * SUBLANES:(c + 1) * SUBLANES], b[c * SUBLANES:(c + 1) * SUBLANES], row, reverse)
                 for c in order]
        for c in (reversed(order) if reverse else order):
            hh = parts[c][0] * h + parts[c][1]
            b_s[d, pl.ds(r0 + c * SUBLANES, SUBLANES), :] = hh
            h = hh[0:1, :] if reverse else hh[SUBLANES - 1:SUBLANES, :]
        return h

    ng_ctx = n_ctx // SCAN_GROUP
    ng_all = (n_ctx + n_lat) // SCAN_GROUP
    h0 = jnp.zeros((1, width), F32)

    def ctx_body(j, hs):
        return (scan_group(0, j, hs[0], False), scan_group(1, ng_ctx - 1 - j, hs[1], True))

    def lat_body(j, hs):
        return (scan_group(0, ng_ctx + j, hs[0], False), scan_group(1, ng_all - 1 - j, hs[1], True))

    hs = lax.fori_loop(0, ng_ctx, ctx_body, (h0, h0))
    lax.fori_loop(0, ng_all - ng_ctx, lat_body, hs)

    def finish(o_ref, g_ref, base, n_rows):
        def body(i, carry):
            r0 = pl.multiple_of(i * tile, tile)
            src = pl.ds(pl.multiple_of(base + r0, SUBLANES), tile)
            hsum = b_s[0, src, :] + b_s[1, src, :]
            o_ref[pl.ds(r0, tile), :] = (hsum * _gelu_tanh(g_ref[pl.ds(r0, tile), :].astype(F32))).astype(BF16)
            return carry
        lax.fori_loop(0, n_rows // tile, body, 0)

    finish(yc_ref, gc_ref, 0, n_ctx)
    finish(y_ref, gl_ref, n_ctx, n_lat)


def _block_diag(w):
    two, nb, bs, _ = w.shape
    eye = jnp.eye(nb, dtype=w.dtype)
    return jnp.einsum('dnij,nm->dnimj', w, eye).reshape(two, nb * bs, nb * bs)


def _rglru(xl, gl, xlc, glc, conv_w, conv_b, wa, ba, wx, bx, lam, bsz):
    n_lat, n_ctx = xl.shape[0], xlc.shape[0]
    width = LANES
    per_b = D_LRU // width
    tile = math.gcd(256, math.gcd(n_lat, n_ctx))
    wa_bd = (0.5 * _block_diag(wa)).astype(BF16)
    wx_bd = (0.5 * _block_diag(wx)).astype(BF16)
    ba = 0.5 * ba
    bx = 0.5 * bx
    col = lambda b, j: (0, b * per_b + j)
    par = lambda b, j: (0, j)
    par3 = lambda b, j: (0, 0, j)
    return pl.pallas_call(
        functools.partial(_lru_kernel, n_lat=n_lat, n_ctx=n_ctx, tile=tile),
        grid=(bsz, per_b),
        in_specs=[
            pl.BlockSpec((n_lat, width), col),
            pl.BlockSpec((n_lat, width), col),
            pl.BlockSpec((n_ctx, width), col),
            pl.BlockSpec((n_ctx, width), col),
            pl.BlockSpec((conv_w.shape[0], width), par),
            pl.BlockSpec((1, width), par),
            pl.BlockSpec((2, width, width), lambda b, j: (0, j, j)),
            pl.BlockSpec((2, 1, width), par3),
            pl.BlockSpec((2, width, width), lambda b, j: (0, j, j)),
            pl.BlockSpec((2, 1, width), par3),
            pl.BlockSpec((2, 1, width), par3),
        ],
        out_specs=[pl.BlockSpec((None, n_lat, width), lambda b, j: (b, 0, j)),
                   pl.BlockSpec((None, n_ctx, width), lambda b, j: (b, 0, j))],
        out_shape=[jax.ShapeDtypeStruct((bsz, n_lat, D_LRU), BF16), jax.ShapeDtypeStruct((bsz, n_ctx, D_LRU), BF16)],
        scratch_shapes=[pltpu.VMEM((2, n_ctx + n_lat, width), F32),
                        pltpu.VMEM((2, n_ctx + n_lat, width), F32)],
        compiler_params=_cparams("parallel", "parallel"),
        name="rglru",
    )(xl, gl, xlc, glc, conv_w, conv_b.reshape(1, -1), wa_bd, ba.reshape(2, 1, -1), wx_bd,
      bx.reshape(2, 1, -1), lam.reshape(2, 1, -1))


def _hy_pre_kernel(z0_ref, z1_ref, z2_ref, w0_ref, w1_ref, w2_ref, b0_ref, b1_ref, b2_ref,
                   ub_ref, x0_ref, *, tile):
    w0, w1, w2 = w0_ref[...], w1_ref[...], w2_ref[...]
    b0, b1, b2 = b0_ref[...], b1_ref[...], b2_ref[...]

    def body(i, carry):
        rows = pl.ds(pl.multiple_of(i * tile, tile), tile)
        x1 = _dwconv_tile(z1_ref, w1, b1, HY_LEFT, i, tile)
        v = _dwconv_tile(z2_ref, w2, b2, HY_LEFT, i, tile)
        ub_ref[rows, :] = (x1 * v).astype(BF16)
        x0_ref[rows, :] = _dwconv_tile(z0_ref, w0, b0, HY_LEFT, i, tile).astype(BF16)
        return carry
    lax.fori_loop(0, z0_ref.shape[0] // tile, body, 0)


def _hyena_pre(zh, conv_w, conv_b, bsz):
    rows = zh.shape[0]
    width = LANES
    per_b = D_HY // width
    tile = math.gcd(256, rows)
    zspec = lambda part: pl.BlockSpec((rows, width), lambda b, j: (0, b * 3 * per_b + part * per_b + j))
    wspec = lambda part: pl.BlockSpec((conv_w.shape[0], width), lambda b, j: (0, part * per_b + j))
    bspec = lambda part: pl.BlockSpec((1, width), lambda b, j: (0, part * per_b + j))
    ospec = pl.BlockSpec((rows, width), lambda b, j: (0, b * per_b + j))
    cb = conv_b.reshape(1, -1)
    return pl.pallas_call(
        functools.partial(_hy_pre_kernel, tile=tile),
        grid=(bsz, per_b),
        in_specs=[zspec(0), zspec(1), zspec(2), wspec(0), wspec(1), wspec(2), bspec(0), bspec(1), bspec(2)],
        out_specs=[ospec, ospec],
        out_shape=[jax.ShapeDtypeStruct((rows, bsz * D_HY), BF16),
                   jax.ShapeDtypeStruct((rows, bsz * D_HY), BF16)],
        compiler_params=_cparams("parallel", "parallel"),
        name="hyena_pre",
    )(zh, zh, zh, conv_w, conv_w, conv_w, cb, cb, cb)


def _sin_half_lanes(x):
    rows, width = x.shape
    if 2 * width != LANES or rows % (2 * SUBLANES):
        return jnp.sin(x)
    s = jnp.sin(jnp.concatenate([x[:rows // 2], x[rows // 2:]], axis=1))
    return jnp.concatenate([s[:, :width], s[:, width:]], axis=0)


def _hy_filter_kernel(z_ref, fw0_ref, fb0_ref, fwin_ref, fbin_ref, freq_ref, fwl_ref, dl_ref, o_ref, *, tile):
    hp = lax.Precision.HIGHEST
    z = z_ref[...]
    fr = freq_ref[...]
    hdn = _sin_half_lanes(fr * (jnp.dot(z, fw0_ref[...], preferred_element_type=F32, precision=hp) + fb0_ref[...]))
    for j in range(fwin_ref.shape[0]):
        hdn = _sin_half_lanes(
            fr * (jnp.dot(hdn, fwin_ref[j], preferred_element_type=F32, precision=hp) + fbin_ref[j]))
    k = jnp.dot(hdn, fwl_ref[...], preferred_element_type=F32, precision=hp)
    decay = jnp.exp(-z[:, 0:1] * dl_ref[...])
    k_fwd = k[:, :D_HY] * decay
    k_bwd = k[:, D_HY:] * decay
    row = lax.broadcasted_iota(jnp.int32, k_bwd.shape, 0) + pl.program_id(0) * tile
    k_bwd = jnp.where(row == 0, 0.0, k_bwd)
    o_ref[...] = jnp.concatenate([k_fwd, k_bwd], axis=-1).astype(BF16)


def _hyena_filter_taps(n, fw0, fb0, fw_in, fb_in, freq, fw_last):
    t = jnp.linspace(0.0, 1.0, n, dtype=F32)[:, None]
    w = 2.0 * math.pi * jnp.arange(n, dtype=F32)[:, None] / n
    f = jnp.linspace(1e-4, HY_BANDS - 1, HY_BANDS, dtype=F32)[None, :]
    z = jnp.concatenate([t, jnp.cos(f * w), -jnp.sin(f * w)], axis=-1)
    z = jnp.pad(z, ((0, 0), (0, LANES - HY_EMB)))
    fw0p = jnp.pad(fw0, ((0, LANES - HY_EMB), (0, 0)))
    max_decay = math.log(HY_TARGET) / HY_FAST
    min_decay = math.log(HY_TARGET) / HY_SLOW
    deltas = jnp.abs(jnp.linspace(min_decay, max_decay, D_HY, dtype=F32))[None, :]
    hid = fw0.shape[1]
    tile = min(512, n)
    full = lambda a: pl.BlockSpec(a.shape, lambda i: (0,) * a.ndim)
    args = [fw0p, fb0.reshape(1, hid), fw_in, fb_in.reshape(-1, 1, hid), freq.reshape(1, hid), fw_last, deltas]
    return pl.pallas_call(
        functools.partial(_hy_filter_kernel, tile=tile),
        grid=(n // tile,),
        in_specs=[pl.BlockSpec((tile, LANES), lambda i: (i, 0))] + [full(a) for a in args],
        out_specs=pl.BlockSpec((tile, 2 * D_HY), lambda i: (i, 0)),
        out_shape=jax.ShapeDtypeStruct((n, 2 * D_HY), BF16),
        compiler_params=_cparams("parallel"),
        name="hyena_filter",
    )(z, *args)


def _dft_expand_kernel(ca_ref, sa_ref, cb_ref, sb_ref, cm_ref, sm_ref, *, tile):
    ca, sa, cb, sb = ca_ref[...], sa_ref[...], cb_ref[...], sb_ref[...]
    row = lax.broadcasted_iota(jnp.int32, cb.shape, 0) + pl.program_id(0) * tile
    lane = lax.broadcasted_iota(jnp.int32, cb.shape, 1)
    alt = jnp.where(lane % 2 == 0, 1.0, -1.0)
    for s1 in range(cm_ref.shape[1] // LANES):
        c1 = ca[:, s1:s1 + 1]
        d1 = sa[:, s1:s1 + 1]
        cols = slice(s1 * LANES, (s1 + 1) * LANES)
        cm_ref[:, cols] = (c1 * cb - d1 * sb).astype(BF16)
        sm_ref[:, cols] = jnp.where(row == 0, alt, -(d1 * cb + c1 * sb)).astype(BF16)


def _dft_matrices(n):
    two_n = 2 * n
    f = jnp.arange(n, dtype=jnp.int32)[:, None]
    s_hi = jnp.arange(n // LANES, dtype=jnp.int32)[None, :] * LANES
    s_lo = jnp.arange(LANES, dtype=jnp.int32)[None, :]
    ang_a = ((f * s_hi) % two_n).astype(F32) * (2.0 * math.pi / two_n)
    ang_b = ((f * s_lo) % two_n).astype(F32) * (2.0 * math.pi / two_n)
    tile = min(DFT_TILE, n)
    hi_spec = pl.BlockSpec((tile, n // LANES), lambda i: (i, 0))
    lo_spec = pl.BlockSpec((tile, LANES), lambda i: (i, 0))
    return pl.pallas_call(
        functools.partial(_dft_expand_kernel, tile=tile),
        grid=(n // tile,),
        in_specs=[hi_spec, hi_spec, lo_spec, lo_spec],
        out_specs=[pl.BlockSpec((tile, n), lambda i: (i, 0))] * 2,
        out_shape=[jax.ShapeDtypeStruct((n, n), BF16)] * 2,
        compiler_params=_cparams("parallel"),
        name="dft_matrices",
    )(jnp.cos(ang_a), jnp.sin(ang_a), jnp.cos(ang_b), jnp.sin(ang_b))


def _dft_filter_kernel(cm_ref, sm_ref, kk_ref, kre_ref, kim_ref, *, tile):
    kk = kk_ref[...]
    xre = jnp.dot(cm_ref[...], kk, preferred_element_type=F32)
    xim = jnp.dot(sm_ref[...], kk, preferred_element_type=F32)
    row = lax.broadcasted_iota(jnp.int32, (tile, D_HY), 0) + pl.program_id(0) * tile
    kre_ref[...] = xre[:, :D_HY] + xre[:, D_HY:]
    kim_ref[...] = jnp.where(row == 0, xim[:, :D_HY] + xim[:, D_HY:], xim[:, :D_HY] - xim[:, D_HY:])


def _dft_filter(cm, sm, kk):
    n = cm.shape[0]
    tile = min(DFT_FILTER_TILE, n)
    return pl.pallas_call(
        functools.partial(_dft_filter_kernel, tile=tile),
        grid=(n // tile,),
        in_specs=[pl.BlockSpec((tile, n), lambda f: (f, 0)),
                  pl.BlockSpec((tile, n), lambda f: (f, 0)),
                  _resident(kk.shape)],
        out_specs=[pl.BlockSpec((tile, D_HY), lambda f: (f, 0))] * 2,
        out_shape=[jax.ShapeDtypeStruct((n, D_HY), F32)] * 2,
        compiler_params=_cparams("parallel"),
        name="dft_filter",
    )(cm, sm, kk)


def _dft_fwd_kernel(cm_ref, sm_ref, u_ref, kre_ref, kim_ref, yre_ref, yim_ref, *, tile):
    n_freq = cm_ref.shape[1]
    u = u_ref[...]
    xre = jnp.dot(cm_ref[...], u, preferred_element_type=F32)
    xim = jnp.dot(sm_ref[...], u, preferred_element_type=F32)
    kre = kre_ref[...]
    kim = kim_ref[...]
    row = lax.broadcasted_iota(jnp.int32, xre.shape, 0) + pl.program_id(0) * tile
    first = row == 0
    scale = jnp.where(first, 0.5 / n_freq, 1.0 / n_freq)
    yre_ref[...] = (scale * (xre * kre - jnp.where(first, 0.0, xim * kim))).astype(BF16)
    yim_ref[...] = (scale * jnp.where(first, xim * kim, xre * kim + xim * kre)).astype(BF16)


def _dft_fwd(cm, sm, ub, kre, kim):
    n = cm.shape[0]
    cols = ub.shape[1]
    tile = min(DFT_TILE, n)
    return pl.pallas_call(
        functools.partial(_dft_fwd_kernel, tile=tile),
        grid=(n // tile, cols // D_HY),
        in_specs=[pl.BlockSpec((tile, n), lambda f, c: (f, 0)),
                  pl.BlockSpec((tile, n), lambda f, c: (f, 0)),
                  pl.BlockSpec((n, D_HY), lambda f, c: (0, c)),
                  pl.BlockSpec((tile, D_HY), lambda f, c: (f, 0)),
                  pl.BlockSpec((tile, D_HY), lambda f, c: (f, 0))],
        out_specs=[pl.BlockSpec((tile, D_HY), lambda f, c: (f, c))] * 2,
        out_shape=[jax.ShapeDtypeStruct((n, cols), BF16)] * 2,
        compiler_params=_cparams("parallel", "parallel"),
        name="dft_fwd",
    )(cm, sm, ub, kre, kim)


def _dft_inv_kernel(ci_ref, si_ref, yre_ref, yim_ref, u_ref, x0_ref, skip_ref, o_ref, *, tile):
    y_cos = jnp.dot(ci_ref[...], yre_ref[...], preferred_element_type=F32)
    y_sin = jnp.dot(si_ref[...], yim_ref[...], preferred_element_type=F32)
    row = lax.broadcasted_iota(jnp.int32, y_cos.shape, 0) + pl.program_id(0) * tile
    nyq = jnp.where(row % 2 == 0, 1.0, -1.0) * yim_ref[0:1, :].astype(F32)
    y = y_cos + jnp.where(row == 0, 0.0, y_sin) + nyq
    o_ref[...] = (x0_ref[...] * (y + u_ref[...] * skip_ref[...])).astype(BF16)


def _dft_inv(ci, si, yre, yim, u, x0, skip):
    n = ci.shape[0]
    cols = yre.shape[1]
    tile = min(DFT_TILE, n)
    return pl.pallas_call(
        functools.partial(_dft_inv_kernel, tile=tile),
        grid=(n // tile, cols // D_HY),
        in_specs=[pl.BlockSpec((tile, n), lambda t, c: (t, 0)),
                  pl.BlockSpec((tile, n), lambda t, c: (t, 0)),
                  pl.BlockSpec((n, D_HY), lambda t, c: (0, c)),
                  pl.BlockSpec((n, D_HY), lambda t, c: (0, c)),
                  pl.BlockSpec((tile, D_HY), lambda t, c: (t, c)),
                  pl.BlockSpec((tile, D_HY), lambda t, c: (t, c)),
                  pl.BlockSpec((1, D_HY), lambda t, c: (0, 0))],
        out_specs=pl.BlockSpec((tile, D_HY), lambda t, c: (t, c)),
        out_shape=jax.ShapeDtypeStruct((n, cols), BF16),
        compiler_params=_cparams("parallel", "parallel"),
        name="dft_inv",
    )(ci, si, yre, yim, u, x0, skip.reshape(1, D_HY))


def _fft_tables(n):
    n1 = 2 * n // FFT_BLOCK
    half = n1 // 2
    nf = half + 1
    groups = FFT_BLOCK // FFT_SLAB
    f1 = jnp.arange(nf, dtype=F32)[:, None]
    s1 = jnp.arange(half, dtype=F32)[None, :]
    ang1 = (2.0 * math.pi / n1) * f1 * s1
    eye = jnp.eye(FFT_SLAB, dtype=F32)
    m1 = jnp.concatenate([jnp.kron(jnp.cos(ang1), eye), jnp.kron(-jnp.sin(ang1), eye)], axis=0)
    m1_inv = jnp.concatenate([jnp.kron(jnp.cos(ang1).T, eye), jnp.kron(-jnp.sin(ang1).T, eye)], axis=1)
    k = jnp.arange(FFT_BLOCK, dtype=F32)
    ang2 = (2.0 * math.pi / FFT_BLOCK) * k[:, None] * k[None, :]
    c2, d2 = jnp.cos(ang2), jnp.sin(ang2)
    g2 = jnp.block([[c2, d2], [-d2, c2]])
    g2_inv = jnp.block([[c2, -d2], [d2, c2]])
    psi = (2.0 * math.pi / (2 * n)) * f1 * k[None, :]
    tw = jnp.stack([jnp.cos(psi), jnp.sin(psi)])
    tw_s2 = jnp.broadcast_to(tw[..., None], (2, nf, FFT_BLOCK, LANES))
    tw_s1 = tw.reshape(2, nf, groups, FFT_SLAB).transpose(2, 0, 1, 3).reshape(groups, 2, nf * FFT_SLAB)
    tw_s1 = jnp.broadcast_to(tw_s1[..., None], (groups, 2, nf * FFT_SLAB, LANES))
    return dict(n=n, half=half, nf=nf, m1=m1.astype(BF16), m1_inv=m1_inv.astype(BF16), g2=g2.astype(BF16),
                g2_inv=g2_inv.astype(BF16), tw_s1=tw_s1, tw_s2=tw_s2)


def _lane_tile(x, width):
    return jnp.concatenate([x] * (width // x.shape[-1]), axis=-1) if width != x.shape[-1] else x


def _fft_s1_kernel(u_ref, m_ref, tw_ref, o_ref):
    h, slab, tc = u_ref.shape
    r = jnp.dot(m_ref[...], u_ref[...].reshape(h * slab, tc), preferred_element_type=F32)
    rows = r.shape[0] // 2
    re, im = r[:rows], r[rows:]
    cs = _lane_tile(tw_ref[0], tc)
    sn = _lane_tile(tw_ref[1], tc)
    o_ref[0] = (re * cs + im * sn).astype(BF16).reshape(rows // slab, slab, tc)
    o_ref[1] = (im * cs - re * sn).astype(BF16).reshape(rows // slab, slab, tc)


def _fft_s1(ub, tabs):
    n, cols = ub.shape
    half, nf = tabs["half"], tabs["nf"]
    groups = FFT_BLOCK // FFT_SLAB
    tc = min(FFT_COLS, cols)
    out = pl.pallas_call(
        _fft_s1_kernel,
        grid=(groups, cols // tc),
        in_specs=[pl.BlockSpec((half, None, FFT_SLAB, tc), lambda m, c: (0, m, 0, c)),
                  _resident(tabs["m1"].shape),
                  pl.BlockSpec((None, 2, nf * FFT_SLAB, LANES), lambda m, c: (m, 0, 0, 0))],
        out_specs=pl.BlockSpec((2, nf, None, FFT_SLAB, tc), lambda m, c: (0, 0, m, 0, c)),
        out_shape=jax.ShapeDtypeStruct((2, nf, groups, FFT_SLAB, cols), BF16),
        compiler_params=_cparams("parallel", "parallel"),
        name="fft_stage1",
    )(ub.reshape(half, groups, FFT_SLAB, cols), tabs["m1"], tabs["tw_s1"])
    return out.reshape(2, nf, FFT_BLOCK, cols)


def _fft_mid_kernel(a_ref, g_ref, gi_ref, k_ref, tw_ref, b_ref, *, gf, half, scale):
    g2 = g_ref[...]
    g2_inv = gi_ref[...]
    tc = a_ref.shape[-1]
    for j in range(gf):
        x = jnp.dot(g2, jnp.concatenate([a_ref[0, j], a_ref[1, j]], axis=0), preferred_element_type=F32)
        xre, xim = x[:FFT_BLOCK], x[FFT_BLOCK:]
        kre, kim = k_ref[0, j], k_ref[1, j]
        f1 = pl.program_id(0) * gf + j
        w = jnp.where(jnp.logical_or(f1 == 0, f1 == half), scale, 2.0 * scale)
        y = jnp.concatenate([w * (xre * kre - xim * kim), w * (xre * kim + xim * kre)], axis=0).astype(BF16)
        b = jnp.dot(g2_inv, y, preferred_element_type=F32)
        bre, bim = b[:FFT_BLOCK], b[FFT_BLOCK:]
        cs = _lane_tile(tw_ref[0, j], tc)
        sn = _lane_tile(tw_ref[1, j], tc)
        b_ref[0, j] = (bre * cs - bim * sn).astype(BF16)
        b_ref[1, j] = (bre * sn + bim * cs).astype(BF16)


def _fft_group(nf):
    return max(g for g in range(1, 12) if nf % g == 0)


def _fft_mid(a, tabs, spec):
    _, nf, _, cols = a.shape
    gf = _fft_group(nf)
    blk = lambda w, idx: pl.BlockSpec((2, gf, FFT_BLOCK, w), idx)
    return pl.pallas_call(
        functools.partial(_fft_mid_kernel, gf=gf, half=tabs["half"], scale=0.5 / tabs["n"]),
        grid=(nf // gf, cols // D_HY),
        in_specs=[blk(D_HY, lambda g, c: (0, g, 0, c)), _resident(tabs["g2"].shape),
                  _resident(tabs["g2_inv"].shape), blk(D_HY, lambda g, c: (0, g, 0, 0)),
                  blk(LANES, lambda g, c: (0, g, 0, 0))],
        out_specs=blk(D_HY, lambda g, c: (0, g, 0, c)),
        out_shape=jax.ShapeDtypeStruct(a.shape, BF16),
        compiler_params=_cparams("parallel", "parallel"),
        name="fft_mid",
    )(a, tabs["g2"], tabs["g2_inv"], spec, tabs["tw_s2"])


def _fft_s2_filter_kernel(a_ref, g_ref, k_ref, *, gf):
    g2 = g_ref[...]
    for j in range(gf):
        x = jnp.dot(g2, jnp.concatenate([a_ref[0, j], a_ref[1, j]], axis=0), preferred_element_type=F32)
        xre, xim = x[:FFT_BLOCK], x[FFT_BLOCK:]
        k_ref[0, j] = xre[:, :D_HY] + xre[:, D_HY:]
        k_ref[1, j] = xim[:, :D_HY] - xim[:, D_HY:]


def _fft_s2_filter(a, tabs):
    _, nf, _, cols = a.shape
    gf = _fft_group(nf)
    return pl.pallas_call(
        functools.partial(_fft_s2_filter_kernel, gf=gf),
        grid=(nf // gf,),
        in_specs=[pl.BlockSpec((2, gf, FFT_BLOCK, cols), lambda g: (0, g, 0, 0)), _resident(tabs["g2"].shape)],
        out_specs=pl.BlockSpec((2, gf, FFT_BLOCK, D_HY), lambda g: (0, g, 0, 0)),
        out_shape=jax.ShapeDtypeStruct((2, nf, FFT_BLOCK, D_HY), F32),
        compiler_params=_cparams("parallel"),
        name="fft_stage2_filter",
    )(a, tabs["g2"])


def _fft_s1_inv_kernel(b_ref, m_ref, u_ref, x0_ref, skip_ref, o_ref):
    two, nf, slab, tc = b_ref.shape
    h = u_ref.shape[0]
    y = jnp.dot(m_ref[...], b_ref[...].reshape(two * nf * slab, tc), preferred_element_type=F32)
    u = u_ref[...].reshape(h * slab, tc).astype(F32)
    x0 = x0_ref[...].reshape(h * slab, tc).astype(F32)
    res = (x0 * (y + u * _lane_tile(skip_ref[...], tc))).astype(BF16)
    for bi in range(o_ref.shape[0]):
        o_ref[bi] = res[:, bi * D_HY:(bi + 1) * D_HY].reshape(h, slab, D_HY)


def _fft_s1_inv(b, tabs, u, x0, skip):
    _, nf, _, cols = b.shape
    n, half = tabs["n"], tabs["half"]
    groups = FFT_BLOCK // FFT_SLAB
    tc = min(FFT_COLS, cols)
    rows4 = lambda a: a.reshape(half, groups, FFT_SLAB, cols)
    tspec = pl.BlockSpec((half, None, FFT_SLAB, tc), lambda m, c: (0, m, 0, c))
    out = pl.pallas_call(
        _fft_s1_inv_kernel,
        grid=(groups, cols // tc),
        in_specs=[pl.BlockSpec((2, nf, None, FFT_SLAB, tc), lambda m, c: (0, 0, m, 0, c)),
                  _resident(tabs["m1_inv"].shape), tspec, tspec,
                  pl.BlockSpec((1, D_HY), lambda m, c: (0, 0))],
        out_specs=pl.BlockSpec((tc // D_HY, half, None, FFT_SLAB, D_HY), lambda m, c: (c, 0, m, 0, 0)),
        out_shape=jax.ShapeDtypeStruct((cols // D_HY, half, groups, FFT_SLAB, D_HY), BF16),
        compiler_params=_cparams("parallel", "parallel"),
        name="fft_stage1_inv",
    )(b.reshape(2, nf, groups, FFT_SLAB, cols), tabs["m1_inv"], rows4(u), rows4(x0), skip.reshape(1, D_HY))
    return out.reshape(cols // D_HY, n, D_HY)


def _hyena_fft(zh, bsz, conv_w, conv_b, filt, tabs, skip):
    ub, x0 = _hyena_pre(zh, conv_w, conv_b, bsz)
    kk = _hyena_filter_taps(zh.shape[0], *filt)
    spec = _fft_s2_filter(_fft_s1(kk, tabs), tabs)
    return _fft_s1_inv(_fft_mid(_fft_s1(ub, tabs), tabs, spec), tabs, ub, x0, skip)


def _hyena(zh, bsz, conv_w, conv_b, filt, dft, skip):
    cm, sm = dft
    n = zh.shape[0]
    ub, x0 = _hyena_pre(zh, conv_w, conv_b, bsz)
    kk = _hyena_filter_taps(n, *filt)
    kre, kim = _dft_filter(cm, sm, kk)
    yre, yim = _dft_fwd(cm, sm, ub, kre, kim)
    y = _dft_inv(cm, sm, yre, yim, ub, x0, skip)
    return y.reshape(n, bsz, D_HY).transpose(1, 0, 2)


def _attend_block(q, kv, sink_ref, masks):
    k_low = [kv[:, g * LANES:(g + 1) * LANES] for g in range(N_KVH)]
    ones = jnp.ones((kv.shape[0], LANES), BF16)
    v_aug = [jnp.concatenate([kv[:, (N_KVH + g) * LANES:(N_KVH + g + 1) * LANES], ones], axis=1)
             for g in range(N_KVH)]
    nq = q.shape[0]
    windowed = masks is not None
    if windowed:
        valid_prev, valid_next = masks
    cols_per_g = D_ATTN // LANES // N_KVH
    out_cols = []
    for g in range(N_KVH):
        cols = [q[:, c * LANES:(c + 1) * LANES] for c in range(g * cols_per_g, (g + 1) * cols_per_g)]
        qg = jnp.concatenate(cols + [pltpu.roll(cq, HEAD_DIM, axis=1) for cq in cols], axis=0).astype(BF16)
        s_all = lax.dot_general(qg, k_low[g], (((1,), (1,)), ((), ())), preferred_element_type=F32)
        es, sinks = [], []
        for hb in range(2 * cols_per_g):
            h = 2 * (g * cols_per_g + hb % cols_per_g) + hb // cols_per_g
            s = s_all[hb * nq:(hb + 1) * nq]
            if windowed:
                s = jnp.concatenate([
                    jnp.where(valid_prev, s[:, :ATT_BLOCK], NEG_INF),
                    s[:, ATT_BLOCK:2 * ATT_BLOCK],
                    jnp.where(valid_next, s[:, 2 * ATT_BLOCK:3 * ATT_BLOCK], NEG_INF),
                    s[:, 3 * ATT_BLOCK:]], axis=1)
            sk = sink_ref[h:h + 1, 0:1] * LOG2E
            m = jnp.maximum(jnp.max(s, axis=-1, keepdims=True), sk)
            es.append(jnp.exp2(s - m).astype(BF16))
            sinks.append(jnp.exp2(sk - m))
        o_all = jnp.dot(jnp.concatenate(es, axis=0), v_aug[g], preferred_element_type=F32)
        outs = []
        for hb in range(2 * cols_per_g):
            o = o_all[hb * nq:(hb + 1) * nq]
            outs.append(o[:, :LANES] / (o[:, LANES:] + sinks[hb]))
        out_cols += [outs[ci] + pltpu.roll(outs[cols_per_g + ci], HEAD_DIM, axis=1) for ci in range(cols_per_g)]
    return out_cols


def _attn_kernel(*refs, windowed, n_sub):
    if windowed:
        (q_ref, kvp_ref, kvc_ref, kvn_ref, kvx_ref, sink_ref, o_ref) = refs
        i = pl.program_id(1)
        last = pl.num_programs(1) - 1
        r = lax.broadcasted_iota(jnp.int32, (ATT_BLOCK, ATT_BLOCK), 0)
        j = lax.broadcasted_iota(jnp.int32, (ATT_BLOCK, ATT_BLOCK), 1)
        cur = kvc_ref[0]
        blocks = ([kvp_ref[0]] + [cur[n * ATT_BLOCK:(n + 1) * ATT_BLOCK] for n in range(n_sub)]
                  + [kvn_ref[0]])
    else:
        (q_ref, kvx_ref, sink_ref, o_ref) = refs
    ctx = kvx_ref[0]
    for n in range(n_sub):
        rows = slice(n * ATT_BLOCK, (n + 1) * ATT_BLOCK)
        q = q_ref[0, rows, :].astype(F32)
        if windowed:
            no_prev = jnp.where(i == 0, ATT_BLOCK, 0) if n == 0 else 0
            no_next = jnp.where(i == last, ATT_BLOCK, 0) if n == n_sub - 1 else 0
            masks = (j >= r + no_prev, j <= r - no_next)
            kv = jnp.concatenate(blocks[n:n + 3] + [ctx], axis=0)
        else:
            masks = None
            kv = ctx
        for c, col in enumerate(_attend_block(q, kv, sink_ref, masks)):
            o_ref[0, rows, c * LANES:(c + 1) * LANES] = col.astype(BF16)


def _attention(q, kv, kvx, sink, windowed):
    bsz, lq, _ = q.shape
    n_ctx, kv_w = kvx.shape[1:]
    nb = lq // ATT_BLOCK
    n_sub = ATT_STEP if nb % ATT_STEP == 0 else 1
    step = n_sub * ATT_BLOCK
    sink_t = jnp.broadcast_to(sink.reshape(N_QH, 1), (N_QH, LANES))
    qspec = pl.BlockSpec((1, step, D_ATTN), lambda b, i: (b, i, 0))
    xspec = pl.BlockSpec((1, n_ctx, kv_w), lambda b, i: (b, 0, 0))
    sspec = pl.BlockSpec((N_QH, LANES), lambda b, i: (0, 0))
    if windowed:
        prev = pl.BlockSpec((1, ATT_BLOCK, kv_w), lambda b, i: (b, jnp.maximum(n_sub * i - 1, 0), 0))
        cur = pl.BlockSpec((1, step, kv_w), lambda b, i: (b, i, 0))
        nxt = pl.BlockSpec((1, ATT_BLOCK, kv_w), lambda b, i: (b, jnp.minimum(n_sub * (i + 1), nb - 1), 0))
        in_specs = [qspec, prev, cur, nxt, xspec, sspec]
        args = (q, kv, kv, kv, kvx, sink_t)
    else:
        in_specs = [qspec, xspec, sspec]
        args = (q, kvx, sink_t)
    return pl.pallas_call(
        functools.partial(_attn_kernel, windowed=windowed, n_sub=n_sub),
        grid=(bsz, lq // step),
        in_specs=in_specs,
        out_specs=pl.BlockSpec((1, step, D_ATTN), lambda b, i: (b, i, 0)),
        out_shape=jax.ShapeDtypeStruct(q.shape, BF16),
        compiler_params=_cparams("parallel", "parallel"),
        name="attention",
    )(*args)


def _rope_tables(n_lat):
    rows = n_lat // GRID_W
    r = jnp.repeat(jnp.arange(rows, dtype=F32), GRID_W)
    col = jnp.tile(jnp.arange(GRID_W, dtype=F32), rows)
    inv = ROPE_THETA ** (-jnp.arange(ROPE_PAIRS_AXIS, dtype=F32) / ROPE_PAIRS_AXIS)
    ang = jnp.concatenate([r[:, None] * inv, col[:, None] * inv], axis=-1)
    cos, sin = jnp.cos(ang), jnp.sin(ang)
    zero = jnp.zeros_like(sin)
    reps = D_ATTN // HEAD_DIM
    cos_t = jnp.tile(jnp.concatenate([cos, cos], axis=-1), (1, reps))
    sin_a = jnp.tile(jnp.concatenate([-sin, zero], axis=-1), (1, reps))
    sin_b = jnp.tile(jnp.concatenate([zero, sin], axis=-1), (1, reps))
    return cos_t, sin_a, sin_b


def kernel(x, c, ctx, c_ctx, w_mod, b_mod, norm_g, ffn_w1, ffn_w2, w_in, w_out, lru_conv_w, lru_conv_b,
           lru_wa, lru_ba, lru_wx, lru_bx, lru_lam, hy_conv_w, hy_conv_b, hy_fw0, hy_fb0, hy_fw_in,
           hy_fb_in, hy_freq, hy_fw_last, hy_skip, attn_sink, final_g):
    bsz, n_lat, d = x.shape
    n_ctx = ctx.shape[1]
    depth = w_mod.shape[0]
    assert n_lat % ATT_BLOCK == 0 and n_ctx % ATT_BLOCK == 0 and n_lat % GRID_W == 0
    assert ATT_BLOCK % SCAN_GROUP == 0 and ATT_BLOCK % LANES == 0

    mod_rows = -(-(bsz + 1) // (2 * SUBLANES)) * (2 * SUBLANES)
    c_rows = jnp.zeros((mod_rows, d), F32).at[:bsz].set(c).at[bsz].set(c_ctx)
    mod_all = _modulation(c_rows, w_mod, b_mod).reshape(depth, mod_rows, N_MOD, d)
    lat_row = lambda b: b
    ctx_row = lambda b: bsz

    rope_tabs = _rope_tables(n_lat)
    fft_lat = _fft_tables(n_lat)
    w1_b = ffn_w1.astype(BF16)
    w2_b = ffn_w2.astype(BF16)
    w_in_b = w_in.astype(BF16)
    w_out_b = w_out.astype(BF16)

    xc = ctx
    for l in range(depth):
        need_ctx = l < depth - 1
        mod = mod_all[l]
        filt = (hy_fw0[l], hy_fb0[l], hy_fw_in[l], hy_fb_in[l], hy_freq[l], hy_fw_last[l])

        x = _ffn(x, mod, lat_row, norm_g[l, 0], w1_b, w2_b, l, 0, 0)
        xc = _ffn(xc, mod, ctx_row, norm_g[l, 0], w1_b, w2_b, l, 0, 0)

        xl, gl, zh, q, kv = _input_proj(x, mod, lat_row, norm_g[l, 1], w_in_b, l, rope_tabs)
        xlc, glc, zhc, qc, kvc = _input_proj(xc, mod, ctx_row, norm_g[l, 1], w_in_b, l, None)

        y_lru, yc_lru = _rglru(xl, gl, xlc, glc, lru_conv_w[l], lru_conv_b[l], lru_wa[l], lru_ba[l],
                               lru_wx[l], lru_bx[l], lru_lam[l], bsz)
        y_hy = _hyena_fft(zh, bsz, hy_conv_w[l], hy_conv_b[l], filt, fft_lat, hy_skip[l])
        y_att = _attention(q, kv, kvc, attn_sink[l], True)
        x = _ffn(x, mod, lat_row, norm_g[l, 2], w1_b, w2_b, l, 1, 6, mixer=(y_lru, y_hy, y_att, w_out_b),
                 final_g=None if need_ctx else final_g)

        if need_ctx:
            yc_hy = _hyena(zhc, bsz, hy_conv_w[l], hy_conv_b[l], filt, _dft_matrices(n_ctx), hy_skip[l])
            yc_att = _attention(qc, None, kvc, attn_sink[l], False)
            xc = _ffn(xc, mod, ctx_row, norm_g[l, 2], w1_b, w2_b, l, 1, 6,
                      mixer=(yc_lru, yc_hy, yc_att, w_out_b))
    return x
```

```python
import functools
import math

import jax
import jax.numpy as jnp
from jax import lax
from jax.experimental import pallas as pl
from jax.experimental.pallas import tpu as pltpu

F32 = jnp.float32
BF16 = jnp.bfloat16

NORM_EPS = 1e-6
N_MOD = 9
MACARON_W = 0.5
D_LRU = 256
LRU_BLOCKS = 4
LRU_C = 8.0
LRU_LEFT = 2
D_HY = 256
HY_LEFT = 1
HY_EMB = 33
HY_BANDS = (HY_EMB - 1) // 2
HY_FAST = 0.3
HY_SLOW = 1.5
HY_TARGET = 1e-2
N_QH = 8
N_KVH = 2
HEAD_DIM = 64
D_ATTN = N_QH * HEAD_DIM
D_KV = N_KVH * HEAD_DIM
WINDOW = 128
ATT_BLOCK = 128
ATT_STEP = 4
GRID_W = 64
ROPE_THETA = 10000.0
ROPE_PAIRS_AXIS = HEAD_DIM // 4
NEG_INF = -1e30
LOG2E = math.log2(math.e)
F32_TINY = 1.1754944e-38

LANES = 128
SUBLANES = 8
KV_WIDTH = 2 * N_KVH * LANES
VMEM_LIMIT_BYTES = 56 * 1024 * 1024
ROW_TILE = 1024
FFN_GROUP_ROWS = 128
FFN_MIXER_GROUP_ROWS = 512
PROJ_GROUP_ROWS = 256
DFT_TILE = 1024
DFT_FILTER_TILE = 512
FFT_BLOCK = 128
FFT_SLAB = 16
FFT_COLS = 512
SCAN_GROUP = 64


def _cparams(*sem):
    return pltpu.CompilerParams(dimension_semantics=sem, vmem_limit_bytes=VMEM_LIMIT_BYTES)


def _row_tile(rows):
    return min(ROW_TILE, rows)


def _row_groups(tile_rows, group_rows):
    return tile_rows // group_rows if tile_rows % group_rows == 0 else 1


def _resident(shape, index=None):
    index = (0,) * len(shape) if index is None else index
    return pl.BlockSpec(shape, lambda *_: index, pipeline_mode=pl.Buffered(1))


def _ada_norm(x, g, shift, scale):
    y = x * lax.rsqrt(jnp.mean(x * x, axis=-1, keepdims=True) + NORM_EPS)
    return (y * g) * (1.0 + scale) + shift


def _sigmoid(x):
    return 0.5 * (1.0 + jnp.tanh(0.5 * x))


def _gelu_tanh(x):
    return 0.5 * x * (1.0 + jnp.tanh(math.sqrt(2.0 / math.pi) * (x + 0.044715 * (x * x * x))))


def _softplus(x):
    return jnp.maximum(x, 0.0) + jnp.log1p(jnp.exp(-jnp.abs(x)))


def _dwconv_tile(x_ref, w, bias, left, i, tile):
    n_rows, width = x_ref.shape
    n_tiles = n_rows // tile
    group = SUBLANES * 4 // x_ref.dtype.itemsize
    r0 = pl.multiple_of(i * tile, tile)
    zeros_i = jnp.zeros((group, width), jnp.int32)
    before = x_ref[pl.ds(pl.multiple_of(jnp.maximum(r0 - group, 0), group), group), :].astype(F32)
    after = x_ref[pl.ds(pl.multiple_of(jnp.minimum(r0 + tile, n_rows - group), group), group), :].astype(F32)
    before = jnp.where(zeros_i + i > 0, before, 0.0)
    after = jnp.where(zeros_i + i < n_tiles - 1, after, 0.0)
    ext = jnp.concatenate([before, x_ref[pl.ds(r0, tile), :].astype(F32), after], axis=0)
    out = jnp.broadcast_to(bias, (tile, width))
    for k in range(w.shape[0]):
        off = k - left
        sh = ext if off == 0 else pltpu.roll(ext, (-off) % ext.shape[0], axis=0)
        out = out + sh[group:group + tile] * w[k:k + 1, :]
    return out


def _mod_kernel(c_ref, w_ref, b_ref, o_ref):
    cv = c_ref[...]
    s = cv * _sigmoid(cv)
    rows = s.shape[0]
    s_hi = s.astype(BF16)
    s_lo = (s - s_hi.astype(F32)).astype(BF16)
    w = w_ref[0]
    w_hi = w.astype(BF16)
    w_lo = (w - w_hi.astype(F32)).astype(BF16)
    p = jnp.dot(jnp.concatenate([s_hi, s_lo], axis=0), w_hi, preferred_element_type=F32)
    o_ref[0] = p[:rows] + p[rows:] + jnp.dot(s_hi, w_lo, preferred_element_type=F32) + b_ref[0]


def _modulation(c_rows, w_mod, b_mod):
    depth, d, nd = w_mod.shape
    rows = c_rows.shape[0]
    tn = nd // 8
    return pl.pallas_call(
        _mod_kernel,
        grid=(depth, nd // tn),
        in_specs=[
            pl.BlockSpec((rows, d), lambda l, j: (0, 0)),
            pl.BlockSpec((1, d, tn), lambda l, j: (l, 0, j)),
            pl.BlockSpec((1, 1, tn), lambda l, j: (l, 0, j)),
        ],
        out_specs=pl.BlockSpec((1, rows, tn), lambda l, j: (l, 0, j)),
        out_shape=jax.ShapeDtypeStruct((depth, rows, nd), F32),
        compiler_params=_cparams("parallel", "parallel"),
        name="modulation",
    )(c_rows, w_mod, b_mod.reshape(depth, 1, nd))


def _ffn_kernel(*refs, i_mod, d_ff, mixer, final):
    refs = list(refs)
    o_ref = refs.pop()
    x_ref, mod_ref, g_ref, w1_ref, w2_ref = refs[:5]
    rest = refs[5:]
    m = mod_ref[...]
    g = g_ref[...]
    tm = x_ref.shape[1]
    n_sub = _row_groups(tm, FFN_MIXER_GROUP_ROWS if mixer else FFN_GROUP_ROWS)
    for n in range(n_sub):
        rows = slice(n * tm // n_sub, (n + 1) * tm // n_sub)
        x = x_ref[0, rows, :]
        if mixer:
            yl_ref, yh_ref, ya_ref, wo_ref = rest[:4]
            y = (jnp.dot(yl_ref[0, rows, :].astype(BF16), wo_ref[0:D_LRU, :], preferred_element_type=F32)
                 + jnp.dot(yh_ref[0, rows, :].astype(BF16), wo_ref[D_LRU:D_LRU + D_HY, :], preferred_element_type=F32)
                 + jnp.dot(ya_ref[0, rows, :].astype(BF16), wo_ref[D_LRU + D_HY:, :], preferred_element_type=F32))
            x = x + m[5:6] * y
        h = _ada_norm(x, g, m[i_mod:i_mod + 1], m[i_mod + 1:i_mod + 2])
        ab = jnp.dot(h.astype(BF16), w1_ref[...], preferred_element_type=F32)
        a = ab[:, :d_ff]
        b = ab[:, d_ff:]
        gated = (a * _sigmoid(a)) * b
        y = jnp.dot(gated.astype(BF16), w2_ref[...], preferred_element_type=F32)
        x = x + (MACARON_W * m[i_mod + 2:i_mod + 3]) * y
        if final:
            x = (x * lax.rsqrt(jnp.mean(x * x, axis=-1, keepdims=True) + NORM_EPS)) * rest[-1][...]
        o_ref[0, rows, :] = x


def _ffn(x, mod, mod_row, g, w1, w2, layer, which, i_mod, mixer=None, final_g=None):
    bsz, rows, d = x.shape
    d_ff = w2.shape[2]
    tm = _row_tile(rows)
    in_specs = [
        pl.BlockSpec((1, tm, d), lambda b, t: (b, t, 0)),
        pl.BlockSpec((None, N_MOD, d), lambda b, t: (mod_row(b), 0, 0)),
        pl.BlockSpec((1, d), lambda b, t: (0, 0)),
        _resident((None, None, d, 2 * d_ff), (layer, which, 0, 0)),
        _resident((None, None, d_ff, d), (layer, which, 0, 0)),
    ]
    args = [x, mod, g.reshape(1, d), w1, w2]
    if mixer is not None:
        y_lru, y_hy, y_att, w_out = mixer
        in_specs += [
            pl.BlockSpec((1, tm, D_LRU), lambda b, t: (b, t, 0)),
            pl.BlockSpec((1, tm, D_HY), lambda b, t: (b, t, 0)),
            pl.BlockSpec((1, tm, D_ATTN), lambda b, t: (b, t, 0)),
            _resident((None,) + w_out.shape[1:], (layer, 0, 0)),
        ]
        args += [y_lru, y_hy, y_att, w_out]
    if final_g is not None:
        in_specs.append(pl.BlockSpec((1, d), lambda b, t: (0, 0)))
        args.append(final_g.reshape(1, d))
    return pl.pallas_call(
        functools.partial(_ffn_kernel, i_mod=i_mod, d_ff=d_ff, mixer=mixer is not None,
                          final=final_g is not None),
        grid=(bsz, rows // tm),
        in_specs=in_specs,
        out_specs=pl.BlockSpec((1, tm, d), lambda b, t: (b, t, 0)),
        out_shape=jax.ShapeDtypeStruct(x.shape, F32),
        compiler_params=_cparams("parallel", "parallel"),
        name="ffn",
    )(*args)


def _rope(x, cos_t, sin_a, sin_b):
    width = x.shape[-1]
    half = HEAD_DIM // 2
    up = pltpu.roll(x, width - half, axis=1)
    dn = pltpu.roll(x, half, axis=1)
    return x * cos_t + up * sin_a + dn * sin_b


def _proj_kernel(*refs, rope):
    if rope:
        (x_ref, mod_ref, g_ref, w_ref, cos_ref, sa_ref, sb_ref,
         xl_ref, gl_ref, zh_ref, q_ref, kv_ref) = refs
    else:
        (x_ref, mod_ref, g_ref, w_ref, xl_ref, gl_ref, zh_ref, q_ref, kv_ref) = refs
    m = mod_ref[...]
    g = g_ref[...]
    tm = x_ref.shape[1]
    n_sub = _row_groups(tm, PROJ_GROUP_ROWS)
    for n in range(n_sub):
        rows = slice(n * tm // n_sub, (n + 1) * tm // n_sub)
        h = _ada_norm(x_ref[0, rows, :], g, m[3:4], m[4:5])
        z = jnp.dot(h.astype(BF16), w_ref[...], preferred_element_type=F32)
        o = 0
        xl_ref[rows, :] = z[:, o:o + D_LRU].astype(BF16); o += D_LRU
        gl_ref[rows, :] = z[:, o:o + D_LRU].astype(BF16); o += D_LRU
        zh_ref[rows, :] = z[:, o:o + 3 * D_HY].astype(BF16); o += 3 * D_HY
        q = z[:, o:o + D_ATTN]; o += D_ATTN
        k = z[:, o:o + D_KV]; o += D_KV
        v = z[:, o:o + D_KV]
        if rope:
            cos_t, sin_a, sin_b = cos_ref[rows, :], sa_ref[rows, :], sb_ref[rows, :]
            q = _rope(q, cos_t, sin_a, sin_b)
            k = _rope(k, cos_t[:, :D_KV], sin_a[:, :D_KV], sin_b[:, :D_KV])
        q_ref[0, rows, :] = (q * (HEAD_DIM ** -0.5 * LOG2E)).astype(BF16)
        low = lax.broadcasted_iota(jnp.int32, k.shape, 1) < HEAD_DIM
        parts = [jnp.where(low, k, 0.0), jnp.where(low, pltpu.roll(k, HEAD_DIM, axis=1), 0.0),
                 jnp.where(low, v, 0.0), jnp.where(low, pltpu.roll(v, HEAD_DIM, axis=1), 0.0)]
        kv_ref[0, rows, :] = jnp.concatenate(parts, axis=1).astype(BF16)


def _input_proj(x, mod, mod_row, g, w_in, layer, rope_tabs):
    bsz, rows, d = x.shape
    tm = _row_tile(rows)
    rope = rope_tabs is not None
    in_specs = [
        pl.BlockSpec((1, tm, d), lambda t, b: (b, t, 0)),
        pl.BlockSpec((None, N_MOD, d), lambda t, b: (mod_row(b), 0, 0)),
        pl.BlockSpec((1, d), lambda t, b: (0, 0)),
        _resident((None,) + w_in.shape[1:], (layer, 0, 0)),
    ]
    args = [x, mod, g.reshape(1, d), w_in]
    if rope:
        in_specs += [pl.BlockSpec((tm, D_ATTN), lambda t, b: (t, 0))] * 3
        args += list(rope_tabs)
    out_shape = [
        jax.ShapeDtypeStruct((rows, bsz * D_LRU), BF16),
        jax.ShapeDtypeStruct((rows, bsz * D_LRU), BF16),
        jax.ShapeDtypeStruct((rows, bsz * 3 * D_HY), BF16),
        jax.ShapeDtypeStruct((bsz, rows, D_ATTN), BF16),
        jax.ShapeDtypeStruct((bsz, rows, KV_WIDTH), BF16),
    ]
    out_specs = [
        pl.BlockSpec((tm, D_LRU), lambda t, b: (t, b)),
        pl.BlockSpec((tm, D_LRU), lambda t, b: (t, b)),
        pl.BlockSpec((tm, 3 * D_HY), lambda t, b: (t, b)),
        pl.BlockSpec((1, tm, D_ATTN), lambda t, b: (b, t, 0)),
        pl.BlockSpec((1, tm, KV_WIDTH), lambda t, b: (b, t, 0)),
    ]
    return pl.pallas_call(
        functools.partial(_proj_kernel, rope=rope),
        grid=(rows // tm, bsz),
        in_specs=in_specs,
        out_specs=out_specs,
        out_shape=out_shape,
        compiler_params=_cparams("parallel", "parallel"),
        name="input_proj",
    )(*args)


def _scan_chunk(a, b, row, reverse):
    for s in (1, 2, 4):
        if reverse:
            ok = row < SUBLANES - s
            sh = SUBLANES - s
        else:
            ok = row >= s
            sh = s
        a_sh = pltpu.roll(a, sh, axis=0)
        b_sh = pltpu.roll(b, sh, axis=0)
        b = jnp.where(ok, a * b_sh + b, b)
        a = jnp.where(ok, a * a_sh, a)
    return a, b


def _lru_kernel(xl_ref, gl_ref, xc_ref, gc_ref, cw_ref, cb_ref, wa_ref, ba_ref, wx_ref, bx_ref,
                lam_ref, y_ref, yc_ref, a_s, b_s, *, n_lat, n_ctx, tile):
    width = xl_ref.shape[1]
    cw = cw_ref[...]
    cb = cb_ref[...]
    neg_c = [(0.5 * LRU_C) * _softplus(-lam_ref[d]) for d in range(2)]

    def coeffs(x_ref, base, n_rows):
        def body(i, carry):
            r0 = pl.multiple_of(i * tile, tile)
            u = _dwconv_tile(x_ref, cw, cb, LRU_LEFT, i, tile)
            ub = u.astype(BF16)
            hu = 0.5 * u
            for d in range(2):
                t_a = jnp.tanh(jnp.dot(ub, wa_ref[d], preferred_element_type=F32) + ba_ref[d])
                t_x = jnp.tanh(jnp.dot(ub, wx_ref[d], preferred_element_type=F32) + bx_ref[d])
                neg_log_a = neg_c[d] + neg_c[d] * t_a
                a = jnp.exp(-neg_log_a)
                dst = pl.ds(pl.multiple_of(base + r0, SUBLANES), tile)
                a_s[d, dst, :] = a
                var = jnp.tanh(neg_log_a) * (1.0 + a * a)
                b_s[d, dst, :] = (var * lax.rsqrt(jnp.maximum(var, F32_TINY))) * (hu + hu * t_x)
            return carry
        lax.fori_loop(0, n_rows // tile, body, 0)

    coeffs(xc_ref, 0, n_ctx)
    coeffs(xl_ref, n_ctx, n_lat)

    row = lax.broadcasted_iota(jnp.int32, (SUBLANES, width), 0)

    def scan_group(d, group, h, reverse):
        r0 = pl.multiple_of(group * SCAN_GROUP, SCAN_GROUP)
        a = a_s[d, pl.ds(r0, SCAN_GROUP), :]
        b = b_s[d, pl.ds(r0, SCAN_GROUP), :]
        order = range(SCAN_GROUP // SUBLANES)
        parts = [_scan_chunk(a[c * SUBLANES:(c + 1) * SUBLANES], b[c * SUBLANES:(c + 1) * SUBLANES], row, reverse)
                 for c in order]
        for c in (reversed(order) if reverse else order):
            hh = parts[c][0] * h + parts[c][1]
            b_s[d, pl.ds(r0 + c * SUBLANES, SUBLANES), :] = hh
            h = hh[0:1, :] if reverse else hh[SUBLANES - 1:SUBLANES, :]
        return h

    ng_ctx = n_ctx // SCAN_GROUP
    ng_all = (n_ctx + n_lat) // SCAN_GROUP
    h0 = jnp.zeros((1, width), F32)

    def ctx_body(j, hs):
        return (scan_group(0, j, hs[0], False), scan_group(1, ng_ctx - 1 - j, hs[1], True))

    def lat_body(j, hs):
        return (scan_group(0, ng_ctx + j, hs[0], False), scan_group(1, ng_all - 1 - j, hs[1], True))

    hs = lax.fori_loop(0, ng_ctx, ctx_body, (h0, h0))
    lax.fori_loop(0, ng_all - ng_ctx, lat_body, hs)

    def finish(o_ref, g_ref, base, n_rows):
        def body(i, carry):
            r0 = pl.multiple_of(i * tile, tile)
            src = pl.ds(pl.multiple_of(base + r0, SUBLANES), tile)
            hsum = b_s[0, src, :] + b_s[1, src, :]
            o_ref[pl.ds(r0, tile), :] = (hsum * _gelu_tanh(g_ref[pl.ds(r0, tile), :].astype(F32))).astype(BF16)
            return carry
        lax.fori_loop(0, n_rows // tile, body, 0)

    finish(yc_ref, gc_ref, 0, n_ctx)
    finish(y_ref, gl_ref, n_ctx, n_lat)


def _block_diag(w):
    two, nb, bs, _ = w.shape
    eye = jnp.eye(nb, dtype=w.dtype)
    return jnp.einsum('dnij,nm->dnimj', w, eye).reshape(two, nb * bs, nb * bs)


def _rglru(xl, gl, xlc, glc, conv_w, conv_b, wa, ba, wx, bx, lam, bsz):
    n_lat, n_ctx = xl.shape[0], xlc.shape[0]
    width = LANES
    per_b = D_LRU // width
    tile = math.gcd(256, math.gcd(n_lat, n_ctx))
    wa_bd = (0.5 * _block_diag(wa)).astype(BF16)
    wx_bd = (0.5 * _block_diag(wx)).astype(BF16)
    ba = 0.5 * ba
    bx = 0.5 * bx
    col = lambda b, j: (0, b * per_b + j)
    par = lambda b, j: (0, j)
    par3 = lambda b, j: (0, 0, j)
    return pl.pallas_call(
        functools.partial(_lru_kernel, n_lat=n_lat, n_ctx=n_ctx, tile=tile),
        grid=(bsz, per_b),
        in_specs=[
            pl.BlockSpec((n_lat, width), col),
            pl.BlockSpec((n_lat, width), col),
            pl.BlockSpec((n_ctx, width), col),
            pl.BlockSpec((n_ctx, width), col),
            pl.BlockSpec((conv_w.shape[0], width), par),
            pl.BlockSpec((1, width), par),
            pl.BlockSpec((2, width, width), lambda b, j: (0, j, j)),
            pl.BlockSpec((2, 1, width), par3),
            pl.BlockSpec((2, width, width), lambda b, j: (0, j, j)),
            pl.BlockSpec((2, 1, width), par3),
            pl.BlockSpec((2, 1, width), par3),
        ],
        out_specs=[pl.BlockSpec((None, n_lat, width), lambda b, j: (b, 0, j)),
                   pl.BlockSpec((None, n_ctx, width), lambda b, j: (b, 0, j))],
        out_shape=[jax.ShapeDtypeStruct((bsz, n_lat, D_LRU), BF16), jax.ShapeDtypeStruct((bsz, n_ctx, D_LRU), BF16)],
        scratch_shapes=[pltpu.VMEM((2, n_ctx + n_lat, width), F32),
                        pltpu.VMEM((2, n_ctx + n_lat, width), F32)],
        compiler_params=_cparams("parallel", "parallel"),
        name="rglru",
    )(xl, gl, xlc, glc, conv_w, conv_b.reshape(1, -1), wa_bd, ba.reshape(2, 1, -1), wx_bd,
      bx.reshape(2, 1, -1), lam.reshape(2, 1, -1))


def _hy_pre_kernel(z0_ref, z1_ref, z2_ref, w0_ref, w1_ref, w2_ref, b0_ref, b1_ref, b2_ref,
                   ub_ref, x0_ref, *, tile):
    w0, w1, w2 = w0_ref[...], w1_ref[...], w2_ref[...]
    b0, b1, b2 = b0_ref[...], b1_ref[...], b2_ref[...]

    def body(i, carry):
        rows = pl.ds(pl.multiple_of(i * tile, tile), tile)
        x1 = _dwconv_tile(z1_ref, w1, b1, HY_LEFT, i, tile)
        v = _dwconv_tile(z2_ref, w2, b2, HY_LEFT, i, tile)
        ub_ref[rows, :] = (x1 * v).astype(BF16)
        x0_ref[rows, :] = _dwconv_tile(z0_ref, w0, b0, HY_LEFT, i, tile).astype(BF16)
        return carry
    lax.fori_loop(0, z0_ref.shape[0] // tile, body, 0)


def _hyena_pre(zh, conv_w, conv_b, bsz):
    rows = zh.shape[0]
    width = LANES
    per_b = D_HY // width
    tile = math.gcd(256, rows)
    zspec = lambda part: pl.BlockSpec((rows, width), lambda b, j: (0, b * 3 * per_b + part * per_b + j))
    wspec = lambda part: pl.BlockSpec((conv_w.shape[0], width), lambda b, j: (0, part * per_b + j))
    bspec = lambda part: pl.BlockSpec((1, width), lambda b, j: (0, part * per_b + j))
    ospec = pl.BlockSpec((rows, width), lambda b, j: (0, b * per_b + j))
    cb = conv_b.reshape(1, -1)
    return pl.pallas_call(
        functools.partial(_hy_pre_kernel, tile=tile),
        grid=(bsz, per_b),
        in_specs=[zspec(0), zspec(1), zspec(2), wspec(0), wspec(1), wspec(2), bspec(0), bspec(1), bspec(2)],
        out_specs=[ospec, ospec],
        out_shape=[jax.ShapeDtypeStruct((rows, bsz * D_HY), BF16),
                   jax.ShapeDtypeStruct((rows, bsz * D_HY), BF16)],
        compiler_params=_cparams("parallel", "parallel"),
        name="hyena_pre",
    )(zh, zh, zh, conv_w, conv_w, conv_w, cb, cb, cb)


def _sin_half_lanes(x):
    rows, width = x.shape
    if 2 * width != LANES or rows % (2 * SUBLANES):
        return jnp.sin(x)
    s = jnp.sin(jnp.concatenate([x[:rows // 2], x[rows // 2:]], axis=1))
    return jnp.concatenate([s[:, :width], s[:, width:]], axis=0)


def _hy_filter_kernel(z_ref, fw0_ref, fb0_ref, fwin_ref, fbin_ref, freq_ref, fwl_ref, dl_ref, o_ref, *, tile):
    hp = lax.Precision.HIGHEST
    z = z_ref[...]
    fr = freq_ref[...]
    hdn = _sin_half_lanes(fr * (jnp.dot(z, fw0_ref[...], preferred_element_type=F32, precision=hp) + fb0_ref[...]))
    for j in range(fwin_ref.shape[0]):
        hdn = _sin_half_lanes(
            fr * (jnp.dot(hdn, fwin_ref[j], preferred_element_type=F32, precision=hp) + fbin_ref[j]))
    k = jnp.dot(hdn, fwl_ref[...], preferred_element_type=F32, precision=hp)
    decay = jnp.exp(-z[:, 0:1] * dl_ref[...])
    k_fwd = k[:, :D_HY] * decay
    k_bwd = k[:, D_HY:] * decay
    row = lax.broadcasted_iota(jnp.int32, k_bwd.shape, 0) + pl.program_id(0) * tile
    k_bwd = jnp.where(row == 0, 0.0, k_bwd)
    o_ref[...] = jnp.concatenate([k_fwd, k_bwd], axis=-1).astype(BF16)


def _hyena_filter_taps(n, fw0, fb0, fw_in, fb_in, freq, fw_last):
    t = jnp.linspace(0.0, 1.0, n, dtype=F32)[:, None]
    w = 2.0 * math.pi * jnp.arange(n, dtype=F32)[:, None] / n
    f = jnp.linspace(1e-4, HY_BANDS - 1, HY_BANDS, dtype=F32)[None, :]
    z = jnp.concatenate([t, jnp.cos(f * w), -jnp.sin(f * w)], axis=-1)
    z = jnp.pad(z, ((0, 0), (0, LANES - HY_EMB)))
    fw0p = jnp.pad(fw0, ((0, LANES - HY_EMB), (0, 0)))
    max_decay = math.log(HY_TARGET) / HY_FAST
    min_decay = math.log(HY_TARGET) / HY_SLOW
    deltas = jnp.abs(jnp.linspace(min_decay, max_decay, D_HY, dtype=F32))[None, :]
    hid = fw0.shape[1]
    tile = min(512, n)
    full = lambda a: pl.BlockSpec(a.shape, lambda i: (0,) * a.ndim)
    args = [fw0p, fb0.reshape(1, hid), fw_in, fb_in.reshape(-1, 1, hid), freq.reshape(1, hid), fw_last, deltas]
    return pl.pallas_call(
        functools.partial(_hy_filter_kernel, tile=tile),
        grid=(n // tile,),
        in_specs=[pl.BlockSpec((tile, LANES), lambda i: (i, 0))] + [full(a) for a in args],
        out_specs=pl.BlockSpec((tile, 2 * D_HY), lambda i: (i, 0)),
        out_shape=jax.ShapeDtypeStruct((n, 2 * D_HY), BF16),
        compiler_params=_cparams("parallel"),
        name="hyena_filter",
    )(z, *args)


def _dft_expand_kernel(ca_ref, sa_ref, cb_ref, sb_ref, cm_ref, sm_ref, *, tile):
    ca, sa, cb, sb = ca_ref[...], sa_ref[...], cb_ref[...], sb_ref[...]
    row = lax.broadcasted_iota(jnp.int32, cb.shape, 0) + pl.program_id(0) * tile
    lane = lax.broadcasted_iota(jnp.int32, cb.shape, 1)
    alt = jnp.where(lane % 2 == 0, 1.0, -1.0)
    for s1 in range(cm_ref.shape[1] // LANES):
        c1 = ca[:, s1:s1 + 1]
        d1 = sa[:, s1:s1 + 1]
        cols = slice(s1 * LANES, (s1 + 1) * LANES)
        cm_ref[:, cols] = (c1 * cb - d1 * sb).astype(BF16)
        sm_ref[:, cols] = jnp.where(row == 0, alt, -(d1 * cb + c1 * sb)).astype(BF16)


def _dft_matrices(n):
    two_n = 2 * n
    f = jnp.arange(n, dtype=jnp.int32)[:, None]
    s_hi = jnp.arange(n // LANES, dtype=jnp.int32)[None, :] * LANES
    s_lo = jnp.arange(LANES, dtype=jnp.int32)[None, :]
    ang_a = ((f * s_hi) % two_n).astype(F32) * (2.0 * math.pi / two_n)
    ang_b = ((f * s_lo) % two_n).astype(F32) * (2.0 * math.pi / two_n)
    tile = min(DFT_TILE, n)
    hi_spec = pl.BlockSpec((tile, n // LANES), lambda i: (i, 0))
    lo_spec = pl.BlockSpec((tile, LANES), lambda i: (i, 0))
    return pl.pallas_call(
        functools.partial(_dft_expand_kernel, tile=tile),
        grid=(n // tile,),
        in_specs=[hi_spec, hi_spec, lo_spec, lo_spec],
        out_specs=[pl.BlockSpec((tile, n), lambda i: (i, 0))] * 2,
        out_shape=[jax.ShapeDtypeStruct((n, n), BF16)] * 2,
        compiler_params=_cparams("parallel"),
        name="dft_matrices",
    )(jnp.cos(ang_a), jnp.sin(ang_a), jnp.cos(ang_b), jnp.sin(ang_b))


def _dft_filter_kernel(cm_ref, sm_ref, kk_ref, kre_ref, kim_ref, *, tile):
    kk = kk_ref[...]
    xre = jnp.dot(cm_ref[...], kk, preferred_element_type=F32)
    xim = jnp.dot(sm_ref[...], kk, preferred_element_type=F32)
    row = lax.broadcasted_iota(jnp.int32, (tile, D_HY), 0) + pl.program_id(0) * tile
    kre_ref[...] = xre[:, :D_HY] + xre[:, D_HY:]
    kim_ref[...] = jnp.where(row == 0, xim[:, :D_HY] + xim[:, D_HY:], xim[:, :D_HY] - xim[:, D_HY:])


def _dft_filter(cm, sm, kk):
    n = cm.shape[0]
    tile = min(DFT_FILTER_TILE, n)
    return pl.pallas_call(
        functools.partial(_dft_filter_kernel, tile=tile),
        grid=(n // tile,),
        in_specs=[pl.BlockSpec((tile, n), lambda f: (f, 0)),
                  pl.BlockSpec((tile, n), lambda f: (f, 0)),
                  _resident(kk.shape)],
        out_specs=[pl.BlockSpec((tile, D_HY), lambda f: (f, 0))] * 2,
        out_shape=[jax.ShapeDtypeStruct((n, D_HY), F32)] * 2,
        compiler_params=_cparams("parallel"),
        name="dft_filter",
    )(cm, sm, kk)


def _dft_fwd_kernel(cm_ref, sm_ref, u_ref, kre_ref, kim_ref, yre_ref, yim_ref, *, tile):
    n_freq = cm_ref.shape[1]
    u = u_ref[...]
    xre = jnp.dot(cm_ref[...], u, preferred_element_type=F32)
    xim = jnp.dot(sm_ref[...], u, preferred_element_type=F32)
    kre = kre_ref[...]
    kim = kim_ref[...]
    row = lax.broadcasted_iota(jnp.int32, xre.shape, 0) + pl.program_id(0) * tile
    first = row == 0
    scale = jnp.where(first, 0.5 / n_freq, 1.0 / n_freq)
    yre_ref[...] = (scale * (xre * kre - jnp.where(first, 0.0, xim * kim))).astype(BF16)
    yim_ref[...] = (scale * jnp.where(first, xim * kim, xre * kim + xim * kre)).astype(BF16)


def _dft_fwd(cm, sm, ub, kre, kim):
    n = cm.shape[0]
    cols = ub.shape[1]
    tile = min(DFT_TILE, n)
    return pl.pallas_call(
        functools.partial(_dft_fwd_kernel, tile=tile),
        grid=(n // tile, cols // D_HY),
        in_specs=[pl.BlockSpec((tile, n), lambda f, c: (f, 0)),
                  pl.BlockSpec((tile, n), lambda f, c: (f, 0)),
                  pl.BlockSpec((n, D_HY), lambda f, c: (0, c)),
                  pl.BlockSpec((tile, D_HY), lambda f, c: (f, 0)),
                  pl.BlockSpec((tile, D_HY), lambda f, c: (f, 0))],
        out_specs=[pl.BlockSpec((tile, D_HY), lambda f, c: (f, c))] * 2,
        out_shape=[jax.ShapeDtypeStruct((n, cols), BF16)] * 2,
        compiler_params=_cparams("parallel", "parallel"),
        name="dft_fwd",
    )(cm, sm, ub, kre, kim)


def _dft_inv_kernel(ci_ref, si_ref, yre_ref, yim_ref, u_ref, x0_ref, skip_ref, o_ref, *, tile):
    y_cos = jnp.dot(ci_ref[...], yre_ref[...], preferred_element_type=F32)
    y_sin = jnp.dot(si_ref[...], yim_ref[...], preferred_element_type=F32)
    row = lax.broadcasted_iota(jnp.int32, y_cos.shape, 0) + pl.program_id(0) * tile
    nyq = jnp.where(row % 2 == 0, 1.0, -1.0) * yim_ref[0:1, :].astype(F32)
    y = y_cos + jnp.where(row == 0, 0.0, y_sin) + nyq
    o_ref[...] = (x0_ref[...] * (y + u_ref[...] * skip_ref[...])).astype(BF16)


def _dft_inv(ci, si, yre, yim, u, x0, skip):
    n = ci.shape[0]
    cols = yre.shape[1]
    tile = min(DFT_TILE, n)
    return pl.pallas_call(
        functools.partial(_dft_inv_kernel, tile=tile),
        grid=(n // tile, cols // D_HY),
        in_specs=[pl.BlockSpec((tile, n), lambda t, c: (t, 0)),
                  pl.BlockSpec((tile, n), lambda t, c: (t, 0)),
                  pl.BlockSpec((n, D_HY), lambda t, c: (0, c)),
                  pl.BlockSpec((n, D_HY), lambda t, c: (0, c)),
                  pl.BlockSpec((tile, D_HY), lambda t, c: (t, c)),
                  pl.BlockSpec((tile, D_HY), lambda t, c: (t, c)),
                  pl.BlockSpec((1, D_HY), lambda t, c: (0, 0))],
        out_specs=pl.BlockSpec((tile, D_HY), lambda t, c: (t, c)),
        out_shape=jax.ShapeDtypeStruct((n, cols), BF16),
        compiler_params=_cparams("parallel", "parallel"),
        name="dft_inv",
    )(ci, si, yre, yim, u, x0, skip.reshape(1, D_HY))


def _fft_tables(n):
    n1 = 2 * n // FFT_BLOCK
    half = n1 // 2
    nf = half + 1
    groups = FFT_BLOCK // FFT_SLAB
    f1 = jnp.arange(nf, dtype=F32)[:, None]
    s1 = jnp.arange(half, dtype=F32)[None, :]
    ang1 = (2.0 * math.pi / n1) * f1 * s1
    eye = jnp.eye(FFT_SLAB, dtype=F32)
    m1 = jnp.concatenate([jnp.kron(jnp.cos(ang1), eye), jnp.kron(-jnp.sin(ang1), eye)], axis=0)
    m1_inv = jnp.concatenate([jnp.kron(jnp.cos(ang1).T, eye), jnp.kron(-jnp.sin(ang1).T, eye)], axis=1)
    k = jnp.arange(FFT_BLOCK, dtype=F32)
    ang2 = (2.0 * math.pi / FFT_BLOCK) * k[:, None] * k[None, :]
    c2, d2 = jnp.cos(ang2), jnp.sin(ang2)
    g2 = jnp.block([[c2, d2], [-d2, c2]])
    g2_inv = jnp.block([[c2, -d2], [d2, c2]])
    psi = (2.0 * math.pi / (2 * n)) * f1 * k[None, :]
    tw = jnp.stack([jnp.cos(psi), jnp.sin(psi)])
    tw_s2 = jnp.broadcast_to(tw[..., None], (2, nf, FFT_BLOCK, LANES))
    tw_s1 = tw.reshape(2, nf, groups, FFT_SLAB).transpose(2, 0, 1, 3).reshape(groups, 2, nf * FFT_SLAB)
    tw_s1 = jnp.broadcast_to(tw_s1[..., None], (groups, 2, nf * FFT_SLAB, LANES))
    return dict(n=n, half=half, nf=nf, m1=m1.astype(BF16), m1_inv=m1_inv.astype(BF16), g2=g2.astype(BF16),
                g2_inv=g2_inv.astype(BF16), tw_s1=tw_s1, tw_s2=tw_s2)


def _lane_tile(x, width):
    return jnp.concatenate([x] * (width // x.shape[-1]), axis=-1) if width != x.shape[-1] else x


def _fft_s1_kernel(u_ref, m_ref, tw_ref, o_ref):
    h, slab, tc = u_ref.shape
    r = jnp.dot(m_ref[...], u_ref[...].reshape(h * slab, tc), preferred_element_type=F32)
    rows = r.shape[0] // 2
    re, im = r[:rows], r[rows:]
    cs = _lane_tile(tw_ref[0], tc)
    sn = _lane_tile(tw_ref[1], tc)
    o_ref[0] = (re * cs + im * sn).astype(BF16).reshape(rows // slab, slab, tc)
    o_ref[1] = (im * cs - re * sn).astype(BF16).reshape(rows // slab, slab, tc)


def _fft_s1(ub, tabs):
    n, cols = ub.shape
    half, nf = tabs["half"], tabs["nf"]
    groups = FFT_BLOCK // FFT_SLAB
    tc = min(FFT_COLS, cols)
    out = pl.pallas_call(
        _fft_s1_kernel,
        grid=(groups, cols // tc),
        in_specs=[pl.BlockSpec((half, None, FFT_SLAB, tc), lambda m, c: (0, m, 0, c)),
                  _resident(tabs["m1"].shape),
                  pl.BlockSpec((None, 2, nf * FFT_SLAB, LANES), lambda m, c: (m, 0, 0, 0))],
        out_specs=pl.BlockSpec((2, nf, None, FFT_SLAB, tc), lambda m, c: (0, 0, m, 0, c)),
        out_shape=jax.ShapeDtypeStruct((2, nf, groups, FFT_SLAB, cols), BF16),
        compiler_params=_cparams("parallel", "parallel"),
        name="fft_stage1",
    )(ub.reshape(half, groups, FFT_SLAB, cols), tabs["m1"], tabs["tw_s1"])
    return out.reshape(2, nf, FFT_BLOCK, cols)


def _fft_mid_kernel(a_ref, g_ref, gi_ref, k_ref, tw_ref, b_ref, *, gf, half, scale):
    g2 = g_ref[...]
    g2_inv = gi_ref[...]
    tc = a_ref.shape[-1]
    for j in range(gf):
        x = jnp.dot(g2, jnp.concatenate([a_ref[0, j], a_ref[1, j]], axis=0), preferred_element_type=F32)
        xre, xim = x[:FFT_BLOCK], x[FFT_BLOCK:]
        kre, kim = k_ref[0, j], k_ref[1, j]
        f1 = pl.program_id(0) * gf + j
        w = jnp.where(jnp.logical_or(f1 == 0, f1 == half), scale, 2.0 * scale)
        y = jnp.concatenate([w * (xre * kre - xim * kim), w * (xre * kim + xim * kre)], axis=0).astype(BF16)
        b = jnp.dot(g2_inv, y, preferred_element_type=F32)
        bre, bim = b[:FFT_BLOCK], b[FFT_BLOCK:]
        cs = _lane_tile(tw_ref[0, j], tc)
        sn = _lane_tile(tw_ref[1, j], tc)
        b_ref[0, j] = (bre * cs - bim * sn).astype(BF16)
        b_ref[1, j] = (bre * sn + bim * cs).astype(BF16)


def _fft_group(nf):
    return max(g for g in range(1, 12) if nf % g == 0)


def _fft_mid(a, tabs, spec):
    _, nf, _, cols = a.shape
    gf = _fft_group(nf)
    blk = lambda w, idx: pl.BlockSpec((2, gf, FFT_BLOCK, w), idx)
    return pl.pallas_call(
        functools.partial(_fft_mid_kernel, gf=gf, half=tabs["half"], scale=0.5 / tabs["n"]),
        grid=(nf // gf, cols // D_HY),
        in_specs=[blk(D_HY, lambda g, c: (0, g, 0, c)), _resident(tabs["g2"].shape),
                  _resident(tabs["g2_inv"].shape), blk(D_HY, lambda g, c: (0, g, 0, 0)),
                  blk(LANES, lambda g, c: (0, g, 0, 0))],
        out_specs=blk(D_HY, lambda g, c: (0, g, 0, c)),
        out_shape=jax.ShapeDtypeStruct(a.shape, BF16),
        compiler_params=_cparams("parallel", "parallel"),
        name="fft_mid",
    )(a, tabs["g2"], tabs["g2_inv"], spec, tabs["tw_s2"])


def _fft_s2_filter_kernel(a_ref, g_ref, k_ref, *, gf):
    g2 = g_ref[...]
    for j in range(gf):
        x = jnp.dot(g2, jnp.concatenate([a_ref[0, j], a_ref[1, j]], axis=0), preferred_element_type=F32)
        xre, xim = x[:FFT_BLOCK], x[FFT_BLOCK:]
        k_ref[0, j] = xre[:, :D_HY] + xre[:, D_HY:]
        k_ref[1, j] = xim[:, :D_HY] - xim[:, D_HY:]


def _fft_s2_filter(a, tabs):
    _, nf, _, cols = a.shape
    gf = _fft_group(nf)
    return pl.pallas_call(
        functools.partial(_fft_s2_filter_kernel, gf=gf),
        grid=(nf // gf,),
        in_specs=[pl.BlockSpec((2, gf, FFT_BLOCK, cols), lambda g: (0, g, 0, 0)), _resident(tabs["g2"].shape)],
        out_specs=pl.BlockSpec((2, gf, FFT_BLOCK, D_HY), lambda g: (0, g, 0, 0)),
        out_shape=jax.ShapeDtypeStruct((2, nf, FFT_BLOCK, D_HY), F32),
        compiler_params=_cparams("parallel"),
        name="fft_stage2_filter",
    )(a, tabs["g2"])


def _fft_s1_inv_kernel(b_ref, m_ref, u_ref, x0_ref, skip_ref, o_ref):
    two, nf, slab, tc = b_ref.shape
    h = u_ref.shape[0]
    y = jnp.dot(m_ref[...], b_ref[...].reshape(two * nf * slab, tc), preferred_element_type=F32)
    u = u_ref[...].reshape(h * slab, tc).astype(F32)
    x0 = x0_ref[...].reshape(h * slab, tc).astype(F32)
    res = (x0 * (y + u * _lane_tile(skip_ref[...], tc))).astype(BF16)
    for bi in range(o_ref.shape[0]):
        o_ref[bi] = res[:, bi * D_HY:(bi + 1) * D_HY].reshape(h, slab, D_HY)


def _fft_s1_inv(b, tabs, u, x0, skip):
    _, nf, _, cols = b.shape
    n, half = tabs["n"], tabs["half"]
    groups = FFT_BLOCK // FFT_SLAB
    tc = min(FFT_COLS, cols)
    rows4 = lambda a: a.reshape(half, groups, FFT_SLAB, cols)
    tspec = pl.BlockSpec((half, None, FFT_SLAB, tc), lambda m, c: (0, m, 0, c))
    out = pl.pallas_call(
        _fft_s1_inv_kernel,
        grid=(groups, cols // tc),
        in_specs=[pl.BlockSpec((2, nf, None, FFT_SLAB, tc), lambda m, c: (0, 0, m, 0, c)),
                  _resident(tabs["m1_inv"].shape), tspec, tspec,
                  pl.BlockSpec((1, D_HY), lambda m, c: (0, 0))],
        out_specs=pl.BlockSpec((tc // D_HY, half, None, FFT_SLAB, D_HY), lambda m, c: (c, 0, m, 0, 0)),
        out_shape=jax.ShapeDtypeStruct((cols // D_HY, half, groups, FFT_SLAB, D_HY), BF16),
        compiler_params=_cparams("parallel", "parallel"),
        name="fft_stage1_inv",
    )(b.reshape(2, nf, groups, FFT_SLAB, cols), tabs["m1_inv"], rows4(u), rows4(x0), skip.reshape(1, D_HY))
    return out.reshape(cols // D_HY, n, D_HY)


def _hyena_fft(zh, bsz, conv_w, conv_b, filt, tabs, skip):
    ub, x0 = _hyena_pre(zh, conv_w, conv_b, bsz)
    kk = _hyena_filter_taps(zh.shape[0], *filt)
    spec = _fft_s2_filter(_fft_s1(kk, tabs), tabs)
    return _fft_s1_inv(_fft_mid(_fft_s1(ub, tabs), tabs, spec), tabs, ub, x0, skip)


def _hyena(zh, bsz, conv_w, conv_b, filt, dft, skip):
    cm, sm = dft
    n = zh.shape[0]
    ub, x0 = _hyena_pre(zh, conv_w, conv_b, bsz)
    kk = _hyena_filter_taps(n, *filt)
    kre, kim = _dft_filter(cm, sm, kk)
    yre, yim = _dft_fwd(cm, sm, ub, kre, kim)
    y = _dft_inv(cm, sm, yre, yim, ub, x0, skip)
    return y.reshape(n, bsz, D_HY).transpose(1, 0, 2)


def _attend_block(q, kv, sink_ref, masks):
    k_low = [kv[:, g * LANES:(g + 1) * LANES] for g in range(N_KVH)]
    ones = jnp.ones((kv.shape[0], LANES), BF16)
    v_aug = [jnp.concatenate([kv[:, (N_KVH + g) * LANES:(N_KVH + g + 1) * LANES], ones], axis=1)
             for g in range(N_KVH)]
    nq = q.shape[0]
    windowed = masks is not None
    if windowed:
        valid_prev, valid_next = masks
    cols_per_g = D_ATTN // LANES // N_KVH
    out_cols = []
    for g in range(N_KVH):
        cols = [q[:, c * LANES:(c + 1) * LANES] for c in range(g * cols_per_g, (g + 1) * cols_per_g)]
        qg = jnp.concatenate(cols + [pltpu.roll(cq, HEAD_DIM, axis=1) for cq in cols], axis=0).astype(BF16)
        s_all = lax.dot_general(qg, k_low[g], (((1,), (1,)), ((), ())), preferred_element_type=F32)
        es, sinks = [], []
        for hb in range(2 * cols_per_g):
            h = 2 * (g * cols_per_g + hb % cols_per_g) + hb // cols_per_g
            s = s_all[hb * nq:(hb + 1) * nq]
            if windowed:
                s = jnp.concatenate([
                    jnp.where(valid_prev, s[:, :ATT_BLOCK], NEG_INF),
                    s[:, ATT_BLOCK:2 * ATT_BLOCK],
                    jnp.where(valid_next, s[:, 2 * ATT_BLOCK:3 * ATT_BLOCK], NEG_INF),
                    s[:, 3 * ATT_BLOCK:]], axis=1)
            sk = sink_ref[h:h + 1, 0:1] * LOG2E
            m = jnp.maximum(jnp.max(s, axis=-1, keepdims=True), sk)
            es.append(jnp.exp2(s - m).astype(BF16))
            sinks.append(jnp.exp2(sk - m))
        o_all = jnp.dot(jnp.concatenate(es, axis=0), v_aug[g], preferred_element_type=F32)
        outs = []
        for hb in range(2 * cols_per_g):
            o = o_all[hb * nq:(hb + 1) * nq]
            outs.append(o[:, :LANES] / (o[:, LANES:] + sinks[hb]))
        out_cols += [outs[ci] + pltpu.roll(outs[cols_per_g + ci], HEAD_DIM, axis=1) for ci in range(cols_per_g)]
    return out_cols


def _attn_kernel(*refs, windowed, n_sub):
    if windowed:
        (q_ref, kvp_ref, kvc_ref, kvn_ref, kvx_ref, sink_ref, o_ref) = refs
        i = pl.program_id(1)
        last = pl.num_programs(1) - 1
        r = lax.broadcasted_iota(jnp.int32, (ATT_BLOCK, ATT_BLOCK), 0)
        j = lax.broadcasted_iota(jnp.int32, (ATT_BLOCK, ATT_BLOCK), 1)
        cur = kvc_ref[0]
        blocks = ([kvp_ref[0]] + [cur[n * ATT_BLOCK:(n + 1) * ATT_BLOCK] for n in range(n_sub)]
                  + [kvn_ref[0]])
    else:
        (q_ref, kvx_ref, sink_ref, o_ref) = refs
    ctx = kvx_ref[0]
    for n in range(n_sub):
        rows = slice(n * ATT_BLOCK, (n + 1) * ATT_BLOCK)
        q = q_ref[0, rows, :].astype(F32)
        if windowed:
            no_prev = jnp.where(i == 0, ATT_BLOCK, 0) if n == 0 else 0
            no_next = jnp.where(i == last, ATT_BLOCK, 0) if n == n_sub - 1 else 0
            masks = (j >= r + no_prev, j <= r - no_next)
            kv = jnp.concatenate(blocks[n:n + 3] + [ctx], axis=0)
        else:
            masks = None
            kv = ctx
        for c, col in enumerate(_attend_block(q, kv, sink_ref, masks)):
            o_ref[0, rows, c * LANES:(c + 1) * LANES] = col.astype(BF16)


def _attention(q, kv, kvx, sink, windowed):
    bsz, lq, _ = q.shape
    n_ctx, kv_w = kvx.shape[1:]
    nb = lq // ATT_BLOCK
    n_sub = max(n for n in range(1, ATT_STEP + 1) if nb % n == 0)
    step = n_sub * ATT_BLOCK
    sink_t = jnp.broadcast_to(sink.reshape(N_QH, 1), (N_QH, LANES))
    qspec = pl.BlockSpec((1, step, D_ATTN), lambda b, i: (b, i, 0))
    xspec = pl.BlockSpec((1, n_ctx, kv_w), lambda b, i: (b, 0, 0))
    sspec = pl.BlockSpec((N_QH, LANES), lambda b, i: (0, 0))
    if windowed:
        prev = pl.BlockSpec((1, ATT_BLOCK, kv_w), lambda b, i: (b, jnp.maximum(n_sub * i - 1, 0), 0))
        cur = pl.BlockSpec((1, step, kv_w), lambda b, i: (b, i, 0))
        nxt = pl.BlockSpec((1, ATT_BLOCK, kv_w), lambda b, i: (b, jnp.minimum(n_sub * (i + 1), nb - 1), 0))
        in_specs = [qspec, prev, cur, nxt, xspec, sspec]
        args = (q, kv, kv, kv, kvx, sink_t)
    else:
        in_specs = [qspec, xspec, sspec]
        args = (q, kvx, sink_t)
    return pl.pallas_call(
        functools.partial(_attn_kernel, windowed=windowed, n_sub=n_sub),
        grid=(bsz, lq // step),
        in_specs=in_specs,
        out_specs=pl.BlockSpec((1, step, D_ATTN), lambda b, i: (b, i, 0)),
        out_shape=jax.ShapeDtypeStruct(q.shape, BF16),
        compiler_params=_cparams("parallel", "parallel"),
        name="attention",
    )(*args)


def _rope_tables(n_lat):
    rows = n_lat // GRID_W
    r = jnp.repeat(jnp.arange(rows, dtype=F32), GRID_W)
    col = jnp.tile(jnp.arange(GRID_W, dtype=F32), rows)
    inv = ROPE_THETA ** (-jnp.arange(ROPE_PAIRS_AXIS, dtype=F32) / ROPE_PAIRS_AXIS)
    ang = jnp.concatenate([r[:, None] * inv, col[:, None] * inv], axis=-1)
    cos, sin = jnp.cos(ang), jnp.sin(ang)
    zero = jnp.zeros_like(sin)
    reps = D_ATTN // HEAD_DIM
    cos_t = jnp.tile(jnp.concatenate([cos, cos], axis=-1), (1, reps))
    sin_a = jnp.tile(jnp.concatenate([-sin, zero], axis=-1), (1, reps))
    sin_b = jnp.tile(jnp.concatenate([zero, sin], axis=-1), (1, reps))
    return cos_t, sin_a, sin_b


def kernel(x, c, ctx, c_ctx, w_mod, b_mod, norm_g, ffn_w1, ffn_w2, w_in, w_out, lru_conv_w, lru_conv_b,
           lru_wa, lru_ba, lru_wx, lru_bx, lru_lam, hy_conv_w, hy_conv_b, hy_fw0, hy_fb0, hy_fw_in,
           hy_fb_in, hy_freq, hy_fw_last, hy_skip, attn_sink, final_g):
    bsz, n_lat, d = x.shape
    n_ctx = ctx.shape[1]
    depth = w_mod.shape[0]
    assert n_lat % ATT_BLOCK == 0 and n_ctx % ATT_BLOCK == 0 and n_lat % GRID_W == 0
    assert ATT_BLOCK % SCAN_GROUP == 0 and ATT_BLOCK % LANES == 0

    mod_rows = -(-(bsz + 1) // (2 * SUBLANES)) * (2 * SUBLANES)
    c_rows = jnp.zeros((mod_rows, d), F32).at[:bsz].set(c).at[bsz].set(c_ctx)
    mod_all = _modulation(c_rows, w_mod, b_mod).reshape(depth, mod_rows, N_MOD, d)
    lat_row = lambda b: b
    ctx_row = lambda b: bsz

    rope_tabs = _rope_tables(n_lat)
    fft_lat = _fft_tables(n_lat)
    w1_b = ffn_w1.astype(BF16)
    w2_b = ffn_w2.astype(BF16)
    w_in_b = w_in.astype(BF16)
    w_out_b = w_out.astype(BF16)

    xc = ctx
    for l in range(depth):
        need_ctx = l < depth - 1
        mod = mod_all[l]
        filt = (hy_fw0[l], hy_fb0[l], hy_fw_in[l], hy_fb_in[l], hy_freq[l], hy_fw_last[l])

        x = _ffn(x, mod, lat_row, norm_g[l, 0], w1_b, w2_b, l, 0, 0)
        xc = _ffn(xc, mod, ctx_row, norm_g[l, 0], w1_b, w2_b, l, 0, 0)

        xl, gl, zh, q, kv = _input_proj(x, mod, lat_row, norm_g[l, 1], w_in_b, l, rope_tabs)
        xlc, glc, zhc, qc, kvc = _input_proj(xc, mod, ctx_row, norm_g[l, 1], w_in_b, l, None)

        y_lru, yc_lru = _rglru(xl, gl, xlc, glc, lru_conv_w[l], lru_conv_b[l], lru_wa[l], lru_ba[l],
                               lru_wx[l], lru_bx[l], lru_lam[l], bsz)
        y_hy = _hyena_fft(zh, bsz, hy_conv_w[l], hy_conv_b[l], filt, fft_lat, hy_skip[l])
        y_att = _attention(q, kv, kvc, attn_sink[l], True)
        x = _ffn(x, mod, lat_row, norm_g[l, 2], w1_b, w2_b, l, 1, 6, mixer=(y_lru, y_hy, y_att, w_out_b),
                 final_g=None if need_ctx else final_g)

        if need_ctx:
            yc_hy = _hyena(zhc, bsz, hy_conv_w[l], hy_conv_b[l], filt, _dft_matrices(n_ctx), hy_skip[l])
            yc_att = _attention(qc, None, kvc, attn_sink[l], False)
            xc = _ffn(xc, mod, ctx_row, norm_g[l, 2], w1_b, w2_b, l, 1, 6,
                      mixer=(yc_lru, yc_hy, yc_att, w_out_b))
    return x
```

```python
import functools
import math

import jax
import jax.numpy as jnp
from jax import lax
from jax.experimental import pallas as pl
from jax.experimental.pallas import tpu as pltpu

F32 = jnp.float32
BF16 = jnp.bfloat16

NORM_EPS = 1e-6
N_MOD = 9
MACARON_W = 0.5
D_LRU = 256
LRU_BLOCKS = 4
LRU_C = 8.0
LRU_LEFT = 2
D_HY = 256
HY_LEFT = 1
HY_EMB = 33
HY_BANDS = (HY_EMB - 1) // 2
HY_FAST = 0.3
HY_SLOW = 1.5
HY_TARGET = 1e-2
N_QH = 8
N_KVH = 2
HEAD_DIM = 64
D_ATTN = N_QH * HEAD_DIM
D_KV = N_KVH * HEAD_DIM
WINDOW = 128
ATT_BLOCK = 128
ATT_STEP = 4
GRID_W = 64
ROPE_THETA = 10000.0
ROPE_PAIRS_AXIS = HEAD_DIM // 4
NEG_INF = -1e30
LOG2E = math.log2(math.e)
F32_TINY = 1.1754944e-38

LANES = 128
SUBLANES = 8
KV_WIDTH = 2 * N_KVH * LANES
VMEM_LIMIT_BYTES = 56 * 1024 * 1024
ROW_TILE = 1024
FFN_GROUP_ROWS = 128
FFN_MIXER_GROUP_ROWS = 512
PROJ_GROUP_ROWS = 256
DFT_TILE = 1024
DFT_FILTER_TILE = 512
FFT_BLOCK = 128
FFT_SLAB = 16
FFT_COLS = 1024
SCAN_GROUP = 64


def _cparams(*sem):
    return pltpu.CompilerParams(dimension_semantics=sem, vmem_limit_bytes=VMEM_LIMIT_BYTES)


def _row_tile(rows):
    return min(ROW_TILE, rows)


def _row_groups(tile_rows, group_rows):
    return tile_rows // group_rows if tile_rows % group_rows == 0 else 1


def _resident(shape, index=None):
    index = (0,) * len(shape) if index is None else index
    return pl.BlockSpec(shape, lambda *_: index, pipeline_mode=pl.Buffered(1))


def _ada_norm(x, g, shift, scale):
    y = x * lax.rsqrt(jnp.mean(x * x, axis=-1, keepdims=True) + NORM_EPS)
    return (y * g) * (1.0 + scale) + shift


def _sigmoid(x):
    return 0.5 * (1.0 + jnp.tanh(0.5 * x))


def _gelu_tanh(x):
    return 0.5 * x * (1.0 + jnp.tanh(math.sqrt(2.0 / math.pi) * (x + 0.044715 * (x * x * x))))


def _softplus(x):
    return jnp.maximum(x, 0.0) + jnp.log1p(jnp.exp(-jnp.abs(x)))


def _dwconv_tile(x_ref, w, bias, left, i, tile):
    n_rows, width = x_ref.shape
    n_tiles = n_rows // tile
    group = SUBLANES * 4 // x_ref.dtype.itemsize
    r0 = pl.multiple_of(i * tile, tile)
    zeros_i = jnp.zeros((group, width), jnp.int32)
    before = x_ref[pl.ds(pl.multiple_of(jnp.maximum(r0 - group, 0), group), group), :].astype(F32)
    after = x_ref[pl.ds(pl.multiple_of(jnp.minimum(r0 + tile, n_rows - group), group), group), :].astype(F32)
    before = jnp.where(zeros_i + i > 0, before, 0.0)
    after = jnp.where(zeros_i + i < n_tiles - 1, after, 0.0)
    ext = jnp.concatenate([before, x_ref[pl.ds(r0, tile), :].astype(F32), after], axis=0)
    out = jnp.broadcast_to(bias, (tile, width))
    for k in range(w.shape[0]):
        off = k - left
        sh = ext if off == 0 else pltpu.roll(ext, (-off) % ext.shape[0], axis=0)
        out = out + sh[group:group + tile] * w[k:k + 1, :]
    return out


def _mod_kernel(c_ref, w_ref, b_ref, o_ref):
    cv = c_ref[...]
    s = cv * _sigmoid(cv)
    rows = s.shape[0]
    s_hi = s.astype(BF16)
    s_lo = (s - s_hi.astype(F32)).astype(BF16)
    w = w_ref[0]
    w_hi = w.astype(BF16)
    w_lo = (w - w_hi.astype(F32)).astype(BF16)
    p = jnp.dot(jnp.concatenate([s_hi, s_lo], axis=0), w_hi, preferred_element_type=F32)
    o_ref[0] = p[:rows] + p[rows:] + jnp.dot(s_hi, w_lo, preferred_element_type=F32) + b_ref[0]


def _modulation(c_rows, w_mod, b_mod):
    depth, d, nd = w_mod.shape
    rows = c_rows.shape[0]
    tn = nd // 8
    return pl.pallas_call(
        _mod_kernel,
        grid=(depth, nd // tn),
        in_specs=[
            pl.BlockSpec((rows, d), lambda l, j: (0, 0)),
            pl.BlockSpec((1, d, tn), lambda l, j: (l, 0, j)),
            pl.BlockSpec((1, 1, tn), lambda l, j: (l, 0, j)),
        ],
        out_specs=pl.BlockSpec((1, rows, tn), lambda l, j: (l, 0, j)),
        out_shape=jax.ShapeDtypeStruct((depth, rows, nd), F32),
        compiler_params=_cparams("parallel", "parallel"),
        name="modulation",
    )(c_rows, w_mod, b_mod.reshape(depth, 1, nd))


def _ffn_kernel(*refs, i_mod, d_ff, mixer, final):
    refs = list(refs)
    o_ref = refs.pop()
    x_ref, mod_ref, g_ref, w1_ref, w2_ref = refs[:5]
    rest = refs[5:]
    m = mod_ref[...]
    g = g_ref[...]
    tm = x_ref.shape[1]
    n_sub = _row_groups(tm, FFN_MIXER_GROUP_ROWS if mixer else FFN_GROUP_ROWS)
    for n in range(n_sub):
        rows = slice(n * tm // n_sub, (n + 1) * tm // n_sub)
        x = x_ref[0, rows, :]
        if mixer:
            yl_ref, yh_ref, ya_ref, wo_ref = rest[:4]
            y = (jnp.dot(yl_ref[0, rows, :].astype(BF16), wo_ref[0:D_LRU, :], preferred_element_type=F32)
                 + jnp.dot(yh_ref[0, rows, :].astype(BF16), wo_ref[D_LRU:D_LRU + D_HY, :], preferred_element_type=F32)
                 + jnp.dot(ya_ref[0, rows, :].astype(BF16), wo_ref[D_LRU + D_HY:, :], preferred_element_type=F32))
            x = x + m[5:6] * y
        h = _ada_norm(x, g, m[i_mod:i_mod + 1], m[i_mod + 1:i_mod + 2])
        ab = jnp.dot(h.astype(BF16), w1_ref[...], preferred_element_type=F32)
        a = ab[:, :d_ff]
        b = ab[:, d_ff:]
        gated = (a * _sigmoid(a)) * b
        y = jnp.dot(gated.astype(BF16), w2_ref[...], preferred_element_type=F32)
        x = x + (MACARON_W * m[i_mod + 2:i_mod + 3]) * y
        if final:
            x = (x * lax.rsqrt(jnp.mean(x * x, axis=-1, keepdims=True) + NORM_EPS)) * rest[-1][...]
        o_ref[0, rows, :] = x


def _ffn(x, mod, mod_row, g, w1, w2, layer, which, i_mod, mixer=None, final_g=None):
    bsz, rows, d = x.shape
    d_ff = w2.shape[2]
    tm = _row_tile(rows)
    in_specs = [
        pl.BlockSpec((1, tm, d), lambda b, t: (b, t, 0)),
        pl.BlockSpec((None, N_MOD, d), lambda b, t: (mod_row(b), 0, 0)),
        pl.BlockSpec((1, d), lambda b, t: (0, 0)),
        _resident((None, None, d, 2 * d_ff), (layer, which, 0, 0)),
        _resident((None, None, d_ff, d), (layer, which, 0, 0)),
    ]
    args = [x, mod, g.reshape(1, d), w1, w2]
    if mixer is not None:
        y_lru, y_hy, y_att, w_out = mixer
        in_specs += [
            pl.BlockSpec((1, tm, D_LRU), lambda b, t: (b, t, 0)),
            pl.BlockSpec((1, tm, D_HY), lambda b, t: (b, t, 0)),
            pl.BlockSpec((1, tm, D_ATTN), lambda b, t: (b, t, 0)),
            _resident((None,) + w_out.shape[1:], (layer, 0, 0)),
        ]
        args += [y_lru, y_hy, y_att, w_out]
    if final_g is not None:
        in_specs.append(pl.BlockSpec((1, d), lambda b, t: (0, 0)))
        args.append(final_g.reshape(1, d))
    return pl.pallas_call(
        functools.partial(_ffn_kernel, i_mod=i_mod, d_ff=d_ff, mixer=mixer is not None,
                          final=final_g is not None),
        grid=(bsz, rows // tm),
        in_specs=in_specs,
        out_specs=pl.BlockSpec((1, tm, d), lambda b, t: (b, t, 0)),
        out_shape=jax.ShapeDtypeStruct(x.shape, F32),
        compiler_params=_cparams("parallel", "parallel"),
        name="ffn",
    )(*args)


def _rope(x, cos_t, sin_a, sin_b):
    width = x.shape[-1]
    half = HEAD_DIM // 2
    up = pltpu.roll(x, width - half, axis=1)
    dn = pltpu.roll(x, half, axis=1)
    return x * cos_t + up * sin_a + dn * sin_b


def _proj_kernel(*refs, rope):
    if rope:
        (x_ref, mod_ref, g_ref, w_ref, cos_ref, sa_ref, sb_ref,
         xl_ref, gl_ref, zh_ref, q_ref, kv_ref) = refs
    else:
        (x_ref, mod_ref, g_ref, w_ref, xl_ref, gl_ref, zh_ref, q_ref, kv_ref) = refs
    m = mod_ref[...]
    g = g_ref[...]
    tm = x_ref.shape[1]
    n_sub = _row_groups(tm, PROJ_GROUP_ROWS)
    for n in range(n_sub):
        rows = slice(n * tm // n_sub, (n + 1) * tm // n_sub)
        h = _ada_norm(x_ref[0, rows, :], g, m[3:4], m[4:5])
        z = jnp.dot(h.astype(BF16), w_ref[...], preferred_element_type=F32)
        o = 0
        xl_ref[rows, :] = z[:, o:o + D_LRU].astype(BF16); o += D_LRU
        gl_ref[rows, :] = z[:, o:o + D_LRU].astype(BF16); o += D_LRU
        zh_ref[rows, :] = z[:, o:o + 3 * D_HY].astype(BF16); o += 3 * D_HY
        q = z[:, o:o + D_ATTN]; o += D_ATTN
        k = z[:, o:o + D_KV]; o += D_KV
        v = z[:, o:o + D_KV]
        if rope:
            cos_t, sin_a, sin_b = cos_ref[rows, :], sa_ref[rows, :], sb_ref[rows, :]
            q = _rope(q, cos_t, sin_a, sin_b)
            k = _rope(k, cos_t[:, :D_KV], sin_a[:, :D_KV], sin_b[:, :D_KV])
        q_ref[0, rows, :] = (q * (HEAD_DIM ** -0.5 * LOG2E)).astype(BF16)
        low = lax.broadcasted_iota(jnp.int32, k.shape, 1) < HEAD_DIM
        parts = [jnp.where(low, k, 0.0), jnp.where(low, pltpu.roll(k, HEAD_DIM, axis=1), 0.0),
                 jnp.where(low, v, 0.0), jnp.where(low, pltpu.roll(v, HEAD_DIM, axis=1), 0.0)]
        kv_ref[0, rows, :] = jnp.concatenate(parts, axis=1).astype(BF16)


def _input_proj(x, mod, mod_row, g, w_in, layer, rope_tabs):
    bsz, rows, d = x.shape
    tm = _row_tile(rows)
    rope = rope_tabs is not None
    in_specs = [
        pl.BlockSpec((1, tm, d), lambda t, b: (b, t, 0)),
        pl.BlockSpec((None, N_MOD, d), lambda t, b: (mod_row(b), 0, 0)),
        pl.BlockSpec((1, d), lambda t, b: (0, 0)),
        _resident((None,) + w_in.shape[1:], (layer, 0, 0)),
    ]
    args = [x, mod, g.reshape(1, d), w_in]
    if rope:
        in_specs += [pl.BlockSpec((tm, D_ATTN), lambda t, b: (t, 0))] * 3
        args += list(rope_tabs)
    out_shape = [
        jax.ShapeDtypeStruct((rows, bsz * D_LRU), BF16),
        jax.ShapeDtypeStruct((rows, bsz * D_LRU), BF16),
        jax.ShapeDtypeStruct((rows, bsz * 3 * D_HY), BF16),
        jax.ShapeDtypeStruct((bsz, rows, D_ATTN), BF16),
        jax.ShapeDtypeStruct((bsz, rows, KV_WIDTH), BF16),
    ]
    out_specs = [
        pl.BlockSpec((tm, D_LRU), lambda t, b: (t, b)),
        pl.BlockSpec((tm, D_LRU), lambda t, b: (t, b)),
        pl.BlockSpec((tm, 3 * D_HY), lambda t, b: (t, b)),
        pl.BlockSpec((1, tm, D_ATTN), lambda t, b: (b, t, 0)),
        pl.BlockSpec((1, tm, KV_WIDTH), lambda t, b: (b, t, 0)),
    ]
    return pl.pallas_call(
        functools.partial(_proj_kernel, rope=rope),
        grid=(rows // tm, bsz),
        in_specs=in_specs,
        out_specs=out_specs,
        out_shape=out_shape,
        compiler_params=_cparams("parallel", "parallel"),
        name="input_proj",
    )(*args)


def _scan_chunk(a, b, row, reverse):
    for s in (1, 2, 4):
        if reverse:
            ok = row < SUBLANES - s
            sh = SUBLANES - s
        else:
            ok = row >= s
            sh = s
        a_sh = pltpu.roll(a, sh, axis=0)
        b_sh = pltpu.roll(b, sh, axis=0)
        b = jnp.where(ok, a * b_sh + b, b)
        a = jnp.where(ok, a * a_sh, a)
    return a, b


def _lru_kernel(xl_ref, gl_ref, xc_ref, gc_ref, cw_ref, cb_ref, wa_ref, ba_ref, wx_ref, bx_ref,
                lam_ref, y_ref, yc_ref, a_s, b_s, *, n_lat, n_ctx, tile):
    width = xl_ref.shape[1]
    cw = cw_ref[...]
    cb = cb_ref[...]
    neg_c = [(0.5 * LRU_C) * _softplus(-lam_ref[d]) for d in range(2)]

    def coeffs(x_ref, base, n_rows):
        def body(i, carry):
            r0 = pl.multiple_of(i * tile, tile)
            u = _dwconv_tile(x_ref, cw, cb, LRU_LEFT, i, tile)
            ub = u.astype(BF16)
            hu = 0.5 * u
            for d in range(2):
                t_a = jnp.tanh(jnp.dot(ub, wa_ref[d], preferred_element_type=F32) + ba_ref[d])
                t_x = jnp.tanh(jnp.dot(ub, wx_ref[d], preferred_element_type=F32) + bx_ref[d])
                neg_log_a = neg_c[d] + neg_c[d] * t_a
                a = jnp.exp(-neg_log_a)
                dst = pl.ds(pl.multiple_of(base + r0, SUBLANES), tile)
                a_s[d, dst, :] = a
                var = jnp.tanh(neg_log_a) * (1.0 + a * a)
                b_s[d, dst, :] = (var * lax.rsqrt(jnp.maximum(var, F32_TINY))) * (hu + hu * t_x)
            return carry
        lax.fori_loop(0, n_rows // tile, body, 0)

    coeffs(xc_ref, 0, n_ctx)
    coeffs(xl_ref, n_ctx, n_lat)

    row = lax.broadcasted_iota(jnp.int32, (SUBLANES, width), 0)

    def scan_group(d, group, h, reverse):
        r0 = pl.multiple_of(group * SCAN_GROUP, SCAN_GROUP)
        a = a_s[d, pl.ds(r0, SCAN_GROUP), :]
        b = b_s[d, pl.ds(r0, SCAN_GROUP), :]
        order = range(SCAN_GROUP // SUBLANES)
        parts = [_scan_chunk(a[c * SUBLANES:(c + 1) * SUBLANES], b[c * SUBLANES:(c + 1) * SUBLANES], row, reverse)
                 for c in order]
        for c in (reversed(order) if reverse else order):
            hh = parts[c][0] * h + parts[c][1]
            b_s[d, pl.ds(r0 + c * SUBLANES, SUBLANES), :] = hh
            h = hh[0:1, :] if reverse else hh[SUBLANES - 1:SUBLANES, :]
        return h

    ng_ctx = n_ctx // SCAN_GROUP
    ng_all = (n_ctx + n_lat) // SCAN_GROUP
    h0 = jnp.zeros((1, width), F32)

    def ctx_body(j, hs):
        return (scan_group(0, j, hs[0], False), scan_group(1, ng_ctx - 1 - j, hs[1], True))

    def lat_body(j, hs):
        return (scan_group(0, ng_ctx + j, hs[0], False), scan_group(1, ng_all - 1 - j, hs[1], True))

    hs = lax.fori_loop(0, ng_ctx, ctx_body, (h0, h0))
    lax.fori_loop(0, ng_all - ng_ctx, lat_body, hs)

    def finish(o_ref, g_ref, base, n_rows):
        def body(i, carry):
            r0 = pl.multiple_of(i * tile, tile)
            src = pl.ds(pl.multiple_of(base + r0, SUBLANES), tile)
            hsum = b_s[0, src, :] + b_s[1, src, :]
            o_ref[pl.ds(r0, tile), :] = (hsum * _gelu_tanh(g_ref[pl.ds(r0, tile), :].astype(F32))).astype(BF16)
            return carry
        lax.fori_loop(0, n_rows // tile, body, 0)

    finish(yc_ref, gc_ref, 0, n_ctx)
    finish(y_ref, gl_ref, n_ctx, n_lat)


def _block_diag(w):
    two, nb, bs, _ = w.shape
    eye = jnp.eye(nb, dtype=w.dtype)
    return jnp.einsum('dnij,nm->dnimj', w, eye).reshape(two, nb * bs, nb * bs)


def _rglru(xl, gl, xlc, glc, conv_w, conv_b, wa, ba, wx, bx, lam, bsz):
    n_lat, n_ctx = xl.shape[0], xlc.shape[0]
    width = LANES
    per_b = D_LRU // width
    tile = math.gcd(256, math.gcd(n_lat, n_ctx))
    wa_bd = (0.5 * _block_diag(wa)).astype(BF16)
    wx_bd = (0.5 * _block_diag(wx)).astype(BF16)
    ba = 0.5 * ba
    bx = 0.5 * bx
    col = lambda b, j: (0, b * per_b + j)
    par = lambda b, j: (0, j)
    par3 = lambda b, j: (0, 0, j)
    return pl.pallas_call(
        functools.partial(_lru_kernel, n_lat=n_lat, n_ctx=n_ctx, tile=tile),
        grid=(bsz, per_b),
        in_specs=[
            pl.BlockSpec((n_lat, width), col),
            pl.BlockSpec((n_lat, width), col),
            pl.BlockSpec((n_ctx, width), col),
            pl.BlockSpec((n_ctx, width), col),
            pl.BlockSpec((conv_w.shape[0], width), par),
            pl.BlockSpec((1, width), par),
            pl.BlockSpec((2, width, width), lambda b, j: (0, j, j)),
            pl.BlockSpec((2, 1, width), par3),
            pl.BlockSpec((2, width, width), lambda b, j: (0, j, j)),
            pl.BlockSpec((2, 1, width), par3),
            pl.BlockSpec((2, 1, width), par3),
        ],
        out_specs=[pl.BlockSpec((None, n_lat, width), lambda b, j: (b, 0, j)),
                   pl.BlockSpec((None, n_ctx, width), lambda b, j: (b, 0, j))],
        out_shape=[jax.ShapeDtypeStruct((bsz, n_lat, D_LRU), BF16), jax.ShapeDtypeStruct((bsz, n_ctx, D_LRU), BF16)],
        scratch_shapes=[pltpu.VMEM((2, n_ctx + n_lat, width), F32),
                        pltpu.VMEM((2, n_ctx + n_lat, width), F32)],
        compiler_params=_cparams("parallel", "parallel"),
        name="rglru",
    )(xl, gl, xlc, glc, conv_w, conv_b.reshape(1, -1), wa_bd, ba.reshape(2, 1, -1), wx_bd,
      bx.reshape(2, 1, -1), lam.reshape(2, 1, -1))


def _hy_pre_kernel(z0_ref, z1_ref, z2_ref, w0_ref, w1_ref, w2_ref, b0_ref, b1_ref, b2_ref,
                   ub_ref, x0_ref, *, tile):
    w0, w1, w2 = w0_ref[...], w1_ref[...], w2_ref[...]
    b0, b1, b2 = b0_ref[...], b1_ref[...], b2_ref[...]

    def body(i, carry):
        rows = pl.ds(pl.multiple_of(i * tile, tile), tile)
        x1 = _dwconv_tile(z1_ref, w1, b1, HY_LEFT, i, tile)
        v = _dwconv_tile(z2_ref, w2, b2, HY_LEFT, i, tile)
        ub_ref[rows, :] = (x1 * v).astype(BF16)
        x0_ref[rows, :] = _dwconv_tile(z0_ref, w0, b0, HY_LEFT, i, tile).astype(BF16)
        return carry
    lax.fori_loop(0, z0_ref.shape[0] // tile, body, 0)


def _hyena_pre(zh, conv_w, conv_b, bsz):
    rows = zh.shape[0]
    width = LANES
    per_b = D_HY // width
    tile = math.gcd(256, rows)
    zspec = lambda part: pl.BlockSpec((rows, width), lambda b, j: (0, b * 3 * per_b + part * per_b + j))
    wspec = lambda part: pl.BlockSpec((conv_w.shape[0], width), lambda b, j: (0, part * per_b + j))
    bspec = lambda part: pl.BlockSpec((1, width), lambda b, j: (0, part * per_b + j))
    ospec = pl.BlockSpec((rows, width), lambda b, j: (0, b * per_b + j))
    cb = conv_b.reshape(1, -1)
    return pl.pallas_call(
        functools.partial(_hy_pre_kernel, tile=tile),
        grid=(bsz, per_b),
        in_specs=[zspec(0), zspec(1), zspec(2), wspec(0), wspec(1), wspec(2), bspec(0), bspec(1), bspec(2)],
        out_specs=[ospec, ospec],
        out_shape=[jax.ShapeDtypeStruct((rows, bsz * D_HY), BF16),
                   jax.ShapeDtypeStruct((rows, bsz * D_HY), BF16)],
        compiler_params=_cparams("parallel", "parallel"),
        name="hyena_pre",
    )(zh, zh, zh, conv_w, conv_w, conv_w, cb, cb, cb)


def _sin_half_lanes(x):
    rows, width = x.shape
    if 2 * width != LANES or rows % (2 * SUBLANES):
        return jnp.sin(x)
    s = jnp.sin(jnp.concatenate([x[:rows // 2], x[rows // 2:]], axis=1))
    return jnp.concatenate([s[:, :width], s[:, width:]], axis=0)


def _hy_filter_kernel(z_ref, fw0_ref, fb0_ref, fwin_ref, fbin_ref, freq_ref, fwl_ref, dl_ref, o_ref, *, tile):
    hp = lax.Precision.HIGHEST
    z = z_ref[...]
    fr = freq_ref[...]
    hdn = _sin_half_lanes(fr * (jnp.dot(z, fw0_ref[...], preferred_element_type=F32, precision=hp) + fb0_ref[...]))
    for j in range(fwin_ref.shape[0]):
        hdn = _sin_half_lanes(
            fr * (jnp.dot(hdn, fwin_ref[j], preferred_element_type=F32, precision=hp) + fbin_ref[j]))
    k = jnp.dot(hdn, fwl_ref[...], preferred_element_type=F32, precision=hp)
    decay = jnp.exp(-z[:, 0:1] * dl_ref[...])
    k_fwd = k[:, :D_HY] * decay
    k_bwd = k[:, D_HY:] * decay
    row = lax.broadcasted_iota(jnp.int32, k_bwd.shape, 0) + pl.program_id(0) * tile
    k_bwd = jnp.where(row == 0, 0.0, k_bwd)
    o_ref[...] = jnp.concatenate([k_fwd, k_bwd], axis=-1).astype(BF16)


def _hyena_filter_taps(n, fw0, fb0, fw_in, fb_in, freq, fw_last):
    t = jnp.linspace(0.0, 1.0, n, dtype=F32)[:, None]
    w = 2.0 * math.pi * jnp.arange(n, dtype=F32)[:, None] / n
    f = jnp.linspace(1e-4, HY_BANDS - 1, HY_BANDS, dtype=F32)[None, :]
    z = jnp.concatenate([t, jnp.cos(f * w), -jnp.sin(f * w)], axis=-1)
    z = jnp.pad(z, ((0, 0), (0, LANES - HY_EMB)))
    fw0p = jnp.pad(fw0, ((0, LANES - HY_EMB), (0, 0)))
    max_decay = math.log(HY_TARGET) / HY_FAST
    min_decay = math.log(HY_TARGET) / HY_SLOW
    deltas = jnp.abs(jnp.linspace(min_decay, max_decay, D_HY, dtype=F32))[None, :]
    hid = fw0.shape[1]
    tile = min(512, n)
    full = lambda a: pl.BlockSpec(a.shape, lambda i: (0,) * a.ndim)
    args = [fw0p, fb0.reshape(1, hid), fw_in, fb_in.reshape(-1, 1, hid), freq.reshape(1, hid), fw_last, deltas]
    return pl.pallas_call(
        functools.partial(_hy_filter_kernel, tile=tile),
        grid=(n // tile,),
        in_specs=[pl.BlockSpec((tile, LANES), lambda i: (i, 0))] + [full(a) for a in args],
        out_specs=pl.BlockSpec((tile, 2 * D_HY), lambda i: (i, 0)),
        out_shape=jax.ShapeDtypeStruct((n, 2 * D_HY), BF16),
        compiler_params=_cparams("parallel"),
        name="hyena_filter",
    )(z, *args)


def _dft_expand_kernel(ca_ref, sa_ref, cb_ref, sb_ref, cm_ref, sm_ref, *, tile):
    ca, sa, cb, sb = ca_ref[...], sa_ref[...], cb_ref[...], sb_ref[...]
    row = lax.broadcasted_iota(jnp.int32, cb.shape, 0) + pl.program_id(0) * tile
    lane = lax.broadcasted_iota(jnp.int32, cb.shape, 1)
    alt = jnp.where(lane % 2 == 0, 1.0, -1.0)
    for s1 in range(cm_ref.shape[1] // LANES):
        c1 = ca[:, s1:s1 + 1]
        d1 = sa[:, s1:s1 + 1]
        cols = slice(s1 * LANES, (s1 + 1) * LANES)
        cm_ref[:, cols] = (c1 * cb - d1 * sb).astype(BF16)
        sm_ref[:, cols] = jnp.where(row == 0, alt, -(d1 * cb + c1 * sb)).astype(BF16)


def _dft_matrices(n):
    two_n = 2 * n
    f = jnp.arange(n, dtype=jnp.int32)[:, None]
    s_hi = jnp.arange(n // LANES, dtype=jnp.int32)[None, :] * LANES
    s_lo = jnp.arange(LANES, dtype=jnp.int32)[None, :]
    ang_a = ((f * s_hi) % two_n).astype(F32) * (2.0 * math.pi / two_n)
    ang_b = ((f * s_lo) % two_n).astype(F32) * (2.0 * math.pi / two_n)
    tile = min(DFT_TILE, n)
    hi_spec = pl.BlockSpec((tile, n // LANES), lambda i: (i, 0))
    lo_spec = pl.BlockSpec((tile, LANES), lambda i: (i, 0))
    return pl.pallas_call(
        functools.partial(_dft_expand_kernel, tile=tile),
        grid=(n // tile,),
        in_specs=[hi_spec, hi_spec, lo_spec, lo_spec],
        out_specs=[pl.BlockSpec((tile, n), lambda i: (i, 0))] * 2,
        out_shape=[jax.ShapeDtypeStruct((n, n), BF16)] * 2,
        compiler_params=_cparams("parallel"),
        name="dft_matrices",
    )(jnp.cos(ang_a), jnp.sin(ang_a), jnp.cos(ang_b), jnp.sin(ang_b))


def _dft_filter_kernel(cm_ref, sm_ref, kk_ref, kre_ref, kim_ref, *, tile):
    kk = kk_ref[...]
    xre = jnp.dot(cm_ref[...], kk, preferred_element_type=F32)
    xim = jnp.dot(sm_ref[...], kk, preferred_element_type=F32)
    row = lax.broadcasted_iota(jnp.int32, (tile, D_HY), 0) + pl.program_id(0) * tile
    kre_ref[...] = xre[:, :D_HY] + xre[:, D_HY:]
    kim_ref[...] = jnp.where(row == 0, xim[:, :D_HY] + xim[:, D_HY:], xim[:, :D_HY] - xim[:, D_HY:])


def _dft_filter(cm, sm, kk):
    n = cm.shape[0]
    tile = min(DFT_FILTER_TILE, n)
    return pl.pallas_call(
        functools.partial(_dft_filter_kernel, tile=tile),
        grid=(n // tile,),
        in_specs=[pl.BlockSpec((tile, n), lambda f: (f, 0)),
                  pl.BlockSpec((tile, n), lambda f: (f, 0)),
                  _resident(kk.shape)],
        out_specs=[pl.BlockSpec((tile, D_HY), lambda f: (f, 0))] * 2,
        out_shape=[jax.ShapeDtypeStruct((n, D_HY), F32)] * 2,
        compiler_params=_cparams("parallel"),
        name="dft_filter",
    )(cm, sm, kk)


def _dft_fwd_kernel(cm_ref, sm_ref, u_ref, kre_ref, kim_ref, yre_ref, yim_ref, *, tile):
    n_freq = cm_ref.shape[1]
    u = u_ref[...]
    xre = jnp.dot(cm_ref[...], u, preferred_element_type=F32)
    xim = jnp.dot(sm_ref[...], u, preferred_element_type=F32)
    kre = kre_ref[...]
    kim = kim_ref[...]
    row = lax.broadcasted_iota(jnp.int32, xre.shape, 0) + pl.program_id(0) * tile
    first = row == 0
    scale = jnp.where(first, 0.5 / n_freq, 1.0 / n_freq)
    yre_ref[...] = (scale * (xre * kre - jnp.where(first, 0.0, xim * kim))).astype(BF16)
    yim_ref[...] = (scale * jnp.where(first, xim * kim, xre * kim + xim * kre)).astype(BF16)


def _dft_fwd(cm, sm, ub, kre, kim):
    n = cm.shape[0]
    cols = ub.shape[1]
    tile = min(DFT_TILE, n)
    return pl.pallas_call(
        functools.partial(_dft_fwd_kernel, tile=tile),
        grid=(n // tile, cols // D_HY),
        in_specs=[pl.BlockSpec((tile, n), lambda f, c: (f, 0)),
                  pl.BlockSpec((tile, n), lambda f, c: (f, 0)),
                  pl.BlockSpec((n, D_HY), lambda f, c: (0, c)),
                  pl.BlockSpec((tile, D_HY), lambda f, c: (f, 0)),
                  pl.BlockSpec((tile, D_HY), lambda f, c: (f, 0))],
        out_specs=[pl.BlockSpec((tile, D_HY), lambda f, c: (f, c))] * 2,
        out_shape=[jax.ShapeDtypeStruct((n, cols), BF16)] * 2,
        compiler_params=_cparams("parallel", "parallel"),
        name="dft_fwd",
    )(cm, sm, ub, kre, kim)


def _dft_inv_kernel(ci_ref, si_ref, yre_ref, yim_ref, u_ref, x0_ref, skip_ref, o_ref, *, tile):
    y_cos = jnp.dot(ci_ref[...], yre_ref[...], preferred_element_type=F32)
    y_sin = jnp.dot(si_ref[...], yim_ref[...], preferred_element_type=F32)
    row = lax.broadcasted_iota(jnp.int32, y_cos.shape, 0) + pl.program_id(0) * tile
    nyq = jnp.where(row % 2 == 0, 1.0, -1.0) * yim_ref[0:1, :].astype(F32)
    y = y_cos + jnp.where(row == 0, 0.0, y_sin) + nyq
    o_ref[...] = (x0_ref[...] * (y + u_ref[...] * skip_ref[...])).astype(BF16)


def _dft_inv(ci, si, yre, yim, u, x0, skip):
    n = ci.shape[0]
    cols = yre.shape[1]
    tile = min(DFT_TILE, n)
    return pl.pallas_call(
        functools.partial(_dft_inv_kernel, tile=tile),
        grid=(n // tile, cols // D_HY),
        in_specs=[pl.BlockSpec((tile, n), lambda t, c: (t, 0)),
                  pl.BlockSpec((tile, n), lambda t, c: (t, 0)),
                  pl.BlockSpec((n, D_HY), lambda t, c: (0, c)),
                  pl.BlockSpec((n, D_HY), lambda t, c: (0, c)),
                  pl.BlockSpec((tile, D_HY), lambda t, c: (t, c)),
                  pl.BlockSpec((tile, D_HY), lambda t, c: (t, c)),
                  pl.BlockSpec((1, D_HY), lambda t, c: (0, 0))],
        out_specs=pl.BlockSpec((tile, D_HY), lambda t, c: (t, c)),
        out_shape=jax.ShapeDtypeStruct((n, cols), BF16),
        compiler_params=_cparams("parallel", "parallel"),
        name="dft_inv",
    )(ci, si, yre, yim, u, x0, skip.reshape(1, D_HY))


def _fft_tables(n):
    n1 = 2 * n // FFT_BLOCK
    half = n1 // 2
    nf = half + 1
    groups = FFT_BLOCK // FFT_SLAB
    f1 = jnp.arange(nf, dtype=F32)[:, None]
    s1 = jnp.arange(half, dtype=F32)[None, :]
    ang1 = (2.0 * math.pi / n1) * f1 * s1
    eye = jnp.eye(FFT_SLAB, dtype=F32)
    m1 = jnp.concatenate([jnp.kron(jnp.cos(ang1), eye), jnp.kron(-jnp.sin(ang1), eye)], axis=0)
    m1_inv = jnp.concatenate([jnp.kron(jnp.cos(ang1).T, eye), jnp.kron(-jnp.sin(ang1).T, eye)], axis=1)
    k = jnp.arange(FFT_BLOCK, dtype=F32)
    ang2 = (2.0 * math.pi / FFT_BLOCK) * k[:, None] * k[None, :]
    c2, d2 = jnp.cos(ang2), jnp.sin(ang2)
    g2 = jnp.block([[c2, d2], [-d2, c2]])
    g2_inv = jnp.block([[c2, -d2], [d2, c2]])
    psi = (2.0 * math.pi / (2 * n)) * f1 * k[None, :]
    tw = jnp.stack([jnp.cos(psi), jnp.sin(psi)])
    tw_s2 = jnp.broadcast_to(tw[..., None], (2, nf, FFT_BLOCK, LANES))
    tw_s1 = tw.reshape(2, nf, groups, FFT_SLAB).transpose(2, 0, 1, 3).reshape(groups, 2, nf * FFT_SLAB)
    tw_s1 = jnp.broadcast_to(tw_s1[..., None], (groups, 2, nf * FFT_SLAB, LANES))
    return dict(n=n, half=half, nf=nf, m1=m1.astype(BF16), m1_inv=m1_inv.astype(BF16), g2=g2.astype(BF16),
                g2_inv=g2_inv.astype(BF16), tw_s1=tw_s1, tw_s2=tw_s2)


def _lane_tile(x, width):
    return jnp.concatenate([x] * (width // x.shape[-1]), axis=-1) if width != x.shape[-1] else x


def _fft_s1_kernel(u_ref, m_ref, tw_ref, o_ref):
    h, slab, tc = u_ref.shape
    r = jnp.dot(m_ref[...], u_ref[...].reshape(h * slab, tc), preferred_element_type=F32)
    rows = r.shape[0] // 2
    re, im = r[:rows], r[rows:]
    cs = _lane_tile(tw_ref[0], tc)
    sn = _lane_tile(tw_ref[1], tc)
    o_ref[0] = (re * cs + im * sn).astype(BF16).reshape(rows // slab, slab, tc)
    o_ref[1] = (im * cs - re * sn).astype(BF16).reshape(rows // slab, slab, tc)


def _fft_s1(ub, tabs):
    n, cols = ub.shape
    half, nf = tabs["half"], tabs["nf"]
    groups = FFT_BLOCK // FFT_SLAB
    tc = min(FFT_COLS, cols)
    out = pl.pallas_call(
        _fft_s1_kernel,
        grid=(groups, cols // tc),
        in_specs=[pl.BlockSpec((half, None, FFT_SLAB, tc), lambda m, c: (0, m, 0, c)),
                  _resident(tabs["m1"].shape),
                  pl.BlockSpec((None, 2, nf * FFT_SLAB, LANES), lambda m, c: (m, 0, 0, 0))],
        out_specs=pl.BlockSpec((2, nf, None, FFT_SLAB, tc), lambda m, c: (0, 0, m, 0, c)),
        out_shape=jax.ShapeDtypeStruct((2, nf, groups, FFT_SLAB, cols), BF16),
        compiler_params=_cparams("parallel", "parallel"),
        name="fft_stage1",
    )(ub.reshape(half, groups, FFT_SLAB, cols), tabs["m1"], tabs["tw_s1"])
    return out.reshape(2, nf, FFT_BLOCK, cols)


def _fft_mid_kernel(a_ref, g_ref, gi_ref, k_ref, tw_ref, b_ref, *, gf, half, scale):
    g2 = g_ref[...]
    g2_inv = gi_ref[...]
    tc = a_ref.shape[-1]
    for j in range(gf):
        x = jnp.dot(g2, jnp.concatenate([a_ref[0, j], a_ref[1, j]], axis=0), preferred_element_type=F32)
        xre, xim = x[:FFT_BLOCK], x[FFT_BLOCK:]
        kre, kim = k_ref[0, j], k_ref[1, j]
        f1 = pl.program_id(0) * gf + j
        w = jnp.where(jnp.logical_or(f1 == 0, f1 == half), scale, 2.0 * scale)
        y = jnp.concatenate([w * (xre * kre - xim * kim), w * (xre * kim + xim * kre)], axis=0).astype(BF16)
        b = jnp.dot(g2_inv, y, preferred_element_type=F32)
        bre, bim = b[:FFT_BLOCK], b[FFT_BLOCK:]
        cs = _lane_tile(tw_ref[0, j], tc)
        sn = _lane_tile(tw_ref[1, j], tc)
        b_ref[0, j] = (bre * cs - bim * sn).astype(BF16)
        b_ref[1, j] = (bre * sn + bim * cs).astype(BF16)


def _fft_group(nf):
    return max(g for g in range(1, 12) if nf % g == 0)


def _fft_mid(a, tabs, spec):
    _, nf, _, cols = a.shape
    gf = _fft_group(nf)
    blk = lambda w, idx: pl.BlockSpec((2, gf, FFT_BLOCK, w), idx)
    return pl.pallas_call(
        functools.partial(_fft_mid_kernel, gf=gf, half=tabs["half"], scale=0.5 / tabs["n"]),
        grid=(nf // gf, cols // D_HY),
        in_specs=[blk(D_HY, lambda g, c: (0, g, 0, c)), _resident(tabs["g2"].shape),
                  _resident(tabs["g2_inv"].shape), blk(D_HY, lambda g, c: (0, g, 0, 0)),
                  blk(LANES, lambda g, c: (0, g, 0, 0))],
        out_specs=blk(D_HY, lambda g, c: (0, g, 0, c)),
        out_shape=jax.ShapeDtypeStruct(a.shape, BF16),
        compiler_params=_cparams("parallel", "parallel"),
        name="fft_mid",
    )(a, tabs["g2"], tabs["g2_inv"], spec, tabs["tw_s2"])


def _fft_s2_filter_kernel(a_ref, g_ref, k_ref, *, gf):
    g2 = g_ref[...]
    for j in range(gf):
        x = jnp.dot(g2, jnp.concatenate([a_ref[0, j], a_ref[1, j]], axis=0), preferred_element_type=F32)
        xre, xim = x[:FFT_BLOCK], x[FFT_BLOCK:]
        k_ref[0, j] = xre[:, :D_HY] + xre[:, D_HY:]
        k_ref[1, j] = xim[:, :D_HY] - xim[:, D_HY:]


def _fft_s2_filter(a, tabs):
    _, nf, _, cols = a.shape
    gf = _fft_group(nf)
    return pl.pallas_call(
        functools.partial(_fft_s2_filter_kernel, gf=gf),
        grid=(nf // gf,),
        in_specs=[pl.BlockSpec((2, gf, FFT_BLOCK, cols), lambda g: (0, g, 0, 0)), _resident(tabs["g2"].shape)],
        out_specs=pl.BlockSpec((2, gf, FFT_BLOCK, D_HY), lambda g: (0, g, 0, 0)),
        out_shape=jax.ShapeDtypeStruct((2, nf, FFT_BLOCK, D_HY), F32),
        compiler_params=_cparams("parallel"),
        name="fft_stage2_filter",
    )(a, tabs["g2"])


def _fft_s1_inv_kernel(b_ref, m_ref, u_ref, x0_ref, skip_ref, o_ref):
    two, nf, slab, tc = b_ref.shape
    h = u_ref.shape[0]
    y = jnp.dot(m_ref[...], b_ref[...].reshape(two * nf * slab, tc), preferred_element_type=F32)
    u = u_ref[...].reshape(h * slab, tc).astype(F32)
    x0 = x0_ref[...].reshape(h * slab, tc).astype(F32)
    res = (x0 * (y + u * _lane_tile(skip_ref[...], tc))).astype(BF16)
    for bi in range(o_ref.shape[0]):
        o_ref[bi] = res[:, bi * D_HY:(bi + 1) * D_HY].reshape(h, slab, D_HY)


def _fft_s1_inv(b, tabs, u, x0, skip):
    _, nf, _, cols = b.shape
    n, half = tabs["n"], tabs["half"]
    groups = FFT_BLOCK // FFT_SLAB
    tc = min(FFT_COLS, cols)
    rows4 = lambda a: a.reshape(half, groups, FFT_SLAB, cols)
    tspec = pl.BlockSpec((half, None, FFT_SLAB, tc), lambda m, c: (0, m, 0, c))
    out = pl.pallas_call(
        _fft_s1_inv_kernel,
        grid=(groups, cols // tc),
        in_specs=[pl.BlockSpec((2, nf, None, FFT_SLAB, tc), lambda m, c: (0, 0, m, 0, c)),
                  _resident(tabs["m1_inv"].shape), tspec, tspec,
                  pl.BlockSpec((1, D_HY), lambda m, c: (0, 0))],
        out_specs=pl.BlockSpec((tc // D_HY, half, None, FFT_SLAB, D_HY), lambda m, c: (c, 0, m, 0, 0)),
        out_shape=jax.ShapeDtypeStruct((cols // D_HY, half, groups, FFT_SLAB, D_HY), BF16),
        compiler_params=_cparams("parallel", "parallel"),
        name="fft_stage1_inv",
    )(b.reshape(2, nf, groups, FFT_SLAB, cols), tabs["m1_inv"], rows4(u), rows4(x0), skip.reshape(1, D_HY))
    return out.reshape(cols // D_HY, n, D_HY)


def _hyena_fft(zh, bsz, conv_w, conv_b, filt, tabs, skip):
    ub, x0 = _hyena_pre(zh, conv_w, conv_b, bsz)
    kk = _hyena_filter_taps(zh.shape[0], *filt)
    spec = _fft_s2_filter(_fft_s1(kk, tabs), tabs)
    return _fft_s1_inv(_fft_mid(_fft_s1(ub, tabs), tabs, spec), tabs, ub, x0, skip)


def _hyena(zh, bsz, conv_w, conv_b, filt, dft, skip):
    cm, sm = dft
    n = zh.shape[0]
    ub, x0 = _hyena_pre(zh, conv_w, conv_b, bsz)
    kk = _hyena_filter_taps(n, *filt)
    kre, kim = _dft_filter(cm, sm, kk)
    yre, yim = _dft_fwd(cm, sm, ub, kre, kim)
    y = _dft_inv(cm, sm, yre, yim, ub, x0, skip)
    return y.reshape(n, bsz, D_HY).transpose(1, 0, 2)


def _attend_block(q, kv, sink_ref, masks):
    k_low = [kv[:, g * LANES:(g + 1) * LANES] for g in range(N_KVH)]
    ones = jnp.ones((kv.shape[0], LANES), BF16)
    v_aug = [jnp.concatenate([kv[:, (N_KVH + g) * LANES:(N_KVH + g + 1) * LANES], ones], axis=1)
             for g in range(N_KVH)]
    nq = q.shape[0]
    windowed = masks is not None
    if windowed:
        valid_prev, valid_next = masks
    cols_per_g = D_ATTN // LANES // N_KVH
    out_cols = []
    for g in range(N_KVH):
        cols = [q[:, c * LANES:(c + 1) * LANES] for c in range(g * cols_per_g, (g + 1) * cols_per_g)]
        qg = jnp.concatenate(cols + [pltpu.roll(cq, HEAD_DIM, axis=1) for cq in cols], axis=0).astype(BF16)
        s_all = lax.dot_general(qg, k_low[g], (((1,), (1,)), ((), ())), preferred_element_type=F32)
        es, sinks = [], []
        for hb in range(2 * cols_per_g):
            h = 2 * (g * cols_per_g + hb % cols_per_g) + hb // cols_per_g
            s = s_all[hb * nq:(hb + 1) * nq]
            if windowed:
                s = jnp.concatenate([
                    jnp.where(valid_prev, s[:, :ATT_BLOCK], NEG_INF),
                    s[:, ATT_BLOCK:2 * ATT_BLOCK],
                    jnp.where(valid_next, s[:, 2 * ATT_BLOCK:3 * ATT_BLOCK], NEG_INF),
                    s[:, 3 * ATT_BLOCK:]], axis=1)
            sk = sink_ref[h:h + 1, 0:1] * LOG2E
            m = jnp.maximum(jnp.max(s, axis=-1, keepdims=True), sk)
            es.append(jnp.exp2(s - m).astype(BF16))
            sinks.append(jnp.exp2(sk - m))
        o_all = jnp.dot(jnp.concatenate(es, axis=0), v_aug[g], preferred_element_type=F32)
        outs = []
        for hb in range(2 * cols_per_g):
            o = o_all[hb * nq:(hb + 1) * nq]
            outs.append(o[:, :LANES] / (o[:, LANES:] + sinks[hb]))
        out_cols += [outs[ci] + pltpu.roll(outs[cols_per_g + ci], HEAD_DIM, axis=1) for ci in range(cols_per_g)]
    return out_cols


def _attn_kernel(*refs, windowed, n_sub):
    if windowed:
        (q_ref, kvp_ref, kvc_ref, kvn_ref, kvx_ref, sink_ref, o_ref) = refs
        i = pl.program_id(1)
        last = pl.num_programs(1) - 1
        r = lax.broadcasted_iota(jnp.int32, (ATT_BLOCK, ATT_BLOCK), 0)
        j = lax.broadcasted_iota(jnp.int32, (ATT_BLOCK, ATT_BLOCK), 1)
        cur = kvc_ref[0]
        blocks = ([kvp_ref[0]] + [cur[n * ATT_BLOCK:(n + 1) * ATT_BLOCK] for n in range(n_sub)]
                  + [kvn_ref[0]])
    else:
        (q_ref, kvx_ref, sink_ref, o_ref) = refs
    ctx = kvx_ref[0]
    for n in range(n_sub):
        rows = slice(n * ATT_BLOCK, (n + 1) * ATT_BLOCK)
        q = q_ref[0, rows, :].astype(F32)
        if windowed:
            no_prev = jnp.where(i == 0, ATT_BLOCK, 0) if n == 0 else 0
            no_next = jnp.where(i == last, ATT_BLOCK, 0) if n == n_sub - 1 else 0
            masks = (j >= r + no_prev, j <= r - no_next)
            kv = jnp.concatenate(blocks[n:n + 3] + [ctx], axis=0)
        else:
            masks = None
            kv = ctx
        for c, col in enumerate(_attend_block(q, kv, sink_ref, masks)):
            o_ref[0, rows, c * LANES:(c + 1) * LANES] = col.astype(BF16)


def _attention(q, kv, kvx, sink, windowed):
    bsz, lq, _ = q.shape
    n_ctx, kv_w = kvx.shape[1:]
    nb = lq // ATT_BLOCK
    n_sub = max(n for n in range(1, ATT_STEP + 1) if nb % n == 0)
    step = n_sub * ATT_BLOCK
    sink_t = jnp.broadcast_to(sink.reshape(N_QH, 1), (N_QH, LANES))
    qspec = pl.BlockSpec((1, step, D_ATTN), lambda b, i: (b, i, 0))
    xspec = pl.BlockSpec((1, n_ctx, kv_w), lambda b, i: (b, 0, 0))
    sspec = pl.BlockSpec((N_QH, LANES), lambda b, i: (0, 0))
    if windowed:
        prev = pl.BlockSpec((1, ATT_BLOCK, kv_w), lambda b, i: (b, jnp.maximum(n_sub * i - 1, 0), 0))
        cur = pl.BlockSpec((1, step, kv_w), lambda b, i: (b, i, 0))
        nxt = pl.BlockSpec((1, ATT_BLOCK, kv_w), lambda b, i: (b, jnp.minimum(n_sub * (i + 1), nb - 1), 0))
        in_specs = [qspec, prev, cur, nxt, xspec, sspec]
        args = (q, kv, kv, kv, kvx, sink_t)
    else:
        in_specs = [qspec, xspec, sspec]
        args = (q, kvx, sink_t)
    return pl.pallas_call(
        functools.partial(_attn_kernel, windowed=windowed, n_sub=n_sub),
        grid=(bsz, lq // step),
        in_specs=in_specs,
        out_specs=pl.BlockSpec((1, step, D_ATTN), lambda b, i: (b, i, 0)),
        out_shape=jax.ShapeDtypeStruct(q.shape, BF16),
        compiler_params=_cparams("parallel", "parallel"),
        name="attention",
    )(*args)


def _rope_tables(n_lat):
    rows = n_lat // GRID_W
    r = jnp.repeat(jnp.arange(rows, dtype=F32), GRID_W)
    col = jnp.tile(jnp.arange(GRID_W, dtype=F32), rows)
    inv = ROPE_THETA ** (-jnp.arange(ROPE_PAIRS_AXIS, dtype=F32) / ROPE_PAIRS_AXIS)
    ang = jnp.concatenate([r[:, None] * inv, col[:, None] * inv], axis=-1)
    cos, sin = jnp.cos(ang), jnp.sin(ang)
    zero = jnp.zeros_like(sin)
    reps = D_ATTN // HEAD_DIM
    cos_t = jnp.tile(jnp.concatenate([cos, cos], axis=-1), (1, reps))
    sin_a = jnp.tile(jnp.concatenate([-sin, zero], axis=-1), (1, reps))
    sin_b = jnp.tile(jnp.concatenate([zero, sin], axis=-1), (1, reps))
    return cos_t, sin_a, sin_b


def kernel(x, c, ctx, c_ctx, w_mod, b_mod, norm_g, ffn_w1, ffn_w2, w_in, w_out, lru_conv_w, lru_conv_b,
           lru_wa, lru_ba, lru_wx, lru_bx, lru_lam, hy_conv_w, hy_conv_b, hy_fw0, hy_fb0, hy_fw_in,
           hy_fb_in, hy_freq, hy_fw_last, hy_skip, attn_sink, final_g):
    bsz, n_lat, d = x.shape
    n_ctx = ctx.shape[1]
    depth = w_mod.shape[0]
    assert n_lat % ATT_BLOCK == 0 and n_ctx % ATT_BLOCK == 0 and n_lat % GRID_W == 0
    assert ATT_BLOCK % SCAN_GROUP == 0 and ATT_BLOCK % LANES == 0

    mod_rows = -(-(bsz + 1) // (2 * SUBLANES)) * (2 * SUBLANES)
    c_rows = jnp.zeros((mod_rows, d), F32).at[:bsz].set(c).at[bsz].set(c_ctx)
    mod_all = _modulation(c_rows, w_mod, b_mod).reshape(depth, mod_rows, N_MOD, d)
    lat_row = lambda b: b
    ctx_row = lambda b: bsz

    rope_tabs = _rope_tables(n_lat)
    fft_lat = _fft_tables(n_lat)
    w1_b = ffn_w1.astype(BF16)
    w2_b = ffn_w2.astype(BF16)
    w_in_b = w_in.astype(BF16)
    w_out_b = w_out.astype(BF16)

    xc = ctx
    for l in range(depth):
        need_ctx = l < depth - 1
        mod = mod_all[l]
        filt = (hy_fw0[l], hy_fb0[l], hy_fw_in[l], hy_fb_in[l], hy_freq[l], hy_fw_last[l])

        x = _ffn(x, mod, lat_row, norm_g[l, 0], w1_b, w2_b, l, 0, 0)
        xc = _ffn(xc, mod, ctx_row, norm_g[l, 0], w1_b, w2_b, l, 0, 0)

        xl, gl, zh, q, kv = _input_proj(x, mod, lat_row, norm_g[l, 1], w_in_b, l, rope_tabs)
        xlc, glc, zhc, qc, kvc = _input_proj(xc, mod, ctx_row, norm_g[l, 1], w_in_b, l, None)

        y_lru, yc_lru = _rglru(xl, gl, xlc, glc, lru_conv_w[l], lru_conv_b[l], lru_wa[l], lru_ba[l],
                               lru_wx[l], lru_bx[l], lru_lam[l], bsz)
        y_hy = _hyena_fft(zh, bsz, hy_conv_w[l], hy_conv_b[l], filt, fft_lat, hy_skip[l])
        y_att = _attention(q, kv, kvc, attn_sink[l], True)
        x = _ffn(x, mod, lat_row, norm_g[l, 2], w1_b, w2_b, l, 1, 6, mixer=(y_lru, y_hy, y_att, w_out_b),
                 final_g=None if need_ctx else final_g)

        if need_ctx:
            yc_hy = _hyena(zhc, bsz, hy_conv_w[l], hy_conv_b[l], filt, _dft_matrices(n_ctx), hy_skip[l])
            yc_att = _attention(qc, None, kvc, attn_sink[l], False)
            xc = _ffn(xc, mod, ctx_row, norm_g[l, 2], w1_b, w2_b, l, 1, 6,
                      mixer=(yc_lru, yc_hy, yc_att, w_out_b))
    return x
```
